```python
import jax, jax.numpy as jnp
from jax import lax
import numpy as np

D_MODEL = 1024
BATCH = 8
SEQ = 4096
DEPTH = 1

MEM_LEN = 256
D_MIX = D_MODEL
DSA_HEADS = 8
DSA_HEAD_DIM = 64
IDX_HEADS = 8
IDX_DIM = 32
TOPK_MAX = 256
Q_BLOCK = 128
GLA_HEADS = 4
GLA_DK = 64
GLA_DV = 128
GLA_GATE_RANK = 16
GLA_GATE_TEMP = 16.0
GLA_CHUNK = 64
ROPE_THETA = 500000.0
ROPE_FRACTION = 4
XATTN_HEADS = 4
XATTN_HEAD_DIM = D_MODEL // XATTN_HEADS
PEER_N_KEYS = 128
PEER_N_EXPERTS = PEER_N_KEYS * PEER_N_KEYS
PEER_HEADS = 8
PEER_D_KEY = 256
PEER_TOPK = 16
PEER_BLOCK = 128
LN_EPS = 1e-5
RMS_EPS = 1e-6
DEEPNORM_ALPHA = (2.0 * DEPTH) ** 0.25
DEEPNORM_BETA = (8.0 * DEPTH) ** -0.25
IN_SPLITS = (
    DSA_HEADS * DSA_HEAD_DIM,
    DSA_HEADS * DSA_HEAD_DIM,
    DSA_HEADS * DSA_HEAD_DIM,
    IDX_HEADS * IDX_DIM,
    IDX_DIM,
    IDX_HEADS,
    GLA_HEADS * GLA_DK,
    GLA_HEADS * GLA_DK,
    GLA_HEADS * GLA_DV,
    GLA_GATE_RANK,
    GLA_HEADS * GLA_DV,
)
IN_IS_VALUE = (False, False, True, False, False, False, False, False, True, False, False)
IN_WIDTH = sum(IN_SPLITS)

kernel_name = "hybrid_dsa_gla_peer_deepnorm"


def layer_norm(x, g, b):
    xf = x.astype(jnp.float32)
    mu = jnp.mean(xf, axis=-1, keepdims=True)
    var = jnp.mean(jnp.square(xf - mu), axis=-1, keepdims=True)
    return ((xf - mu) * lax.rsqrt(var + LN_EPS) * g.astype(jnp.float32) + b.astype(jnp.float32)).astype(x.dtype)


def rms_norm(x, g):
    xf = x.astype(jnp.float32)
    return xf * lax.rsqrt(jnp.mean(jnp.square(xf), axis=-1, keepdims=True) + RMS_EPS) * g.astype(jnp.float32)


def rotary_partial(x, positions):
    d = x.shape[-1]
    r = d // ROPE_FRACTION
    half = r // 2
    inv_freq = ROPE_THETA ** (-jnp.arange(half, dtype=jnp.float32) / half)
    ang = positions.astype(jnp.float32)[..., None] * inv_freq
    cos = jnp.cos(ang)[:, :, None, :]
    sin = jnp.sin(ang)[:, :, None, :]
    xf = x.astype(jnp.float32)
    x1, x2, x_pass = xf[..., :half], xf[..., half:r], xf[..., r:]
    out = jnp.concatenate([x1 * cos - x2 * sin, x2 * cos + x1 * sin, x_pass], axis=-1)
    return out.astype(x.dtype)


def dsa_attention(q, k, v, q_idx, k_idx, w_idx):
    B, S = q.shape[0], q.shape[1]
    n_sel = min(TOPK_MAX, S // 4)
    nb = S // Q_BLOCK

    def to_blocks(a):
        return jnp.moveaxis(a.reshape((B, nb, Q_BLOCK) + a.shape[2:]), 1, 0)

    k_idx_f = k_idx.astype(jnp.float32)
    key_pos = jnp.arange(S)
    b_ix = jnp.arange(B)[:, None, None]
    idx_scale = IDX_DIM ** -0.5
    w_scale = IDX_HEADS ** -0.5
    attn_scale = DSA_HEAD_DIM ** -0.5

    def block(args):
        blk, qb, qib, wb = args
        q_pos = blk * Q_BLOCK + jnp.arange(Q_BLOCK)
        causal = key_pos[None, :] <= q_pos[:, None]
        dots = jnp.einsum('bqhd,bsd->bqhs', qib.astype(jnp.float32), k_idx_f) * idx_scale
        score = jnp.einsum('bqh,bqhs->bqs', wb.astype(jnp.float32) * w_scale, jax.nn.relu(dots))
        score = jnp.where(causal[None], score, -jnp.inf)
        _, sel = lax.top_k(score, n_sel)
        valid = sel <= q_pos[None, :, None]
        k_sel = k[b_ix, sel]
        v_sel = v[b_ix, sel]
        logits = jnp.einsum('bqhd,bqkhd->bqhk', qb.astype(jnp.float32), k_sel.astype(jnp.float32)) * attn_scale
        logits = jnp.where(valid[:, :, None, :], logits, -jnp.inf)
        p = jax.nn.softmax(logits, axis=-1)
        return jnp.einsum('bqhk,bqkhd->bqhd', p.astype(v.dtype), v_sel)

    outs = lax.map(block, (jnp.arange(nb), to_blocks(q), to_blocks(q_idx), to_blocks(w_idx)))
    return jnp.moveaxis(outs, 0, 1).reshape(B, S, DSA_HEADS * DSA_HEAD_DIM)


def gla_chunked(q, k, v, log_a):
    B, S, H, dk = q.shape
    dv = v.shape[-1]
    C = GLA_CHUNK
    N = S // C

    def chunks(a):
        return a.astype(jnp.float32).reshape(B, N, C, H, a.shape[-1]).transpose(0, 3, 1, 2, 4)

    qc = chunks(q) * dk ** -0.5
    kc, vc, ac = chunks(k), chunks(v), chunks(log_a)
    bcum = jnp.cumsum(ac, axis=3)
    b_last = bcum[:, :, :, -1:, :]
    q_dec = qc * jnp.exp(bcum)
    k_inv = kc * jnp.exp(-bcum)
    k_to_end = kc * jnp.exp(b_last - bcum)
    causal = jnp.tril(jnp.ones((C, C), dtype=bool))
    attn = jnp.where(causal, jnp.einsum('bhncd,bhnsd->bhncs', q_dec, k_inv), 0.0)
    o_intra = jnp.einsum('bhncs,bhnse->bhnce', attn, vc)
    chunk_kv = jnp.einsum('bhnsd,bhnse->bhnde', k_to_end, vc)
    chunk_decay = jnp.exp(b_last[:, :, :, 0, :])

    def step(state, inp):
        decay, kv = inp
        return decay[..., None] * state + kv, state

    init = jnp.zeros((B, H, dk, dv), jnp.float32)
    _, prev = lax.scan(step, init, (jnp.moveaxis(chunk_decay, 2, 0), jnp.moveaxis(chunk_kv, 2, 0)))
    prev = jnp.moveaxis(prev, 0, 2)
    o_inter = jnp.einsum('bhncd,bhnde->bhnce', q_dec, prev)
    return (o_intra + o_inter).transpose(0, 2, 3, 1, 4).reshape(B, S, H, dv)


def hybrid_mixer(x, positions, w_in, gate_up, gate_bias, norm_g, w_out):
    B, S, _ = x.shape
    proj = jnp.einsum('bsd,de->bse', x, w_in)
    splits = np.cumsum(IN_SPLITS)[:-1].tolist()
    (q, k, v, q_idx, k_idx, w_idx, g_q, g_k, g_v, g_lr, g_r) = jnp.split(proj, splits, axis=-1)

    def heads(a, n):
        return a.reshape(B, S, n, a.shape[-1] // n)

    q = rotary_partial(heads(q, DSA_HEADS), positions)
    k = rotary_partial(heads(k, DSA_HEADS), positions)
    q_idx = rotary_partial(heads(q_idx, IDX_HEADS), positions)
    k_idx = rotary_partial(k_idx[:, :, None, :], positions)[:, :, 0, :]
    y_dsa = dsa_attention(q, k, heads(v, DSA_HEADS), q_idx, k_idx, w_idx)

    log_a = jax.nn.log_sigmoid((g_lr @ gate_up + gate_bias).astype(jnp.float32)) / GLA_GATE_TEMP
    o = gla_chunked(heads(g_q, GLA_HEADS), heads(g_k, GLA_HEADS), heads(g_v, GLA_HEADS),
                    log_a.reshape(B, S, GLA_HEADS, GLA_DK))
    o = rms_norm(o, norm_g).reshape(B, S, GLA_HEADS * GLA_DV).astype(x.dtype)
    y_gla = o * jax.nn.silu(g_r)

    y = jnp.concatenate([y_dsa, y_gla], axis=-1)
    return y @ w_out


def memory_cross_attention(x, mem, w_q, w_k, w_v, w_o):
    B, S, _ = x.shape
    M = mem.shape[1]
    q = (x @ w_q).reshape(B, S, XATTN_HEADS, XATTN_HEAD_DIM)
    k = (mem @ w_k).reshape(B, M, XATTN_HEADS, XATTN_HEAD_DIM)
    v = (mem @ w_v).reshape(B, M, XATTN_HEADS, XATTN_HEAD_DIM)
    logits = jnp.einsum('bshd,bmhd->bhsm', q.astype(jnp.float32), k.astype(jnp.float32)) * XATTN_HEAD_DIM ** -0.5
    p = jax.nn.softmax(logits, axis=-1).astype(v.dtype)
    o = jnp.einsum('bhsm,bmhd->bshd', p, v).reshape(B, S, D_MODEL)
    return o @ w_o


def peer(x, w_query, sub_keys_1, sub_keys_2, expert_down, expert_up):
    B, S, D = x.shape
    half = PEER_D_KEY // 2
    q = (x @ w_query).reshape(B, S, PEER_HEADS, PEER_D_KEY).astype(jnp.float32)
    s1 = jnp.einsum('bshd,nd->bshn', q[..., :half], sub_keys_1.astype(jnp.float32))
    s2 = jnp.einsum('bshd,nd->bshn', q[..., half:], sub_keys_2.astype(jnp.float32))
    v1, i1 = lax.top_k(s1, PEER_TOPK)
    v2, i2 = lax.top_k(s2, PEER_TOPK)
    n_cand = PEER_TOPK * PEER_TOPK
    cand = (v1[..., :, None] + v2[..., None, :]).reshape(B, S, PEER_HEADS, n_cand)
    cand_idx = (i1[..., :, None] * PEER_N_KEYS + i2[..., None, :]).reshape(B, S, PEER_HEADS, n_cand)
    top_s, pos = lax.top_k(cand, PEER_TOPK)
    experts = jnp.take_along_axis(cand_idx, pos, axis=-1)
    gates = jax.nn.softmax(top_s, axis=-1)

    T = B * S
    nb = T // PEER_BLOCK
    hk = PEER_HEADS * PEER_TOPK
    xb = x.reshape(nb, PEER_BLOCK, D)
    eb = experts.reshape(nb, PEER_BLOCK, hk)
    gb = gates.reshape(nb, PEER_BLOCK, hk)

    def block(args):
        xt, et, gt = args
        u = expert_down[et]
        act = jax.nn.gelu(jnp.einsum('td,tkd->tk', xt, u).astype(jnp.float32), approximate=False)
        vv = expert_up[et]
        return jnp.einsum('tk,tkd->td', (gt * act).astype(xt.dtype), vv)

    y = lax.map(block, (xb, eb, gb))
    return y.reshape(B, S, D)


def setup_inputs(seed: int = 0) -> dict:
    key = jax.random.key(seed)
    ks = jax.random.split(key, 24)
    f32 = jnp.float32
    nrm = lambda k, shape, scale: jax.random.normal(k, shape, f32) * scale
    col_scale = jnp.concatenate([jnp.full((n,), DEEPNORM_BETA if is_v else 1.0, f32)
                                 for n, is_v in zip(IN_SPLITS, IN_IS_VALUE)])
    return {
        "x": nrm(ks[0], (BATCH, SEQ, D_MODEL), 1.0),
        "positions": jnp.broadcast_to(jnp.arange(SEQ, dtype=jnp.int32)[None, :], (BATCH, SEQ)),
        "mem": nrm(ks[1], (BATCH, MEM_LEN, D_MODEL), 1.0),
        "w_in": nrm(ks[2], (DEPTH, D_MODEL, IN_WIDTH), D_MODEL ** -0.5) * col_scale,
        "gla_gate_up": nrm(ks[3], (DEPTH, GLA_GATE_RANK, GLA_HEADS * GLA_DK), GLA_GATE_RANK ** -0.5),
        "gla_gate_bias": nrm(ks[4], (DEPTH, GLA_HEADS * GLA_DK), 0.1),
        "gla_norm_g": 1.0 + nrm(ks[5], (DEPTH, GLA_DV), 0.01),
        "w_out": nrm(ks[6], (DEPTH, D_MIX, D_MODEL), D_MIX ** -0.5) * DEEPNORM_BETA,
        "ln_mix_g": 1.0 + nrm(ks[7], (DEPTH, D_MODEL), 0.01),
        "ln_mix_b": nrm(ks[8], (DEPTH, D_MODEL), 0.01),
        "xattn_w_q": nrm(ks[9], (DEPTH, D_MODEL, D_MODEL), D_MODEL ** -0.5),
        "xattn_w_k": nrm(ks[10], (DEPTH, D_MODEL, D_MODEL), D_MODEL ** -0.5),
        "xattn_w_v": nrm(ks[11], (DEPTH, D_MODEL, D_MODEL), D_MODEL ** -0.5) * DEEPNORM_BETA,
        "xattn_w_o": nrm(ks[12], (DEPTH, D_MODEL, D_MODEL), D_MODEL ** -0.5) * DEEPNORM_BETA,
        "ln_mem_g": 1.0 + nrm(ks[13], (DEPTH, D_MODEL), 0.01),
        "ln_mem_b": nrm(ks[14], (DEPTH, D_MODEL), 0.01),
        "peer_w_query": nrm(ks[15], (DEPTH, D_MODEL, PEER_HEADS * PEER_D_KEY), D_MODEL ** -0.5),
        "peer_sub_keys_1": nrm(ks[16], (DEPTH, PEER_N_KEYS, PEER_D_KEY // 2), (PEER_D_KEY // 2) ** -0.5),
        "peer_sub_keys_2": nrm(ks[17], (DEPTH, PEER_N_KEYS, PEER_D_KEY // 2), (PEER_D_KEY // 2) ** -0.5),
        "peer_expert_down": nrm(ks[18], (DEPTH, PEER_N_EXPERTS, D_MODEL), D_MODEL ** -0.5),
        "peer_expert_up": nrm(ks[19], (DEPTH, PEER_N_EXPERTS, D_MODEL), PEER_HEADS ** -0.5) * DEEPNORM_BETA,
        "ln_ffn_g": 1.0 + nrm(ks[20], (DEPTH, D_MODEL), 0.01),
        "ln_ffn_b": nrm(ks[21], (DEPTH, D_MODEL), 0.01),
    }


def reference(x, positions, mem, w_in, gla_gate_up, gla_gate_bias, gla_norm_g, w_out,
              ln_mix_g, ln_mix_b, xattn_w_q, xattn_w_k, xattn_w_v, xattn_w_o, ln_mem_g, ln_mem_b,
              peer_w_query, peer_sub_keys_1, peer_sub_keys_2, peer_expert_down, peer_expert_up,
              ln_ffn_g, ln_ffn_b):
    h = x
    for l in range(DEPTH):
        mix = hybrid_mixer(h, positions, w_in[l], gla_gate_up[l], gla_gate_bias[l], gla_norm_g[l], w_out[l])
        h = layer_norm(DEEPNORM_ALPHA * h + mix, ln_mix_g[l], ln_mix_b[l])
        ca = memory_cross_attention(h, mem, xattn_w_q[l], xattn_w_k[l], xattn_w_v[l], xattn_w_o[l])
        h = layer_norm(DEEPNORM_ALPHA * h + ca, ln_mem_g[l], ln_mem_b[l])
        ff = peer(h, peer_w_query[l], peer_sub_keys_1[l], peer_sub_keys_2[l],
                  peer_expert_down[l], peer_expert_up[l])
        h = layer_norm(DEEPNORM_ALPHA * h + ff, ln_ffn_g[l], ln_ffn_b[l])
    return h
```

```python
import functools

import jax
import jax.numpy as jnp
import numpy as np
from jax import lax
from jax.experimental import pallas as pl
from jax.experimental.pallas import tpu as pltpu

f32 = jnp.float32
bf16 = jnp.bfloat16
i32 = jnp.int32

DSA_HEADS = 8
DSA_HEAD_DIM = 64
IDX_HEADS = 8
IDX_DIM = 32
TOPK_MAX = 256
GLA_HEADS = 4
GLA_DK = 64
GLA_DV = 128
GLA_GATE_RANK = 16
GLA_GATE_TEMP = 16.0
GLA_CHUNK = 64
ROPE_THETA = 500000.0
ROPE_FRACTION = 4
XATTN_HEADS = 4
PEER_N_KEYS = 128
PEER_HEADS = 8
PEER_D_KEY = 256
PEER_TOPK = 16
LN_EPS = 1e-5
RMS_EPS = 1e-6

LANES = 128
SUBLANES = 8
VMEM_LIMIT = 56 * 1024 * 1024

INT_MIN = -(2 ** 31)
NEG_INF = float("-inf")

W_DSA = DSA_HEADS * DSA_HEAD_DIM
W_IDX = IDX_HEADS * IDX_DIM
W_GQK = GLA_HEADS * GLA_DK
W_GV = GLA_HEADS * GLA_DV
MISC_KI = 0
MISC_WI = IDX_DIM
MISC_LR = IDX_DIM + IDX_HEADS


def _dot(a, b, dims=(((1,), (0,)), ((), ())), precision=None):
    return lax.dot_general(a, b, dims, precision=precision, preferred_element_type=f32)


_NN = (((1,), (0,)), ((), ()))
_NT = (((1,), (1,)), ((), ()))
_TN = (((0,), (0,)), ((), ()))


def _params(*sem):
    return pltpu.CompilerParams(dimension_semantics=sem, vmem_limit_bytes=VMEM_LIMIT)


def _layer_norm(y, g, b):
    mu = jnp.mean(y, axis=-1, keepdims=True)
    yc = y - mu
    var = jnp.mean(yc * yc, axis=-1, keepdims=True)
    return yc * lax.rsqrt(var + LN_EPS) * g + b


def _rot(xb, c, sa, sb, half):
    return xb * c + pltpu.roll(xb, LANES - half, 1) * sa + pltpu.roll(xb, half, 1) * sb


def _proj_kernel(x_ref, w_ref, gup_ref, gb_ref, cq_ref, saq_ref, sbq_ref, ci_ref, sai_ref, sbi_ref,
                 q_ref, kt_ref, v_ref, qi_ref, kit_ref, misc_ref, gq_ref, gk_ref, gv_ref, la_ref, gr_ref):
    x = x_ref[...].astype(bf16)
    tm = x.shape[0]
    cq, saq, sbq = cq_ref[...], saq_ref[...], sbq_ref[...]
    ci, sai, sbi = ci_ref[...], sai_ref[...], sbi_ref[...]
    hq = DSA_HEAD_DIM // ROPE_FRACTION // 2
    hi = IDX_DIM // ROPE_FRACTION // 2
    o = 0
    scale = DSA_HEAD_DIM ** -0.5
    for j in range(W_DSA // LANES):
        a = _dot(x, w_ref[:, o + j * LANES:o + (j + 1) * LANES])
        q_ref[:, j * LANES:(j + 1) * LANES] = (_rot(a, cq, saq, sbq, hq) * scale).astype(bf16)
    o += W_DSA
    for j in range(W_DSA // LANES):
        a = _dot(x, w_ref[:, o + j * LANES:o + (j + 1) * LANES])
        kt_ref[j * LANES:(j + 1) * LANES, :] = _rot(a, cq, saq, sbq, hq).T.astype(bf16)
    o += W_DSA
    v_ref[...] = _dot(x, w_ref[:, o:o + W_DSA]).astype(bf16)
    o += W_DSA
    for j in range(W_IDX // LANES):
        a = _dot(x, w_ref[:, o + j * LANES:o + (j + 1) * LANES])
        qi_ref[:, j * LANES:(j + 1) * LANES] = _rot(a, ci, sai, sbi, hi).astype(bf16)
    o += W_IDX
    m = _dot(x, w_ref[:, o:o + LANES])
    lane = lax.broadcasted_iota(i32, (tm, LANES), 1)
    is_ki = lane < IDX_DIM
    m = _rot(m, jnp.where(is_ki, ci, 1.0), jnp.where(is_ki, sai, 0.0), jnp.where(is_ki, sbi, 0.0), hi)
    misc_ref[...] = m
    kit_ref[...] = m.T[:IDX_DIM, :].astype(bf16)
    z = _dot(m.astype(bf16), gup_ref[...]) + gb_ref[...]
    la_ref[...] = (jnp.minimum(z, 0.0) - jnp.log1p(jnp.exp(-jnp.abs(z)))) / GLA_GATE_TEMP
    o += LANES
    gq_ref[...] = _dot(x, w_ref[:, o:o + W_GQK])
    o += W_GQK
    gk_ref[...] = _dot(x, w_ref[:, o:o + W_GQK])
    o += W_GQK
    gv_ref[...] = _dot(x, w_ref[:, o:o + W_GV]).astype(bf16)
    o += W_GV
    gr_ref[...] = _dot(x, w_ref[:, o:o + W_GV])


def _rot_tables(positions, head_dim):
    r = head_dim // ROPE_FRACTION
    half = r // 2
    inv_freq = ROPE_THETA ** (-jnp.arange(half, dtype=f32) / half)
    ang = positions.astype(f32)[..., None] * inv_freq
    cos, sin = jnp.cos(ang), jnp.sin(ang)
    lane = np.arange(LANES) % head_dim
    src = np.where(lane < half, lane, np.clip(lane - half, 0, half - 1))
    cos_l, sin_l = cos[..., src], sin[..., src]
    c = jnp.where(lane < r, cos_l, 1.0)
    sa = jnp.where(lane < half, -sin_l, 0.0)
    sb = jnp.where((lane >= half) & (lane < r), sin_l, 0.0)
    return c, sa, sb


def _proj(x, positions, w_in, gate_up, gate_bias, tm):
    B, S, D = x.shape
    splits = np.cumsum([W_DSA, W_DSA, W_DSA, W_IDX, IDX_DIM, IDX_HEADS, W_GQK, W_GQK, W_GV, GLA_GATE_RANK])
    (wq, wk, wv, wqi, wki, wwi, wgq, wgk, wgv, wlr, wgr) = jnp.split(w_in, splits.tolist(), axis=1)
    pad = jnp.zeros((D, LANES - IDX_DIM - IDX_HEADS - GLA_GATE_RANK), w_in.dtype)
    w_a = jnp.concatenate([wq, wk, wv, wqi, wki, wwi, wlr, pad, wgq, wgk, wgv, wgr], axis=1).astype(bf16)
    gup = jnp.zeros((LANES, W_GQK), f32).at[MISC_LR:MISC_LR + GLA_GATE_RANK].set(gate_up).astype(bf16)
    tabs = _rot_tables(positions, DSA_HEAD_DIM) + _rot_tables(positions, IDX_DIM)
    W = w_a.shape[1]
    tok = lambda n: pl.BlockSpec((None, tm, n), lambda b, j: (b, j, 0))
    full = lambda a: pl.BlockSpec(a.shape, lambda b, j: (0,) * a.ndim)
    tr = lambda n: pl.BlockSpec((None, n, tm), lambda b, j: (b, 0, j))
    sd = jax.ShapeDtypeStruct
    out_shape = [sd((B, S, W_DSA), bf16), sd((B, W_DSA, S), bf16), sd((B, S, W_DSA), bf16), sd((B, S, W_IDX), bf16),
                 sd((B, IDX_DIM, S), bf16), sd((B, S, LANES), f32), sd((B, S, W_GQK), f32), sd((B, S, W_GQK), f32),
                 sd((B, S, W_GV), bf16), sd((B, S, W_GQK), f32), sd((B, S, W_GV), f32)]
    out_specs = [tok(W_DSA), tr(W_DSA), tok(W_DSA), tok(W_IDX), tr(IDX_DIM), tok(LANES), tok(W_GQK), tok(W_GQK),
                 tok(W_GV), tok(W_GQK), tok(W_GV)]
    gb = gate_bias.reshape(1, W_GQK)
    return pl.pallas_call(
        _proj_kernel, grid=(B, S // tm), out_shape=out_shape, out_specs=out_specs,
        in_specs=[tok(D), full(w_a), full(gup), full(gb)] + [tok(LANES)] * 6,
        compiler_params=_params("parallel", "parallel"), name="proj",
    )(x, w_a, gup, gb, *tabs)


def _dsa_kernel(q_ref, kt_ref, v_ref, qi_ref, kit_ref, misc_ref, o_ref, *, n_sel, idx_bits):
    qb, S = q_ref.shape[0], kt_ref.shape[1]
    i = pl.program_id(1)
    qi = qi_ref[...]
    kit = kit_ref[...]
    wi = misc_ref[:, MISC_WI:MISC_WI + IDX_HEADS] * (IDX_HEADS ** -0.5)
    score = jnp.zeros((qb, S), f32)
    for h in range(IDX_HEADS):
        d = _dot(qi[:, h * IDX_DIM:(h + 1) * IDX_DIM], kit) * (IDX_DIM ** -0.5)
        score = score + wi[:, h:h + 1] * jnp.maximum(d, 0.0)
    score = jnp.where(score == 0.0, 0.0, score)
    bits = pltpu.bitcast(score, i32)
    key = bits ^ ((bits >> 31) & jnp.int32(0x7FFFFFFF))
    col = lax.broadcasted_iota(i32, (qb, S), 1)
    qpos = lax.broadcasted_iota(i32, (qb, 1), 0) + i * qb
    key = jnp.where(col <= qpos, key, INT_MIN)
    target = jnp.minimum(n_sel, qpos + 1).astype(f32)

    def count(mask):
        return jnp.sum(jnp.where(mask, 1.0, 0.0), axis=1, keepdims=True)

    base = jnp.where(count(key >= 0) >= target, jnp.int32(0), jnp.int32(INT_MIN))

    def tau_bit(t, base):
        cand = base | jnp.left_shift(jnp.int32(1), 30 - t)
        return jnp.where(count(key >= cand) >= target, cand, base)

    tau = lax.fori_loop(0, 31, tau_bit, base)
    gt = key > tau
    tie = key == tau
    need = target - count(gt)

    def idx_bit(t, m):
        cand = m | jnp.left_shift(jnp.int32(1), idx_bits - 1 - t)
        return jnp.where(count(tie & (col < cand)) < need, cand, m)

    m = lax.fori_loop(0, idx_bits, idx_bit, jnp.zeros((qb, 1), i32))
    sel = gt | (tie & (col <= m))

    q = q_ref[...]
    group = 256 // DSA_HEAD_DIM
    for h in range(DSA_HEADS):
        lg = _dot(q[:, h * DSA_HEAD_DIM:(h + 1) * DSA_HEAD_DIM], kt_ref[h * DSA_HEAD_DIM:(h + 1) * DSA_HEAD_DIM, :])
        lg = jnp.where(sel, lg, NEG_INF)
        p = jnp.exp(lg - jnp.max(lg, axis=1, keepdims=True))
        l = jnp.sum(p, axis=1, keepdims=True)
        g = h // group
        r = _dot(p.astype(bf16), v_ref[:, g * 256:(g + 1) * 256])
        off = (h % group) * DSA_HEAD_DIM
        o_ref[:, h * DSA_HEAD_DIM:(h + 1) * DSA_HEAD_DIM] = (r[:, off:off + DSA_HEAD_DIM] / l).astype(o_ref.dtype)


def _dsa(q, kt, v, qi, kit, misc, qb):
    B, S, _ = q.shape
    n_sel = min(TOPK_MAX, S // 4)
    blk = lambda n: pl.BlockSpec((None, qb, n), lambda b, i: (b, i, 0))
    per_b = lambda r, c: pl.BlockSpec((None, r, c), lambda b, i: (b, 0, 0))
    kern = functools.partial(_dsa_kernel, n_sel=n_sel, idx_bits=max(1, (S - 1).bit_length()))
    return pl.pallas_call(
        kern, grid=(B, S // qb), out_shape=jax.ShapeDtypeStruct((B, S, W_DSA), bf16), out_specs=blk(W_DSA),
        in_specs=[blk(W_DSA), per_b(W_DSA, S), per_b(S, W_DSA), blk(W_IDX), per_b(IDX_DIM, S), blk(LANES)],
        compiler_params=_params("parallel", "arbitrary"), name="dsa",
    )(q, kt, v, qi, kit, misc)


def _gla_kernel(gq_ref, gk_ref, gv_ref, la_ref, gr_ref, ng_ref, o_ref, state_ref):
    ct = gq_ref.shape[0]
    nch = ct // GLA_CHUNK

    @pl.when(pl.program_id(1) == 0)
    def _():
        state_ref[...] = jnp.zeros_like(state_ref)

    la = la_ref[...]
    r = lax.broadcasted_iota(i32, (ct, ct), 0)
    c = lax.broadcasted_iota(i32, (ct, ct), 1)
    same = (r // GLA_CHUNK) == (c // GLA_CHUNK)
    causal = same & (c <= r)
    hp = lax.Precision.HIGHEST
    bcum = _dot(jnp.where(causal, 1.0, 0.0), la, precision=hp)
    blast = _dot(jnp.where(same, 1.0, 0.0), la, precision=hp)
    q_dec = (gq_ref[...] * (GLA_DK ** -0.5) * jnp.exp(bcum)).astype(bf16)
    k_inv = (gk_ref[...] * jnp.exp(-bcum)).astype(bf16)
    k_end = (gk_ref[...] * jnp.exp(blast - bcum)).astype(bf16)
    decay = jnp.exp(blast)
    ng = ng_ref[...]
    for h in range(GLA_HEADS):
        ks = slice(h * GLA_DK, (h + 1) * GLA_DK)
        vs = slice(h * GLA_DV, (h + 1) * GLA_DV)
        qd, ki, ke, vh = q_dec[:, ks], k_inv[:, ks], k_end[:, ks], gv_ref[:, vs]
        attn = jnp.where(causal, _dot(qd, ki, _NT), 0.0)
        o = _dot(attn.astype(bf16), vh)
        st = state_ref[h]
        inter = []
        for n in range(nch):
            rows = slice(n * GLA_CHUNK, (n + 1) * GLA_CHUNK)
            inter.append(_dot(qd[rows], st.astype(bf16), _NT))
            st = st * decay[n * GLA_CHUNK:n * GLA_CHUNK + 1, ks] + _dot(vh[rows], ke[rows], _TN)
        state_ref[h] = st
        o = o + jnp.concatenate(inter, axis=0)
        o = o * lax.rsqrt(jnp.mean(o * o, axis=-1, keepdims=True) + RMS_EPS) * ng
        g = gr_ref[:, vs]
        o_ref[:, vs] = (o * (g * jax.nn.sigmoid(g))).astype(o_ref.dtype)


def _gla(gq, gk, gv, la, gr, norm_g, ct):
    B, S, _ = gq.shape
    blk = lambda n: pl.BlockSpec((None, ct, n), lambda b, j: (b, j, 0))
    ng = norm_g.reshape(1, GLA_DV)
    return pl.pallas_call(
        _gla_kernel, grid=(B, S // ct), out_shape=jax.ShapeDtypeStruct((B, S, W_GV), bf16), out_specs=blk(W_GV),
        in_specs=[blk(W_GQK), blk(W_GQK), blk(W_GV), blk(W_GQK), blk(W_GV), pl.BlockSpec(ng.shape, lambda b, j: (0, 0))],
        scratch_shapes=[pltpu.VMEM((GLA_HEADS, GLA_DV, GLA_DK), f32)],
        compiler_params=_params("parallel", "arbitrary"), name="gla",
    )(gq, gk, gv, la, gr, ng)


def _mix_out_kernel(x_ref, ya_ref, yb_ref, wa_ref, wb_ref, g_ref, b_ref, o_ref, *, alpha):
    mix = _dot(ya_ref[...], wa_ref[...]) + _dot(yb_ref[...], wb_ref[...])
    o_ref[...] = _layer_norm(alpha * x_ref[...] + mix, g_ref[...], b_ref[...])


def _mix_out(x2, ya, yb, w_out, g, b, alpha, tm):
    T, D = x2.shape
    wa, wb = w_out[:W_DSA].astype(bf16), w_out[W_DSA:].astype(bf16)
    tok = lambda n: pl.BlockSpec((tm, n), lambda i: (i, 0))
    full = lambda a: pl.BlockSpec(a.shape, lambda i: (0, 0))
    g, b = g.reshape(1, D), b.reshape(1, D)
    return pl.pallas_call(
        functools.partial(_mix_out_kernel, alpha=alpha), grid=(T // tm,),
        out_shape=jax.ShapeDtypeStruct((T, D), f32), out_specs=tok(D),
        in_specs=[tok(D), tok(W_DSA), tok(W_GV), full(wa), full(wb), full(g), full(b)],
        compiler_params=_params("parallel"), name="mix_out",
    )(x2, ya, yb, wa, wb, g, b)


def _mem_kv_kernel(m_ref, wk_ref, wv_ref, k_ref, v_ref):
    m = m_ref[...].astype(bf16)
    k_ref[...] = _dot(m, wk_ref[...]).astype(bf16)
    v_ref[...] = _dot(m, wv_ref[...]).astype(bf16)


def _mem_kv(mem, w_k, w_v):
    B, M, D = mem.shape
    wk, wv = w_k.astype(bf16), w_v.astype(bf16)
    blk = pl.BlockSpec((None, M, D), lambda b: (b, 0, 0))
    full = pl.BlockSpec((D, D), lambda b: (0, 0))
    sd = jax.ShapeDtypeStruct((B, M, D), bf16)
    return pl.pallas_call(_mem_kv_kernel, grid=(B,), out_shape=[sd, sd], out_specs=[blk, blk],
                          in_specs=[blk, full, full], compiler_params=_params("parallel"), name="mem_kv")(mem, wk, wv)


def _xattn_kernel(h_ref, k_ref, v_ref, wq_ref, wo_ref, g_ref, b_ref, o_ref, *, alpha):
    h = h_ref[...]
    D = h.shape[1]
    hd = D // XATTN_HEADS
    q = (_dot(h.astype(bf16), wq_ref[...]) * (hd ** -0.5)).astype(bf16)
    outs = []
    for a in range(XATTN_HEADS):
        s = slice(a * hd, (a + 1) * hd)
        lg = _dot(q[:, s], k_ref[:, s], _NT)
        p = jnp.exp(lg - jnp.max(lg, axis=1, keepdims=True))
        l = jnp.sum(p, axis=1, keepdims=True)
        outs.append((_dot(p.astype(bf16), v_ref[:, s]) / l).astype(bf16))
    ca = _dot(jnp.concatenate(outs, axis=1), wo_ref[...])
    o_ref[...] = _layer_norm(alpha * h + ca, g_ref[...], b_ref[...])


def _xattn(h1, km, vm, w_q, w_o, g, b, alpha, tm):
    B, S, D = h1.shape
    M = km.shape[1]
    wq, wo = w_q.astype(bf16), w_o.astype(bf16)
    g, b = g.reshape(1, D), b.reshape(1, D)
    tok = pl.BlockSpec((None, tm, D), lambda bi, j: (bi, j, 0))
    per_b = pl.BlockSpec((None, M, D), lambda bi, j: (bi, 0, 0))
    full = lambda a: pl.BlockSpec(a.shape, lambda bi, j: (0, 0))
    return pl.pallas_call(
        functools.partial(_xattn_kernel, alpha=alpha), grid=(B, S // tm),
        out_shape=jax.ShapeDtypeStruct((B, S, D), f32), out_specs=tok,
        in_specs=[tok, per_b, per_b, full(wq), full(wo), full(g), full(b)],
        compiler_params=_params("parallel", "parallel"), name="xattn",
    )(h1, km, vm, wq, wo, g, b)


def _top_rows(s, n_top, payload=None):
    n = s.shape[0]
    rows = lax.broadcasted_iota(i32, s.shape, 0)
    vals, picks = [], []
    for _ in range(n_top):
        m = jnp.max(s, axis=0, keepdims=True)
        am = jnp.min(jnp.where(s == m, rows, n), axis=0, keepdims=True)
        hit = rows == am
        vals.append(m)
        picks.append(am if payload is None else jnp.max(jnp.where(hit, payload, -1), axis=0, keepdims=True))
        s = jnp.where(hit, NEG_INF, s)
    return jnp.concatenate(vals, axis=0), jnp.concatenate(picks, axis=0)


def _route_kernel(h_ref, wq_ref, k1_ref, k2_ref, e_ref, g_ref):
    q = _dot(h_ref[...].astype(bf16), wq_ref[...])
    half = PEER_D_KEY // 2
    k1, k2 = k1_ref[...], k2_ref[...]
    for a in range(PEER_HEADS):
        qa = q[:, a * PEER_D_KEY:a * PEER_D_KEY + half].astype(bf16)
        qb = q[:, a * PEER_D_KEY + half:(a + 1) * PEER_D_KEY].astype(bf16)
        v1, i1 = _top_rows(_dot(k1, qa, _NT), PEER_TOPK)
        v2, i2 = _top_rows(_dot(k2, qb, _NT), PEER_TOPK)
        cand = jnp.concatenate([v1[r:r + 1] + v2 for r in range(PEER_TOPK)], axis=0)
        cidx = jnp.concatenate([i1[r:r + 1] * PEER_N_KEYS + i2 for r in range(PEER_TOPK)], axis=0)
        top, experts = _top_rows(cand, PEER_TOPK, payload=cidx)
        p = jnp.exp(top - top[0:1])
        rows = slice(a * PEER_TOPK, (a + 1) * PEER_TOPK)
        e_ref[rows, :] = experts
        g_ref[rows, :] = p / jnp.sum(p, axis=0, keepdims=True)


def _route(h2, w_query, k1, k2, tm):
    T, D = h2.shape
    wq = w_query.astype(bf16)
    k1, k2 = k1.astype(bf16), k2.astype(bf16)
    hk = PEER_HEADS * PEER_TOPK
    full = lambda a: pl.BlockSpec(a.shape, lambda i: (0, 0))
    out = pl.BlockSpec((hk, tm), lambda i: (0, i))
    return pl.pallas_call(
        _route_kernel, grid=(T // tm,),
        out_shape=[jax.ShapeDtypeStruct((hk, T), i32), jax.ShapeDtypeStruct((hk, T), f32)], out_specs=[out, out],
        in_specs=[pl.BlockSpec((tm, D), lambda i: (i, 0)), full(wq), full(k1), full(k2)],
        compiler_params=_params("parallel"), name="route",
    )(h2, wq, k1, k2)


def _pack_table(tab):
    n, d = tab.shape
    u = lax.bitcast_convert_type(tab.astype(bf16), jnp.uint16).astype(jnp.uint32).reshape(n // 2, 2, d // LANES, LANES)
    return lax.bitcast_convert_type((u[:, 0] << 16) | u[:, 1], i32)


def _row(tab_ref, e):
    w = tab_ref[e >> 1]
    return pltpu.bitcast(jnp.left_shift(w, (e & 1) * 16) & jnp.int32(-65536), f32)


def _peer_down_kernel(e_ref, x_ref, g_ref, tab_ref, c_ref, part_ref, act_ref):
    tb, hk = c_ref.shape

    def token(t, carry):
        xt = x_ref[t]

        def expert(k, c2):
            p = _row(tab_ref, e_ref[t, k]) * xt
            part_ref[pl.ds(k, 1), :] = jnp.sum(p, axis=0, keepdims=True)
            return c2

        lax.fori_loop(0, hk, expert, 0, unroll=8)
        act_ref[pl.ds(t, 1), :] = jnp.sum(part_ref[...].T, axis=0, keepdims=True)
        return carry

    lax.fori_loop(0, tb, token, 0)
    a = act_ref[...]
    gelu = 0.5 * a * (1.0 + lax.erf(a * (2.0 ** -0.5)))
    c_ref[...] = g_ref[...] * gelu


def _peer_down(experts, x3, gates, tab, tb):
    T, hk = experts.shape
    smem = pl.BlockSpec((tb, hk), lambda i: (i, 0), memory_space=pltpu.SMEM)
    return pl.pallas_call(
        _peer_down_kernel, grid=(T // tb,), out_shape=jax.ShapeDtypeStruct((T, hk), f32),
        out_specs=pl.BlockSpec((tb, hk), lambda i: (i, 0)),
        in_specs=[smem, pl.BlockSpec((tb,) + x3.shape[1:], lambda i: (i, 0, 0)),
                  pl.BlockSpec((tb, hk), lambda i: (i, 0)), pl.BlockSpec(memory_space=pltpu.VMEM)],
        scratch_shapes=[pltpu.VMEM((hk, LANES), f32), pltpu.VMEM((tb, hk), f32)],
        compiler_params=_params("arbitrary"), name="peer_down",
    )(experts, x3, gates, tab)


def _peer_up_kernel(e_ref, c_ref, tab_ref, o_ref):
    tb, hk = e_ref.shape
    n_acc = 4

    def token(t, carry):
        def expert(j, accs):
            return tuple(acc + _row(tab_ref, e_ref[t, j * n_acc + a]) * c_ref[t, j * n_acc + a]
                         for a, acc in enumerate(accs))

        accs = lax.fori_loop(0, hk // n_acc, expert, (jnp.zeros(o_ref.shape[1:], f32),) * n_acc, unroll=2)
        o_ref[t] = (accs[0] + accs[1]) + (accs[2] + accs[3])
        return carry

    lax.fori_loop(0, tb, token, 0)


def _peer_up(experts, coef, tab, tb):
    T, hk = experts.shape
    smem = pl.BlockSpec((tb, hk), lambda i: (i, 0), memory_space=pltpu.SMEM)
    rows = tab.shape[1:]
    return pl.pallas_call(
        _peer_up_kernel, grid=(T // tb,), out_shape=jax.ShapeDtypeStruct((T,) + rows, f32),
        out_specs=pl.BlockSpec((tb,) + rows, lambda i: (i, 0, 0)),
        in_specs=[smem, smem, pl.BlockSpec(memory_space=pltpu.VMEM)],
        compiler_params=_params("arbitrary"), name="peer_up",
    )(experts, coef, tab)


def _ffn_out_kernel(h_ref, f_ref, g_ref, b_ref, o_ref, *, alpha):
    o_ref[...] = _layer_norm(alpha * h_ref[...] + f_ref[...], g_ref[...], b_ref[...])


def _ffn_out(h2, ff, g, b, alpha, tm):
    T, D = h2.shape
    g, b = g.reshape(1, D), b.reshape(1, D)
    tok = pl.BlockSpec((tm, D), lambda i: (i, 0))
    full = pl.BlockSpec((1, D), lambda i: (0, 0))
    return pl.pallas_call(
        functools.partial(_ffn_out_kernel, alpha=alpha), grid=(T // tm,),
        out_shape=jax.ShapeDtypeStruct((T, D), f32), out_specs=tok, in_specs=[tok, tok, full, full],
        compiler_params=_params("parallel"), name="ffn_out",
    )(h2, ff, g, b)


def _tile(n, want):
    t = min(n, want)
    assert n % t == 0, (n, t)
    return t


def kernel(x, positions, mem, w_in, gla_gate_up, gla_gate_bias, gla_norm_g, w_out, ln_mix_g, ln_mix_b, xattn_w_q, xattn_w_k, xattn_w_v, xattn_w_o, ln_mem_g, ln_mem_b, peer_w_query, peer_sub_keys_1, peer_sub_keys_2, peer_expert_down, peer_expert_up, ln_ffn_g, ln_ffn_b):
    B, S, D = x.shape
    T = B * S
    depth = w_in.shape[0]
    alpha = (2.0 * depth) ** 0.25
    tm = _tile(S, 512)
    h = x
    for l in range(depth):
        q, kt, v, qi, kit, misc, gq, gk, gv, la, gr = _proj(h, positions, w_in[l], gla_gate_up[l], gla_gate_bias[l], tm)
        y_dsa = _dsa(q, kt, v, qi, kit, misc, _tile(S, 128))
        y_gla = _gla(gq, gk, gv, la, gr, gla_norm_g[l], tm)
        h1 = _mix_out(h.reshape(T, D), y_dsa.reshape(T, W_DSA), y_gla.reshape(T, W_GV), w_out[l],
                      ln_mix_g[l], ln_mix_b[l], alpha, tm)
        km, vm = _mem_kv(mem, xattn_w_k[l], xattn_w_v[l])
        h2 = _xattn(h1.reshape(B, S, D), km, vm, xattn_w_q[l], xattn_w_o[l], ln_mem_g[l], ln_mem_b[l], alpha, tm)
        h2 = h2.reshape(T, D)
        experts_t, gates_t = _route(h2, peer_w_query[l], peer_sub_keys_1[l], peer_sub_keys_2[l], _tile(T, 256))
        experts, gates = experts_t.T, gates_t.T
        tb = _tile(T, 128)
        coef = _peer_down(experts, h2.reshape(T, D // LANES, LANES), gates, _pack_table(peer_expert_down[l]), tb)
        ff = _peer_up(experts, coef, _pack_table(peer_expert_up[l]), tb)
        h = _ffn_out(h2, ff.reshape(T, D), ln_ffn_g[l], ln_ffn_b[l], alpha, tm).reshape(B, S, D)
    return h
```

```python
import functools

import jax
import jax.numpy as jnp
import numpy as np
from jax import lax
from jax.experimental import pallas as pl
from jax.experimental.pallas import tpu as pltpu

f32 = jnp.float32
bf16 = jnp.bfloat16
i32 = jnp.int32

DSA_HEADS = 8
DSA_HEAD_DIM = 64
IDX_HEADS = 8
IDX_DIM = 32
TOPK_MAX = 256
GLA_HEADS = 4
GLA_DK = 64
GLA_DV = 128
GLA_GATE_RANK = 16
GLA_GATE_TEMP = 16.0
GLA_CHUNK = 64
ROPE_THETA = 500000.0
ROPE_FRACTION = 4
XATTN_HEADS = 4
PEER_N_KEYS = 128
PEER_HEADS = 8
PEER_D_KEY = 256
PEER_TOPK = 16
LN_EPS = 1e-5
RMS_EPS = 1e-6

LANES = 128
SUBLANES = 8
VMEM_LIMIT = 56 * 1024 * 1024

INT_MIN = -(2 ** 31)
NEG_INF = float("-inf")

W_DSA = DSA_HEADS * DSA_HEAD_DIM
W_IDX = IDX_HEADS * IDX_DIM
W_GQK = GLA_HEADS * GLA_DK
W_GV = GLA_HEADS * GLA_DV
MISC_KI = 0
MISC_WI = IDX_DIM
MISC_LR = IDX_DIM + IDX_HEADS


def _dot(a, b, dims=(((1,), (0,)), ((), ())), precision=None):
    return lax.dot_general(a, b, dims, precision=precision, preferred_element_type=f32)


_NN = (((1,), (0,)), ((), ()))
_NT = (((1,), (1,)), ((), ()))
_TN = (((0,), (0,)), ((), ()))


def _params(*sem):
    return pltpu.CompilerParams(dimension_semantics=sem, vmem_limit_bytes=VMEM_LIMIT)


def _layer_norm(y, g, b):
    mu = jnp.mean(y, axis=-1, keepdims=True)
    yc = y - mu
    var = jnp.mean(yc * yc, axis=-1, keepdims=True)
    return yc * lax.rsqrt(var + LN_EPS) * g + b


def _rot(xb, c, sa, sb, half):
    return xb * c + pltpu.roll(xb, LANES - half, 1) * sa + pltpu.roll(xb, half, 1) * sb


def _proj_kernel(x_ref, w_ref, gup_ref, gb_ref, cq_ref, saq_ref, sbq_ref, ci_ref, sai_ref, sbi_ref,
                 q_ref, kt_ref, v_ref, qi_ref, kit_ref, misc_ref, gq_ref, gk_ref, gv_ref, la_ref, gr_ref):
    x = x_ref[...].astype(bf16)
    tm = x.shape[0]
    cq, saq, sbq = cq_ref[...], saq_ref[...], sbq_ref[...]
    ci, sai, sbi = ci_ref[...], sai_ref[...], sbi_ref[...]
    hq = DSA_HEAD_DIM // ROPE_FRACTION // 2
    hi = IDX_DIM // ROPE_FRACTION // 2
    o = 0
    scale = DSA_HEAD_DIM ** -0.5
    for j in range(W_DSA // LANES):
        a = _dot(x, w_ref[:, o + j * LANES:o + (j + 1) * LANES])
        q_ref[:, j * LANES:(j + 1) * LANES] = (_rot(a, cq, saq, sbq, hq) * scale).astype(bf16)
    o += W_DSA
    for j in range(W_DSA // LANES):
        a = _dot(x, w_ref[:, o + j * LANES:o + (j + 1) * LANES])
        kt_ref[j * LANES:(j + 1) * LANES, :] = _rot(a, cq, saq, sbq, hq).T.astype(bf16)
    o += W_DSA
    v_ref[...] = _dot(x, w_ref[:, o:o + W_DSA]).astype(bf16)
    o += W_DSA
    for j in range(W_IDX // LANES):
        a = _dot(x, w_ref[:, o + j * LANES:o + (j + 1) * LANES])
        qi_ref[:, j * LANES:(j + 1) * LANES] = _rot(a, ci, sai, sbi, hi).astype(bf16)
    o += W_IDX
    m = _dot(x, w_ref[:, o:o + LANES])
    lane = lax.broadcasted_iota(i32, (tm, LANES), 1)
    is_ki = lane < IDX_DIM
    m = _rot(m, jnp.where(is_ki, ci, 1.0), jnp.where(is_ki, sai, 0.0), jnp.where(is_ki, sbi, 0.0), hi)
    misc_ref[...] = m
    kit_ref[...] = m.T[:IDX_DIM, :].astype(bf16)
    z = _dot(m.astype(bf16), gup_ref[...]) + gb_ref[...]
    la_ref[...] = (jnp.minimum(z, 0.0) - jnp.log1p(jnp.exp(-jnp.abs(z)))) / GLA_GATE_TEMP
    o += LANES
    gq_ref[...] = _dot(x, w_ref[:, o:o + W_GQK])
    o += W_GQK
    gk_ref[...] = _dot(x, w_ref[:, o:o + W_GQK])
    o += W_GQK
    gv_ref[...] = _dot(x, w_ref[:, o:o + W_GV]).astype(bf16)
    o += W_GV
    gr_ref[...] = _dot(x, w_ref[:, o:o + W_GV])


def _rot_tables(positions, head_dim):
    r = head_dim // ROPE_FRACTION
    half = r // 2
    inv_freq = ROPE_THETA ** (-jnp.arange(half, dtype=f32) / half)
    ang = positions.astype(f32)[..., None] * inv_freq
    cos, sin = jnp.cos(ang), jnp.sin(ang)
    lane = np.arange(LANES) % head_dim
    src = np.where(lane < half, lane, np.clip(lane - half, 0, half - 1))
    cos_l, sin_l = cos[..., src], sin[..., src]
    c = jnp.where(lane < r, cos_l, 1.0)
    sa = jnp.where(lane < half, -sin_l, 0.0)
    sb = jnp.where((lane >= half) & (lane < r), sin_l, 0.0)
    return c, sa, sb


def _proj(x, positions, w_in, gate_up, gate_bias, tm):
    B, S, D = x.shape
    splits = np.cumsum([W_DSA, W_DSA, W_DSA, W_IDX, IDX_DIM, IDX_HEADS, W_GQK, W_GQK, W_GV, GLA_GATE_RANK])
    (wq, wk, wv, wqi, wki, wwi, wgq, wgk, wgv, wlr, wgr) = jnp.split(w_in, splits.tolist(), axis=1)
    pad = jnp.zeros((D, LANES - IDX_DIM - IDX_HEADS - GLA_GATE_RANK), w_in.dtype)
    w_a = jnp.concatenate([wq, wk, wv, wqi, wki, wwi, wlr, pad, wgq, wgk, wgv, wgr], axis=1).astype(bf16)
    gup = jnp.zeros((LANES, W_GQK), f32).at[MISC_LR:MISC_LR + GLA_GATE_RANK].set(gate_up).astype(bf16)
    tabs = _rot_tables(positions, DSA_HEAD_DIM) + _rot_tables(positions, IDX_DIM)
    W = w_a.shape[1]
    tok = lambda n: pl.BlockSpec((None, tm, n), lambda b, j: (b, j, 0))
    full = lambda a: pl.BlockSpec(a.shape, lambda b, j: (0,) * a.ndim)
    tr = lambda n: pl.BlockSpec((None, n, tm), lambda b, j: (b, 0, j))
    sd = jax.ShapeDtypeStruct
    out_shape = [sd((B, S, W_DSA), bf16), sd((B, W_DSA, S), bf16), sd((B, S, W_DSA), bf16), sd((B, S, W_IDX), bf16),
                 sd((B, IDX_DIM, S), bf16), sd((B, S, LANES), f32), sd((B, S, W_GQK), f32), sd((B, S, W_GQK), f32),
                 sd((B, S, W_GV), bf16), sd((B, S, W_GQK), f32), sd((B, S, W_GV), f32)]
    out_specs = [tok(W_DSA), tr(W_DSA), tok(W_DSA), tok(W_IDX), tr(IDX_DIM), tok(LANES), tok(W_GQK), tok(W_GQK),
                 tok(W_GV), tok(W_GQK), tok(W_GV)]
    gb = gate_bias.reshape(1, W_GQK)
    return pl.pallas_call(
        _proj_kernel, grid=(B, S // tm), out_shape=out_shape, out_specs=out_specs,
        in_specs=[tok(D), full(w_a), full(gup), full(gb)] + [tok(LANES)] * 6,
        compiler_params=_params("parallel", "parallel"), name="proj",
    )(x, w_a, gup, gb, *tabs)


def _dsa_kernel(q_ref, kt_ref, v_ref, qi_ref, kit_ref, misc_ref, o_ref, *, n_sel, idx_bits):
    qb, S = q_ref.shape[0], kt_ref.shape[1]
    i = pl.program_id(1)
    qi = qi_ref[...]
    kit = kit_ref[...]
    wi = misc_ref[:, MISC_WI:MISC_WI + IDX_HEADS] * (IDX_HEADS ** -0.5)
    score = jnp.zeros((qb, S), f32)
    for h in range(IDX_HEADS):
        d = _dot(qi[:, h * IDX_DIM:(h + 1) * IDX_DIM], kit) * (IDX_DIM ** -0.5)
        score = score + wi[:, h:h + 1] * jnp.maximum(d, 0.0)
    score = jnp.where(score == 0.0, 0.0, score)
    bits = pltpu.bitcast(score, i32)
    key = bits ^ ((bits >> 31) & jnp.int32(0x7FFFFFFF))
    col = lax.broadcasted_iota(i32, (qb, S), 1)
    qpos = lax.broadcasted_iota(i32, (qb, 1), 0) + i * qb
    key = jnp.where(col <= qpos, key, INT_MIN)
    target = jnp.minimum(n_sel, qpos + 1).astype(f32)

    def count(mask):
        return jnp.sum(jnp.where(mask, 1.0, 0.0), axis=1, keepdims=True)

    base = jnp.where(count(key >= 0) >= target, jnp.int32(0), jnp.int32(INT_MIN))

    def tau_bit(t, base):
        cand = base | jnp.left_shift(jnp.int32(1), 30 - t)
        return jnp.where(count(key >= cand) >= target, cand, base)

    tau = lax.fori_loop(0, 31, tau_bit, base)
    gt = key > tau
    tie = key == tau
    need = target - count(gt)

    def idx_bit(t, m):
        cand = m | jnp.left_shift(jnp.int32(1), idx_bits - 1 - t)
        return jnp.where(count(tie & (col < cand)) < need, cand, m)

    m = lax.fori_loop(0, idx_bits, idx_bit, jnp.zeros((qb, 1), i32))
    sel = gt | (tie & (col <= m))

    q = q_ref[...]
    group = 256 // DSA_HEAD_DIM
    for h in range(DSA_HEADS):
        lg = _dot(q[:, h * DSA_HEAD_DIM:(h + 1) * DSA_HEAD_DIM], kt_ref[h * DSA_HEAD_DIM:(h + 1) * DSA_HEAD_DIM, :])
        lg = jnp.where(sel, lg, NEG_INF)
        p = jnp.exp(lg - jnp.max(lg, axis=1, keepdims=True))
        l = jnp.sum(p, axis=1, keepdims=True)
        g = h // group
        r = _dot(p.astype(bf16), v_ref[:, g * 256:(g + 1) * 256])
        off = (h % group) * DSA_HEAD_DIM
        o_ref[:, h * DSA_HEAD_DIM:(h + 1) * DSA_HEAD_DIM] = (r[:, off:off + DSA_HEAD_DIM] / l).astype(o_ref.dtype)


def _dsa(q, kt, v, qi, kit, misc, qb):
    B, S, _ = q.shape
    n_sel = min(TOPK_MAX, S // 4)
    blk = lambda n: pl.BlockSpec((None, qb, n), lambda b, i: (b, i, 0))
    per_b = lambda r, c: pl.BlockSpec((None, r, c), lambda b, i: (b, 0, 0))
    kern = functools.partial(_dsa_kernel, n_sel=n_sel, idx_bits=max(1, (S - 1).bit_length()))
    return pl.pallas_call(
        kern, grid=(B, S // qb), out_shape=jax.ShapeDtypeStruct((B, S, W_DSA), bf16), out_specs=blk(W_DSA),
        in_specs=[blk(W_DSA), per_b(W_DSA, S), per_b(S, W_DSA), blk(W_IDX), per_b(IDX_DIM, S), blk(LANES)],
        compiler_params=_params("parallel", "arbitrary"), name="dsa",
    )(q, kt, v, qi, kit, misc)


def _gla_kernel(gq_ref, gk_ref, gv_ref, la_ref, gr_ref, ng_ref, o_ref, state_ref):
    ct = gq_ref.shape[0]
    nch = ct // GLA_CHUNK

    @pl.when(pl.program_id(1) == 0)
    def _():
        state_ref[...] = jnp.zeros_like(state_ref)

    la = la_ref[...]
    r = lax.broadcasted_iota(i32, (ct, ct), 0)
    c = lax.broadcasted_iota(i32, (ct, ct), 1)
    same = (r // GLA_CHUNK) == (c // GLA_CHUNK)
    causal = same & (c <= r)
    hp = lax.Precision.HIGHEST
    bcum = _dot(jnp.where(causal, 1.0, 0.0), la, precision=hp)
    blast = _dot(jnp.where(same, 1.0, 0.0), la, precision=hp)
    q_dec = (gq_ref[...] * (GLA_DK ** -0.5) * jnp.exp(bcum)).astype(bf16)
    k_inv = (gk_ref[...] * jnp.exp(-bcum)).astype(bf16)
    k_end = (gk_ref[...] * jnp.exp(blast - bcum)).astype(bf16)
    decay = jnp.exp(blast)
    ng = ng_ref[...]
    for h in range(GLA_HEADS):
        ks = slice(h * GLA_DK, (h + 1) * GLA_DK)
        vs = slice(h * GLA_DV, (h + 1) * GLA_DV)
        qd, ki, ke, vh = q_dec[:, ks], k_inv[:, ks], k_end[:, ks], gv_ref[:, vs]
        attn = jnp.where(causal, _dot(qd, ki, _NT), 0.0)
        o = _dot(attn.astype(bf16), vh)
        st = state_ref[h]
        inter = []
        for n in range(nch):
            rows = slice(n * GLA_CHUNK, (n + 1) * GLA_CHUNK)
            inter.append(_dot(qd[rows], st.astype(bf16), _NT))
            st = st * decay[n * GLA_CHUNK:n * GLA_CHUNK + 1, ks] + _dot(vh[rows], ke[rows], _TN)
        state_ref[h] = st
        o = o + jnp.concatenate(inter, axis=0)
        o = o * lax.rsqrt(jnp.mean(o * o, axis=-1, keepdims=True) + RMS_EPS) * ng
        g = gr_ref[:, vs]
        o_ref[:, vs] = (o * (g * jax.nn.sigmoid(g))).astype(o_ref.dtype)


def _gla(gq, gk, gv, la, gr, norm_g, ct):
    B, S, _ = gq.shape
    blk = lambda n: pl.BlockSpec((None, ct, n), lambda b, j: (b, j, 0))
    ng = norm_g.reshape(1, GLA_DV)
    return pl.pallas_call(
        _gla_kernel, grid=(B, S // ct), out_shape=jax.ShapeDtypeStruct((B, S, W_GV), bf16), out_specs=blk(W_GV),
        in_specs=[blk(W_GQK), blk(W_GQK), blk(W_GV), blk(W_GQK), blk(W_GV), pl.BlockSpec(ng.shape, lambda b, j: (0, 0))],
        scratch_shapes=[pltpu.VMEM((GLA_HEADS, GLA_DV, GLA_DK), f32)],
        compiler_params=_params("parallel", "arbitrary"), name="gla",
    )(gq, gk, gv, la, gr, ng)


def _mix_out_kernel(x_ref, ya_ref, yb_ref, wa_ref, wb_ref, g_ref, b_ref, o_ref, *, alpha):
    mix = _dot(ya_ref[...], wa_ref[...]) + _dot(yb_ref[...], wb_ref[...])
    o_ref[...] = _layer_norm(alpha * x_ref[...] + mix, g_ref[...], b_ref[...])


def _mix_out(x2, ya, yb, w_out, g, b, alpha, tm):
    T, D = x2.shape
    wa, wb = w_out[:W_DSA].astype(bf16), w_out[W_DSA:].astype(bf16)
    tok = lambda n: pl.BlockSpec((tm, n), lambda i: (i, 0))
    full = lambda a: pl.BlockSpec(a.shape, lambda i: (0, 0))
    g, b = g.reshape(1, D), b.reshape(1, D)
    return pl.pallas_call(
        functools.partial(_mix_out_kernel, alpha=alpha), grid=(T // tm,),
        out_shape=jax.ShapeDtypeStruct((T, D), f32), out_specs=tok(D),
        in_specs=[tok(D), tok(W_DSA), tok(W_GV), full(wa), full(wb), full(g), full(b)],
        compiler_params=_params("parallel"), name="mix_out",
    )(x2, ya, yb, wa, wb, g, b)


def _mem_kv_kernel(m_ref, wk_ref, wv_ref, k_ref, v_ref):
    m = m_ref[...].astype(bf16)
    k_ref[...] = _dot(m, wk_ref[...]).astype(bf16)
    v_ref[...] = _dot(m, wv_ref[...]).astype(bf16)


def _mem_kv(mem, w_k, w_v):
    B, M, D = mem.shape
    wk, wv = w_k.astype(bf16), w_v.astype(bf16)
    blk = pl.BlockSpec((None, M, D), lambda b: (b, 0, 0))
    full = pl.BlockSpec((D, D), lambda b: (0, 0))
    sd = jax.ShapeDtypeStruct((B, M, D), bf16)
    return pl.pallas_call(_mem_kv_kernel, grid=(B,), out_shape=[sd, sd], out_specs=[blk, blk],
                          in_specs=[blk, full, full], compiler_params=_params("parallel"), name="mem_kv")(mem, wk, wv)


def _xattn_kernel(h_ref, k_ref, v_ref, wq_ref, wo_ref, g_ref, b_ref, o_ref, *, alpha):
    h = h_ref[...]
    D = h.shape[1]
    hd = D // XATTN_HEADS
    q = (_dot(h.astype(bf16), wq_ref[...]) * (hd ** -0.5)).astype(bf16)
    outs = []
    for a in range(XATTN_HEADS):
        s = slice(a * hd, (a + 1) * hd)
        lg = _dot(q[:, s], k_ref[:, s], _NT)
        p = jnp.exp(lg - jnp.max(lg, axis=1, keepdims=True))
        l = jnp.sum(p, axis=1, keepdims=True)
        outs.append((_dot(p.astype(bf16), v_ref[:, s]) / l).astype(bf16))
    ca = _dot(jnp.concatenate(outs, axis=1), wo_ref[...])
    o_ref[...] = _layer_norm(alpha * h + ca, g_ref[...], b_ref[...])


def _xattn(h1, km, vm, w_q, w_o, g, b, alpha, tm):
    B, S, D = h1.shape
    M = km.shape[1]
    wq, wo = w_q.astype(bf16), w_o.astype(bf16)
    g, b = g.reshape(1, D), b.reshape(1, D)
    tok = pl.BlockSpec((None, tm, D), lambda bi, j: (bi, j, 0))
    per_b = pl.BlockSpec((None, M, D), lambda bi, j: (bi, 0, 0))
    full = lambda a: pl.BlockSpec(a.shape, lambda bi, j: (0, 0))
    return pl.pallas_call(
        functools.partial(_xattn_kernel, alpha=alpha), grid=(B, S // tm),
        out_shape=jax.ShapeDtypeStruct((B, S, D), f32), out_specs=tok,
        in_specs=[tok, per_b, per_b, full(wq), full(wo), full(g), full(b)],
        compiler_params=_params("parallel", "parallel"), name="xattn",
    )(h1, km, vm, wq, wo, g, b)


def _top_rows(s, n_top, payload=None):
    n = s.shape[0]
    rows = lax.broadcasted_iota(i32, s.shape, 0)
    vals, picks = [], []
    for _ in range(n_top):
        m = jnp.max(s, axis=0, keepdims=True)
        am = jnp.min(jnp.where(s == m, rows, n), axis=0, keepdims=True)
        hit = rows == am
        vals.append(m)
        picks.append(am if payload is None else jnp.max(jnp.where(hit, payload, -1), axis=0, keepdims=True))
        s = jnp.where(hit, NEG_INF, s)
    return jnp.concatenate(vals, axis=0), jnp.concatenate(picks, axis=0)


def _route_kernel(h_ref, wq_ref, k1_ref, k2_ref, row_ref, shift_ref, code_ref, g_ref):
    q = _dot(h_ref[...].astype(bf16), wq_ref[...])
    half = PEER_D_KEY // 2
    k1, k2 = k1_ref[...], k2_ref[...]
    for a in range(PEER_HEADS):
        qa = q[:, a * PEER_D_KEY:a * PEER_D_KEY + half].astype(bf16)
        qb = q[:, a * PEER_D_KEY + half:(a + 1) * PEER_D_KEY].astype(bf16)
        v1, i1 = _top_rows(_dot(k1, qa, _NT), PEER_TOPK)
        v2, i2 = _top_rows(_dot(k2, qb, _NT), PEER_TOPK)
        cand = jnp.concatenate([v1[r:r + 1] + v2 for r in range(PEER_TOPK)], axis=0)
        cidx = jnp.concatenate([i1[r:r + 1] * PEER_N_KEYS + i2 for r in range(PEER_TOPK)], axis=0)
        top, experts = _top_rows(cand, PEER_TOPK, payload=cidx)
        p = jnp.exp(top - top[0:1])
        rows = slice(a * PEER_TOPK, (a + 1) * PEER_TOPK)
        row0 = (experts >> 1) * SUBLANES
        row_ref[rows, :] = row0
        shift_ref[rows, :] = (experts & 1) * 16
        code_ref[rows, :] = row0 + (experts & 1)
        g_ref[rows, :] = p / jnp.sum(p, axis=0, keepdims=True)


def _route(h2, w_query, k1, k2, tm):
    T, D = h2.shape
    wq = w_query.astype(bf16)
    k1, k2 = k1.astype(bf16), k2.astype(bf16)
    hk = PEER_HEADS * PEER_TOPK
    full = lambda a: pl.BlockSpec(a.shape, lambda i: (0, 0))
    out = pl.BlockSpec((hk, tm), lambda i: (0, i))
    sd = jax.ShapeDtypeStruct
    return pl.pallas_call(
        _route_kernel, grid=(T // tm,),
        out_shape=[sd((hk, T), i32)] * 3 + [sd((hk, T), f32)], out_specs=[out] * 4,
        in_specs=[pl.BlockSpec((tm, D), lambda i: (i, 0)), full(wq), full(k1), full(k2)],
        compiler_params=_params("parallel"), name="route",
    )(h2, wq, k1, k2)


def _pack_table(tab):
    n, d = tab.shape
    assert d == SUBLANES * LANES
    u = lax.bitcast_convert_type(tab.astype(bf16), jnp.uint16).astype(jnp.uint32).reshape(n // 2, 2, SUBLANES, LANES)
    return lax.bitcast_convert_type((u[:, 0] << 16) | u[:, 1], i32).reshape(n // 2 * SUBLANES, LANES)


def _row(tab_ref, row0, shift):
    w = tab_ref[pl.ds(pl.multiple_of(row0, SUBLANES), SUBLANES), :]
    return pltpu.bitcast(jnp.left_shift(w, shift) & jnp.int32(-65536), f32)


TOKEN_BATCH = 8
GROUP_UNROLL = 4


_TREE_ORDER = (0, 4, 2, 6, 1, 5, 3, 7)


def _merge(p, q, mask, shift):
    return jnp.where(mask, p, q) + pltpu.roll(jnp.where(mask, q, p), shift, 0)


def _sublane_sums(ps):
    sub = lax.broadcasted_iota(i32, ps[0].shape, 0)
    quad = lambda a, b, o: _merge(a, b, ((sub - o) & 7) < 4, 4)
    duo = lambda a, b, o: _merge(a, b, ((sub - o) & 3) < 2, 6)
    r1 = duo(quad(ps[0], ps[1], 0), quad(ps[2], ps[3], 2), 0)
    r2 = duo(quad(ps[4], ps[5], 1), quad(ps[6], ps[7], 3), 1)
    return _merge(r1, r2, (sub & 1) == 0, 7)


def _peer_down_kernel(row_ref, shift_ref, x_ref, gt_ref, tab_ref, ct_ref, part_ref, actt_ref):
    hk, tb = ct_ref.shape
    lane = lax.broadcasted_iota(i32, (hk, tb), 1)

    def batch(b, carry):
        t0 = b * TOKEN_BATCH

        def token(u, c1):
            t = t0 + u
            xt = x_ref[t]

            def group(g, c2):
                k0 = pl.multiple_of(g * SUBLANES, SUBLANES)
                js = [t * hk + k0 + k for k in _TREE_ORDER]
                ps = [_row(tab_ref, row_ref[j], shift_ref[j]) * xt for j in js]
                part_ref[u, pl.ds(k0, SUBLANES), :] = _sublane_sums(ps)
                return c2

            lax.fori_loop(0, hk // SUBLANES, group, 0, unroll=GROUP_UNROLL)
            return c1

        lax.fori_loop(0, TOKEN_BATCH, token, 0)
        a = actt_ref[...]
        for u in range(TOKEN_BATCH):
            a = jnp.where(lane == t0 + u, jnp.sum(part_ref[u], axis=1, keepdims=True), a)
        actt_ref[...] = a
        return carry

    lax.fori_loop(0, tb // TOKEN_BATCH, batch, 0)
    a = actt_ref[...]
    gelu = 0.5 * a * (1.0 + lax.erf(a * (2.0 ** -0.5)))
    ct_ref[...] = gt_ref[...] * gelu


def _peer_down(rows, shifts, x3, gates_t, tab, tb):
    hk, T = gates_t.shape
    smem = pl.BlockSpec((tb * hk,), lambda i: (i,), memory_space=pltpu.SMEM)
    per_k = pl.BlockSpec((hk, tb), lambda i: (0, i))
    return pl.pallas_call(
        _peer_down_kernel, grid=(T // tb,), out_shape=jax.ShapeDtypeStruct((hk, T), f32), out_specs=per_k,
        in_specs=[smem, smem, pl.BlockSpec((tb,) + x3.shape[1:], lambda i: (i, 0, 0)), per_k,
                  pl.BlockSpec(memory_space=pltpu.VMEM)],
        scratch_shapes=[pltpu.VMEM((TOKEN_BATCH, hk, LANES), f32), pltpu.VMEM((hk, tb), f32)],
        compiler_params=_params("arbitrary"), name="peer_down",
    )(rows, shifts, x3, gates_t, tab)


def _peer_up_kernel(code_ref, c_ref, tab_ref, o_ref, *, hk):
    tb = o_ref.shape[0]
    n_acc = 4
    vreg = o_ref.shape[1:]

    def token(t, carry):
        def group(g, accs):
            accs = list(accs)
            for p in range(SUBLANES):
                j = t * hk + g * SUBLANES + p
                code = code_ref[j]
                shift = (jnp.full(vreg, code, i32) & 1) * 16
                accs[p % n_acc] = accs[p % n_acc] + _row(tab_ref, code & -SUBLANES, shift) * c_ref[j]
            return tuple(accs)

        accs = lax.fori_loop(0, hk // SUBLANES, group, (jnp.zeros(vreg, f32),) * n_acc, unroll=GROUP_UNROLL)
        o_ref[t] = (accs[0] + accs[1]) + (accs[2] + accs[3])
        return carry

    lax.fori_loop(0, tb, token, 0)


def _peer_up(codes, coef, tab, hk, tb):
    T = codes.shape[0] // hk
    smem = pl.BlockSpec((tb * hk,), lambda i: (i,), memory_space=pltpu.SMEM)
    return pl.pallas_call(
        functools.partial(_peer_up_kernel, hk=hk), grid=(T // tb,),
        out_shape=jax.ShapeDtypeStruct((T, SUBLANES, LANES), f32),
        out_specs=pl.BlockSpec((tb, SUBLANES, LANES), lambda i: (i, 0, 0)),
        in_specs=[smem, smem, pl.BlockSpec(memory_space=pltpu.VMEM)],
        compiler_params=_params("arbitrary"), name="peer_up",
    )(codes, coef, tab)


def _ffn_out_kernel(h_ref, f_ref, g_ref, b_ref, o_ref, *, alpha):
    o_ref[...] = _layer_norm(alpha * h_ref[...] + f_ref[...], g_ref[...], b_ref[...])


def _ffn_out(h2, ff, g, b, alpha, tm):
    T, D = h2.shape
    g, b = g.reshape(1, D), b.reshape(1, D)
    tok = pl.BlockSpec((tm, D), lambda i: (i, 0))
    full = pl.BlockSpec((1, D), lambda i: (0, 0))
    return pl.pallas_call(
        functools.partial(_ffn_out_kernel, alpha=alpha), grid=(T // tm,),
        out_shape=jax.ShapeDtypeStruct((T, D), f32), out_specs=tok, in_specs=[tok, tok, full, full],
        compiler_params=_params("parallel"), name="ffn_out",
    )(h2, ff, g, b)


def _tile(n, want):
    t = min(n, want)
    assert n % t == 0, (n, t)
    return t


def kernel(x, positions, mem, w_in, gla_gate_up, gla_gate_bias, gla_norm_g, w_out, ln_mix_g, ln_mix_b, xattn_w_q, xattn_w_k, xattn_w_v, xattn_w_o, ln_mem_g, ln_mem_b, peer_w_query, peer_sub_keys_1, peer_sub_keys_2, peer_expert_down, peer_expert_up, ln_ffn_g, ln_ffn_b):
    B, S, D = x.shape
    T = B * S
    depth = w_in.shape[0]
    alpha = (2.0 * depth) ** 0.25
    tm = _tile(S, 512)
    h = x
    for l in range(depth):
        q, kt, v, qi, kit, misc, gq, gk, gv, la, gr = _proj(h, positions, w_in[l], gla_gate_up[l], gla_gate_bias[l], tm)
        y_dsa = _dsa(q, kt, v, qi, kit, misc, _tile(S, 128))
        y_gla = _gla(gq, gk, gv, la, gr, gla_norm_g[l], tm)
        h1 = _mix_out(h.reshape(T, D), y_dsa.reshape(T, W_DSA), y_gla.reshape(T, W_GV), w_out[l],
                      ln_mix_g[l], ln_mix_b[l], alpha, tm)
        km, vm = _mem_kv(mem, xattn_w_k[l], xattn_w_v[l])
        h2 = _xattn(h1.reshape(B, S, D), km, vm, xattn_w_q[l], xattn_w_o[l], ln_mem_g[l], ln_mem_b[l], alpha, tm)
        h2 = h2.reshape(T, D)
        rows_t, shift_t, code_t, gates_t = _route(h2, peer_w_query[l], peer_sub_keys_1[l], peer_sub_keys_2[l],
                                                  _tile(T, 256))
        flat = lambda a: a.T.reshape(-1)
        tb = _tile(T, LANES)
        coef_t = _peer_down(flat(rows_t), flat(shift_t), h2.reshape(T, SUBLANES, LANES), gates_t,
                            _pack_table(peer_expert_down[l]), tb)
        ff = _peer_up(flat(code_t), flat(coef_t), _pack_table(peer_expert_up[l]), coef_t.shape[0], tb)
        h = _ffn_out(h2, ff.reshape(T, D), ln_ffn_g[l], ln_ffn_b[l], alpha, tm).reshape(B, S, D)
    return h
```

```python
import functools

import jax
import jax.numpy as jnp
import numpy as np
from jax import lax
from jax.experimental import pallas as pl
from jax.experimental.pallas import tpu as pltpu

f32 = jnp.float32
bf16 = jnp.bfloat16
i32 = jnp.int32

DSA_HEADS = 8
DSA_HEAD_DIM = 64
IDX_HEADS = 8
IDX_DIM = 32
TOPK_MAX = 256
GLA_HEADS = 4
GLA_DK = 64
GLA_DV = 128
GLA_GATE_RANK = 16
GLA_GATE_TEMP = 16.0
GLA_CHUNK = 64
ROPE_THETA = 500000.0
ROPE_FRACTION = 4
XATTN_HEADS = 4
PEER_N_KEYS = 128
PEER_HEADS = 8
PEER_D_KEY = 256
PEER_TOPK = 16
LN_EPS = 1e-5
RMS_EPS = 1e-6

LANES = 128
SUBLANES = 8
VMEM_LIMIT = 56 * 1024 * 1024

INT_MIN = -(2 ** 31)
NEG_INF = float("-inf")

W_DSA = DSA_HEADS * DSA_HEAD_DIM
W_IDX = IDX_HEADS * IDX_DIM
W_GQK = GLA_HEADS * GLA_DK
W_GV = GLA_HEADS * GLA_DV
MISC_KI = 0
MISC_WI = IDX_DIM
MISC_LR = IDX_DIM + IDX_HEADS


def _dot(a, b, dims=(((1,), (0,)), ((), ())), precision=None):
    return lax.dot_general(a, b, dims, precision=precision, preferred_element_type=f32)


_NN = (((1,), (0,)), ((), ()))
_NT = (((1,), (1,)), ((), ()))
_TN = (((0,), (0,)), ((), ()))


def _params(*sem):
    return pltpu.CompilerParams(dimension_semantics=sem, vmem_limit_bytes=VMEM_LIMIT)


def _layer_norm(y, g, b):
    mu = jnp.mean(y, axis=-1, keepdims=True)
    yc = y - mu
    var = jnp.mean(yc * yc, axis=-1, keepdims=True)
    return yc * lax.rsqrt(var + LN_EPS) * g + b


def _rot(xb, c, sa, sb, half):
    return xb * c + pltpu.roll(xb, LANES - half, 1) * sa + pltpu.roll(xb, half, 1) * sb


def _proj_kernel(x_ref, w_ref, gup_ref, gb_ref, cq_ref, saq_ref, sbq_ref, ci_ref, sai_ref, sbi_ref,
                 q_ref, kt_ref, v_ref, qi_ref, kit_ref, misc_ref, gq_ref, gk_ref, gv_ref, la_ref, gr_ref):
    x = x_ref[...].astype(bf16)
    tm = x.shape[0]
    cq, saq, sbq = cq_ref[...], saq_ref[...], sbq_ref[...]
    ci, sai, sbi = ci_ref[...], sai_ref[...], sbi_ref[...]
    hq = DSA_HEAD_DIM // ROPE_FRACTION // 2
    hi = IDX_DIM // ROPE_FRACTION // 2
    o = 0
    scale = DSA_HEAD_DIM ** -0.5
    for j in range(W_DSA // LANES):
        a = _dot(x, w_ref[:, o + j * LANES:o + (j + 1) * LANES])
        q_ref[:, j * LANES:(j + 1) * LANES] = (_rot(a, cq, saq, sbq, hq) * scale).astype(bf16)
    o += W_DSA
    for j in range(W_DSA // LANES):
        a = _dot(x, w_ref[:, o + j * LANES:o + (j + 1) * LANES])
        kt_ref[j * LANES:(j + 1) * LANES, :] = _rot(a, cq, saq, sbq, hq).T.astype(bf16)
    o += W_DSA
    v_ref[...] = _dot(x, w_ref[:, o:o + W_DSA]).astype(bf16)
    o += W_DSA
    for j in range(W_IDX // LANES):
        a = _dot(x, w_ref[:, o + j * LANES:o + (j + 1) * LANES])
        qi_ref[:, j * LANES:(j + 1) * LANES] = _rot(a, ci, sai, sbi, hi).astype(bf16)
    o += W_IDX
    m = _dot(x, w_ref[:, o:o + LANES])
    lane = lax.broadcasted_iota(i32, (tm, LANES), 1)
    is_ki = lane < IDX_DIM
    m = _rot(m, jnp.where(is_ki, ci, 1.0), jnp.where(is_ki, sai, 0.0), jnp.where(is_ki, sbi, 0.0), hi)
    misc_ref[...] = m
    kit_ref[...] = m.T[:IDX_DIM, :].astype(bf16)
    z = _dot(m.astype(bf16), gup_ref[...]) + gb_ref[...]
    la_ref[...] = (jnp.minimum(z, 0.0) - jnp.log1p(jnp.exp(-jnp.abs(z)))) / GLA_GATE_TEMP
    o += LANES
    gq_ref[...] = _dot(x, w_ref[:, o:o + W_GQK])
    o += W_GQK
    gk_ref[...] = _dot(x, w_ref[:, o:o + W_GQK])
    o += W_GQK
    gv_ref[...] = _dot(x, w_ref[:, o:o + W_GV]).astype(bf16)
    o += W_GV
    gr_ref[...] = _dot(x, w_ref[:, o:o + W_GV])


def _rot_tables(positions, head_dim):
    r = head_dim // ROPE_FRACTION
    half = r // 2
    inv_freq = ROPE_THETA ** (-jnp.arange(half, dtype=f32) / half)
    ang = positions.astype(f32)[..., None] * inv_freq
    cos, sin = jnp.cos(ang), jnp.sin(ang)
    lane = np.arange(LANES) % head_dim
    src = np.where(lane < half, lane, np.clip(lane - half, 0, half - 1))
    cos_l, sin_l = cos[..., src], sin[..., src]
    c = jnp.where(lane < r, cos_l, 1.0)
    sa = jnp.where(lane < half, -sin_l, 0.0)
    sb = jnp.where((lane >= half) & (lane < r), sin_l, 0.0)
    return c, sa, sb


def _proj(x, positions, w_in, gate_up, gate_bias, tm):
    B, S, D = x.shape
    splits = np.cumsum([W_DSA, W_DSA, W_DSA, W_IDX, IDX_DIM, IDX_HEADS, W_GQK, W_GQK, W_GV, GLA_GATE_RANK])
    (wq, wk, wv, wqi, wki, wwi, wgq, wgk, wgv, wlr, wgr) = jnp.split(w_in, splits.tolist(), axis=1)
    pad = jnp.zeros((D, LANES - IDX_DIM - IDX_HEADS - GLA_GATE_RANK), w_in.dtype)
    w_a = jnp.concatenate([wq, wk, wv, wqi, wki, wwi, wlr, pad, wgq, wgk, wgv, wgr], axis=1).astype(bf16)
    gup = jnp.zeros((LANES, W_GQK), f32).at[MISC_LR:MISC_LR + GLA_GATE_RANK].set(gate_up).astype(bf16)
    tabs = _rot_tables(positions, DSA_HEAD_DIM) + _rot_tables(positions, IDX_DIM)
    W = w_a.shape[1]
    tok = lambda n: pl.BlockSpec((None, tm, n), lambda b, j: (b, j, 0))
    full = lambda a: pl.BlockSpec(a.shape, lambda b, j: (0,) * a.ndim)
    tr = lambda n: pl.BlockSpec((None, n, tm), lambda b, j: (b, 0, j))
    sd = jax.ShapeDtypeStruct
    out_shape = [sd((B, S, W_DSA), bf16), sd((B, W_DSA, S), bf16), sd((B, S, W_DSA), bf16), sd((B, S, W_IDX), bf16),
                 sd((B, IDX_DIM, S), bf16), sd((B, S, LANES), f32), sd((B, S, W_GQK), f32), sd((B, S, W_GQK), f32),
                 sd((B, S, W_GV), bf16), sd((B, S, W_GQK), f32), sd((B, S, W_GV), f32)]
    out_specs = [tok(W_DSA), tr(W_DSA), tok(W_DSA), tok(W_IDX), tr(IDX_DIM), tok(LANES), tok(W_GQK), tok(W_GQK),
                 tok(W_GV), tok(W_GQK), tok(W_GV)]
    gb = gate_bias.reshape(1, W_GQK)
    return pl.pallas_call(
        _proj_kernel, grid=(B, S // tm), out_shape=out_shape, out_specs=out_specs,
        in_specs=[tok(D), full(w_a), full(gup), full(gb)] + [tok(LANES)] * 6,
        compiler_params=_params("parallel", "parallel"), name="proj",
    )(x, w_a, gup, gb, *tabs)


def _dsa_kernel(q_ref, kt_ref, v_ref, qi_ref, kit_ref, misc_ref, o_ref, key_ref, bias_ref, lg_ref, mx_ref, ls_ref,
                acc_ref, *, n_sel, idx_bits, kc):
    qb, S = q_ref.shape[0], kt_ref.shape[1]
    i = pl.program_id(1)
    nk = lax.div((i + 1) * qb + (kc - 1), kc)
    qi = qi_ref[...]
    wi = misc_ref[:, MISC_WI:MISC_WI + IDX_HEADS] * (IDX_HEADS ** -0.5)
    qpos = lax.broadcasted_iota(i32, (qb, 1), 0) + i * qb
    target = jnp.minimum(n_sel, qpos + 1).astype(f32)
    lane = lax.broadcasted_iota(i32, (qb, kc), 1)

    def chunk(c):
        return pl.ds(pl.multiple_of(c * kc, kc), kc)

    def score_chunk(c, carry):
        kit = kit_ref[:, chunk(c)]
        score = jnp.zeros((qb, kc), f32)
        for h in range(IDX_HEADS):
            d = _dot(qi[:, h * IDX_DIM:(h + 1) * IDX_DIM], kit) * (IDX_DIM ** -0.5)
            score = score + wi[:, h:h + 1] * jnp.maximum(d, 0.0)
        score = jnp.where(score == 0.0, 0.0, score)
        bits = pltpu.bitcast(score, i32)
        key = bits ^ ((bits >> 31) & jnp.int32(0x7FFFFFFF))
        key_ref[:, chunk(c)] = jnp.where(lane + c * kc <= qpos, key, INT_MIN)
        return carry

    lax.fori_loop(0, nk, score_chunk, 0)

    def count(pred):
        def body(c, acc):
            hit = jnp.where(pred(key_ref[:, chunk(c)], lane + c * kc), 1.0, 0.0)
            for j in range(kc // LANES):
                acc = acc + hit[:, j * LANES:(j + 1) * LANES]
            return acc

        acc = lax.fori_loop(0, nk, body, jnp.zeros((qb, LANES), f32))
        return jnp.sum(acc, axis=1, keepdims=True)

    base = jnp.where(count(lambda k, col: k >= 0) >= target, jnp.int32(0), jnp.int32(INT_MIN))

    def tau_bit(t, base):
        cand = base | jnp.left_shift(jnp.int32(1), 30 - t)
        return jnp.where(count(lambda k, col: k >= cand) >= target, cand, base)

    tau = lax.fori_loop(0, 31, tau_bit, base)
    excess = jnp.max(count(lambda k, col: k >= tau) - target)

    def tie_limit():
        need = target - count(lambda k, col: k > tau)

        def idx_bit(t, m):
            cand = m | jnp.left_shift(jnp.int32(1), idx_bits - 1 - t)
            return jnp.where(count(lambda k, col: (k == tau) & (col < cand)) < need, cand, m)

        return lax.fori_loop(0, idx_bits, idx_bit, jnp.zeros((qb, 1), i32))

    m_idx = lax.cond(excess > 0.0, tie_limit, lambda: jnp.full((qb, 1), S, i32))

    def bias_chunk(c, carry):
        k = key_ref[:, chunk(c)]
        sel = (k > tau) | ((k == tau) & (lane + c * kc <= m_idx))
        bias_ref[:, chunk(c)] = jnp.where(sel, 0.0, NEG_INF)
        return carry

    lax.fori_loop(0, nk, bias_chunk, 0)

    q = q_ref[...]
    pair = LANES // DSA_HEAD_DIM

    def fold(x, op, acc):
        for j in range(kc // LANES):
            acc = op(acc, x[:, j * LANES:(j + 1) * LANES])
        return acc

    heads = range(DSA_HEADS)
    head_rows = [slice(h * DSA_HEAD_DIM, (h + 1) * DSA_HEAD_DIM) for h in heads]
    mx_ref[...] = jnp.full(mx_ref.shape, NEG_INF, f32)
    ls_ref[...] = jnp.zeros_like(ls_ref)
    acc_ref[...] = jnp.zeros_like(acc_ref)

    def logits(c, carry):
        bias = bias_ref[:, chunk(c)]
        for h in heads:
            lg = _dot(q[:, head_rows[h]], kt_ref[head_rows[h], chunk(c)]) + bias
            lg_ref[h, :, chunk(c)] = lg
            mx_ref[h] = fold(lg, jnp.maximum, mx_ref[h])
        return carry

    lax.fori_loop(0, nk, logits, 0)
    ms = [jnp.max(mx_ref[h], axis=1, keepdims=True) for h in heads]

    def weigh(c, carry):
        for h in heads:
            p = jnp.exp(lg_ref[h, :, chunk(c)] - ms[h])
            ls_ref[h] = fold(p, jnp.add, ls_ref[h])
            slab = slice((h // pair) * LANES, (h // pair + 1) * LANES)
            acc_ref[h] = acc_ref[h] + _dot(p.astype(bf16), v_ref[chunk(c), slab])
        return carry

    lax.fori_loop(0, nk, weigh, 0)
    for h in heads:
        l = jnp.sum(ls_ref[h], axis=1, keepdims=True)
        off = (h % pair) * DSA_HEAD_DIM
        o_ref[:, head_rows[h]] = (acc_ref[h][:, off:off + DSA_HEAD_DIM] / l).astype(o_ref.dtype)


def _dsa(q, kt, v, qi, kit, misc, qb, kc):
    B, S, _ = q.shape
    n_sel = min(TOPK_MAX, S // 4)
    blk = lambda n: pl.BlockSpec((None, qb, n), lambda b, i: (b, i, 0))
    per_b = lambda r, c: pl.BlockSpec((None, r, c), lambda b, i: (b, 0, 0))
    kern = functools.partial(_dsa_kernel, n_sel=n_sel, idx_bits=max(1, (S - 1).bit_length()), kc=kc)
    return pl.pallas_call(
        kern, grid=(B, S // qb), out_shape=jax.ShapeDtypeStruct((B, S, W_DSA), bf16), out_specs=blk(W_DSA),
        in_specs=[blk(W_DSA), per_b(W_DSA, S), per_b(S, W_DSA), blk(W_IDX), per_b(IDX_DIM, S), blk(LANES)],
        scratch_shapes=[pltpu.VMEM((qb, S), i32), pltpu.VMEM((qb, S), f32), pltpu.VMEM((DSA_HEADS, qb, S), f32)]
        + [pltpu.VMEM((DSA_HEADS, qb, LANES), f32)] * 3,
        compiler_params=_params("parallel", "arbitrary"), name="dsa",
    )(q, kt, v, qi, kit, misc)


def _gla_kernel(gq_ref, gk_ref, gv_ref, la_ref, gr_ref, ng_ref, o_ref, state_ref):
    ct = gq_ref.shape[0]
    nch = ct // GLA_CHUNK

    @pl.when(pl.program_id(1) == 0)
    def _():
        state_ref[...] = jnp.zeros_like(state_ref)

    la = la_ref[...]
    r = lax.broadcasted_iota(i32, (ct, ct), 0)
    c = lax.broadcasted_iota(i32, (ct, ct), 1)
    same = (r // GLA_CHUNK) == (c // GLA_CHUNK)
    causal = same & (c <= r)
    hp = lax.Precision.HIGHEST
    bcum = _dot(jnp.where(causal, 1.0, 0.0), la, precision=hp)
    blast = _dot(jnp.where(same, 1.0, 0.0), la, precision=hp)
    q_dec = (gq_ref[...] * (GLA_DK ** -0.5) * jnp.exp(bcum)).astype(bf16)
    k_inv = (gk_ref[...] * jnp.exp(-bcum)).astype(bf16)
    k_end = (gk_ref[...] * jnp.exp(blast - bcum)).astype(bf16)
    decay = jnp.exp(blast)
    ng = ng_ref[...]
    for h in range(GLA_HEADS):
        ks = slice(h * GLA_DK, (h + 1) * GLA_DK)
        vs = slice(h * GLA_DV, (h + 1) * GLA_DV)
        qd, ki, ke, vh = q_dec[:, ks], k_inv[:, ks], k_end[:, ks], gv_ref[:, vs]
        attn = jnp.where(causal, _dot(qd, ki, _NT), 0.0)
        o = _dot(attn.astype(bf16), vh)
        st = state_ref[h]
        inter = []
        for n in range(nch):
            rows = slice(n * GLA_CHUNK, (n + 1) * GLA_CHUNK)
            inter.append(_dot(qd[rows], st.astype(bf16), _NT))
            st = st * decay[n * GLA_CHUNK:n * GLA_CHUNK + 1, ks] + _dot(vh[rows], ke[rows], _TN)
        state_ref[h] = st
        o = o + jnp.concatenate(inter, axis=0)
        o = o * lax.rsqrt(jnp.mean(o * o, axis=-1, keepdims=True) + RMS_EPS) * ng
        g = gr_ref[:, vs]
        o_ref[:, vs] = (o * (g * jax.nn.sigmoid(g))).astype(o_ref.dtype)


def _gla(gq, gk, gv, la, gr, norm_g, ct):
    B, S, _ = gq.shape
    blk = lambda n: pl.BlockSpec((None, ct, n), lambda b, j: (b, j, 0))
    ng = norm_g.reshape(1, GLA_DV)
    return pl.pallas_call(
        _gla_kernel, grid=(B, S // ct), out_shape=jax.ShapeDtypeStruct((B, S, W_GV), bf16), out_specs=blk(W_GV),
        in_specs=[blk(W_GQK), blk(W_GQK), blk(W_GV), blk(W_GQK), blk(W_GV), pl.BlockSpec(ng.shape, lambda b, j: (0, 0))],
        scratch_shapes=[pltpu.VMEM((GLA_HEADS, GLA_DV, GLA_DK), f32)],
        compiler_params=_params("parallel", "arbitrary"), name="gla",
    )(gq, gk, gv, la, gr, ng)


def _mix_out_kernel(x_ref, ya_ref, yb_ref, wa_ref, wb_ref, g_ref, b_ref, o_ref, *, alpha):
    mix = _dot(ya_ref[...], wa_ref[...]) + _dot(yb_ref[...], wb_ref[...])
    o_ref[...] = _layer_norm(alpha * x_ref[...] + mix, g_ref[...], b_ref[...])


def _mix_out(x2, ya, yb, w_out, g, b, alpha, tm):
    T, D = x2.shape
    wa, wb = w_out[:W_DSA].astype(bf16), w_out[W_DSA:].astype(bf16)
    tok = lambda n: pl.BlockSpec((tm, n), lambda i: (i, 0))
    full = lambda a: pl.BlockSpec(a.shape, lambda i: (0, 0))
    g, b = g.reshape(1, D), b.reshape(1, D)
    return pl.pallas_call(
        functools.partial(_mix_out_kernel, alpha=alpha), grid=(T // tm,),
        out_shape=jax.ShapeDtypeStruct((T, D), f32), out_specs=tok(D),
        in_specs=[tok(D), tok(W_DSA), tok(W_GV), full(wa), full(wb), full(g), full(b)],
        compiler_params=_params("parallel"), name="mix_out",
    )(x2, ya, yb, wa, wb, g, b)


def _mem_kv_kernel(m_ref, wk_ref, wv_ref, k_ref, v_ref):
    m = m_ref[...].astype(bf16)
    k_ref[...] = _dot(m, wk_ref[...]).astype(bf16)
    v_ref[...] = _dot(m, wv_ref[...]).astype(bf16)


def _mem_kv(mem, w_k, w_v):
    B, M, D = mem.shape
    wk, wv = w_k.astype(bf16), w_v.astype(bf16)
    blk = pl.BlockSpec((None, M, D), lambda b: (b, 0, 0))
    full = pl.BlockSpec((D, D), lambda b: (0, 0))
    sd = jax.ShapeDtypeStruct((B, M, D), bf16)
    return pl.pallas_call(_mem_kv_kernel, grid=(B,), out_shape=[sd, sd], out_specs=[blk, blk],
                          in_specs=[blk, full, full], compiler_params=_params("parallel"), name="mem_kv")(mem, wk, wv)


def _xattn_kernel(h_ref, k_ref, v_ref, wq_ref, wo_ref, g_ref, b_ref, o_ref, *, alpha):
    h = h_ref[...]
    D = h.shape[1]
    hd = D // XATTN_HEADS
    q = (_dot(h.astype(bf16), wq_ref[...]) * (hd ** -0.5)).astype(bf16)
    outs = []
    for a in range(XATTN_HEADS):
        s = slice(a * hd, (a + 1) * hd)
        lg = _dot(q[:, s], k_ref[:, s], _NT)
        p = jnp.exp(lg - jnp.max(lg, axis=1, keepdims=True))
        l = jnp.sum(p, axis=1, keepdims=True)
        outs.append((_dot(p.astype(bf16), v_ref[:, s]) / l).astype(bf16))
    ca = _dot(jnp.concatenate(outs, axis=1), wo_ref[...])
    o_ref[...] = _layer_norm(alpha * h + ca, g_ref[...], b_ref[...])


def _xattn(h1, km, vm, w_q, w_o, g, b, alpha, tm):
    B, S, D = h1.shape
    M = km.shape[1]
    wq, wo = w_q.astype(bf16), w_o.astype(bf16)
    g, b = g.reshape(1, D), b.reshape(1, D)
    tok = pl.BlockSpec((None, tm, D), lambda bi, j: (bi, j, 0))
    per_b = pl.BlockSpec((None, M, D), lambda bi, j: (bi, 0, 0))
    full = lambda a: pl.BlockSpec(a.shape, lambda bi, j: (0, 0))
    return pl.pallas_call(
        functools.partial(_xattn_kernel, alpha=alpha), grid=(B, S // tm),
        out_shape=jax.ShapeDtypeStruct((B, S, D), f32), out_specs=tok,
        in_specs=[tok, per_b, per_b, full(wq), full(wo), full(g), full(b)],
        compiler_params=_params("parallel", "parallel"), name="xattn",
    )(h1, km, vm, wq, wo, g, b)


def _top_rows(s, n_top, payload=None):
    n = s.shape[0]
    rows = lax.broadcasted_iota(i32, s.shape, 0)
    vals, picks = [], []
    for _ in range(n_top):
        m = jnp.max(s, axis=0, keepdims=True)
        am = jnp.min(jnp.where(s == m, rows, n), axis=0, keepdims=True)
        hit = rows == am
        vals.append(m)
        picks.append(am if payload is None else jnp.max(jnp.where(hit, payload, -1), axis=0, keepdims=True))
        s = jnp.where(hit, NEG_INF, s)
    return jnp.concatenate(vals, axis=0), jnp.concatenate(picks, axis=0)


def _route_kernel(h_ref, wq_ref, k1_ref, k2_ref, row_ref, shift_ref, code_ref, g_ref):
    q = _dot(h_ref[...].astype(bf16), wq_ref[...])
    half = PEER_D_KEY // 2
    k1, k2 = k1_ref[...], k2_ref[...]
    for a in range(PEER_HEADS):
        qa = q[:, a * PEER_D_KEY:a * PEER_D_KEY + half].astype(bf16)
        qb = q[:, a * PEER_D_KEY + half:(a + 1) * PEER_D_KEY].astype(bf16)
        v1, i1 = _top_rows(_dot(k1, qa, _NT), PEER_TOPK)
        v2, i2 = _top_rows(_dot(k2, qb, _NT), PEER_TOPK)
        cand = jnp.concatenate([v1[r:r + 1] + v2 for r in range(PEER_TOPK)], axis=0)
        cidx = jnp.concatenate([i1[r:r + 1] * PEER_N_KEYS + i2 for r in range(PEER_TOPK)], axis=0)
        top, experts = _top_rows(cand, PEER_TOPK, payload=cidx)
        p = jnp.exp(top - top[0:1])
        rows = slice(a * PEER_TOPK, (a + 1) * PEER_TOPK)
        row0 = (experts >> 1) * SUBLANES
        row_ref[rows, :] = row0
        shift_ref[rows, :] = (experts & 1) * 16
        code_ref[rows, :] = row0 + (experts & 1)
        g_ref[rows, :] = p / jnp.sum(p, axis=0, keepdims=True)


def _route(h2, w_query, k1, k2, tm):
    T, D = h2.shape
    wq = w_query.astype(bf16)
    k1, k2 = k1.astype(bf16), k2.astype(bf16)
    hk = PEER_HEADS * PEER_TOPK
    full = lambda a: pl.BlockSpec(a.shape, lambda i: (0, 0))
    out = pl.BlockSpec((hk, tm), lambda i: (0, i))
    sd = jax.ShapeDtypeStruct
    return pl.pallas_call(
        _route_kernel, grid=(T // tm,),
        out_shape=[sd((hk, T), i32)] * 3 + [sd((hk, T), f32)], out_specs=[out] * 4,
        in_specs=[pl.BlockSpec((tm, D), lambda i: (i, 0)), full(wq), full(k1), full(k2)],
        compiler_params=_params("parallel"), name="route",
    )(h2, wq, k1, k2)


def _pack_table(tab):
    n, d = tab.shape
    assert d == SUBLANES * LANES
    u = lax.bitcast_convert_type(tab.astype(bf16), jnp.uint16).astype(jnp.uint32).reshape(n // 2, 2, SUBLANES, LANES)
    return lax.bitcast_convert_type((u[:, 0] << 16) | u[:, 1], i32).reshape(n // 2 * SUBLANES, LANES)


def _row(tab_ref, row0, shift):
    w = tab_ref[pl.ds(pl.multiple_of(row0, SUBLANES), SUBLANES), :]
    return pltpu.bitcast(jnp.left_shift(w, shift) & jnp.int32(-65536), f32)


TOKEN_BATCH = 8
GROUP_UNROLL = 4


_TREE_ORDER = (0, 4, 2, 6, 1, 5, 3, 7)


def _merge(p, q, mask, shift):
    return jnp.where(mask, p, q) + pltpu.roll(jnp.where(mask, q, p), shift, 0)


def _sublane_sums(ps):
    sub = lax.broadcasted_iota(i32, ps[0].shape, 0)
    quad = lambda a, b, o: _merge(a, b, ((sub - o) & 7) < 4, 4)
    duo = lambda a, b, o: _merge(a, b, ((sub - o) & 3) < 2, 6)
    r1 = duo(quad(ps[0], ps[1], 0), quad(ps[2], ps[3], 2), 0)
    r2 = duo(quad(ps[4], ps[5], 1), quad(ps[6], ps[7], 3), 1)
    return _merge(r1, r2, (sub & 1) == 0, 7)


def _peer_down_kernel(row_ref, shift_ref, x_ref, gt_ref, tab_ref, ct_ref, part_ref, actt_ref):
    hk, tb = ct_ref.shape
    lane = lax.broadcasted_iota(i32, (hk, tb), 1)

    def batch(b, carry):
        t0 = b * TOKEN_BATCH

        def token(u, c1):
            t = t0 + u
            xt = x_ref[t]

            def group(g, c2):
                k0 = pl.multiple_of(g * SUBLANES, SUBLANES)
                js = [t * hk + k0 + k for k in _TREE_ORDER]
                ps = [_row(tab_ref, row_ref[j], shift_ref[j]) * xt for j in js]
                part_ref[u, pl.ds(k0, SUBLANES), :] = _sublane_sums(ps)
                return c2

            lax.fori_loop(0, hk // SUBLANES, group, 0, unroll=GROUP_UNROLL)
            return c1

        lax.fori_loop(0, TOKEN_BATCH, token, 0)
        a = actt_ref[...]
        for u in range(TOKEN_BATCH):
            a = jnp.where(lane == t0 + u, jnp.sum(part_ref[u], axis=1, keepdims=True), a)
        actt_ref[...] = a
        return carry

    lax.fori_loop(0, tb // TOKEN_BATCH, batch, 0)
    a = actt_ref[...]
    gelu = 0.5 * a * (1.0 + lax.erf(a * (2.0 ** -0.5)))
    ct_ref[...] = gt_ref[...] * gelu


def _peer_down(rows, shifts, x3, gates_t, tab, tb):
    hk, T = gates_t.shape
    smem = pl.BlockSpec((tb * hk,), lambda i: (i,), memory_space=pltpu.SMEM)
    per_k = pl.BlockSpec((hk, tb), lambda i: (0, i))
    return pl.pallas_call(
        _peer_down_kernel, grid=(T // tb,), out_shape=jax.ShapeDtypeStruct((hk, T), f32), out_specs=per_k,
        in_specs=[smem, smem, pl.BlockSpec((tb,) + x3.shape[1:], lambda i: (i, 0, 0)), per_k,
                  pl.BlockSpec(memory_space=pltpu.VMEM)],
        scratch_shapes=[pltpu.VMEM((TOKEN_BATCH, hk, LANES), f32), pltpu.VMEM((hk, tb), f32)],
        compiler_params=_params("arbitrary"), name="peer_down",
    )(rows, shifts, x3, gates_t, tab)


def _peer_up_kernel(code_ref, c_ref, tab_ref, o_ref, *, hk):
    tb = o_ref.shape[0]
    n_acc = 4
    vreg = o_ref.shape[1:]

    def token(t, carry):
        def group(g, accs):
            accs = list(accs)
            for p in range(SUBLANES):
                j = t * hk + g * SUBLANES + p
                code = code_ref[j]
                shift = (jnp.full(vreg, code, i32) & 1) * 16
                accs[p % n_acc] = accs[p % n_acc] + _row(tab_ref, code & -SUBLANES, shift) * c_ref[j]
            return tuple(accs)

        accs = lax.fori_loop(0, hk // SUBLANES, group, (jnp.zeros(vreg, f32),) * n_acc, unroll=GROUP_UNROLL)
        o_ref[t] = (accs[0] + accs[1]) + (accs[2] + accs[3])
        return carry

    lax.fori_loop(0, tb, token, 0)


def _peer_up(codes, coef, tab, hk, tb):
    T = codes.shape[0] // hk
    smem = pl.BlockSpec((tb * hk,), lambda i: (i,), memory_space=pltpu.SMEM)
    return pl.pallas_call(
        functools.partial(_peer_up_kernel, hk=hk), grid=(T // tb,),
        out_shape=jax.ShapeDtypeStruct((T, SUBLANES, LANES), f32),
        out_specs=pl.BlockSpec((tb, SUBLANES, LANES), lambda i: (i, 0, 0)),
        in_specs=[smem, smem, pl.BlockSpec(memory_space=pltpu.VMEM)],
        compiler_params=_params("arbitrary"), name="peer_up",
    )(codes, coef, tab)


def _ffn_out_kernel(h_ref, f_ref, g_ref, b_ref, o_ref, *, alpha):
    o_ref[...] = _layer_norm(alpha * h_ref[...] + f_ref[...], g_ref[...], b_ref[...])


def _ffn_out(h2, ff, g, b, alpha, tm):
    T, D = h2.shape
    g, b = g.reshape(1, D), b.reshape(1, D)
    tok = pl.BlockSpec((tm, D), lambda i: (i, 0))
    full = pl.BlockSpec((1, D), lambda i: (0, 0))
    return pl.pallas_call(
        functools.partial(_ffn_out_kernel, alpha=alpha), grid=(T // tm,),
        out_shape=jax.ShapeDtypeStruct((T, D), f32), out_specs=tok, in_specs=[tok, tok, full, full],
        compiler_params=_params("parallel"), name="ffn_out",
    )(h2, ff, g, b)


def _tile(n, want):
    t = min(n, want)
    assert n % t == 0, (n, t)
    return t


def kernel(x, positions, mem, w_in, gla_gate_up, gla_gate_bias, gla_norm_g, w_out, ln_mix_g, ln_mix_b, xattn_w_q, xattn_w_k, xattn_w_v, xattn_w_o, ln_mem_g, ln_mem_b, peer_w_query, peer_sub_keys_1, peer_sub_keys_2, peer_expert_down, peer_expert_up, ln_ffn_g, ln_ffn_b):
    B, S, D = x.shape
    T = B * S
    depth = w_in.shape[0]
    alpha = (2.0 * depth) ** 0.25
    tm = _tile(S, 512)
    h = x
    for l in range(depth):
        q, kt, v, qi, kit, misc, gq, gk, gv, la, gr = _proj(h, positions, w_in[l], gla_gate_up[l], gla_gate_bias[l], tm)
        y_dsa = _dsa(q, kt, v, qi, kit, misc, _tile(S, 128), _tile(S, 512))
        y_gla = _gla(gq, gk, gv, la, gr, gla_norm_g[l], tm)
        h1 = _mix_out(h.reshape(T, D), y_dsa.reshape(T, W_DSA), y_gla.reshape(T, W_GV), w_out[l],
                      ln_mix_g[l], ln_mix_b[l], alpha, tm)
        km, vm = _mem_kv(mem, xattn_w_k[l], xattn_w_v[l])
        h2 = _xattn(h1.reshape(B, S, D), km, vm, xattn_w_q[l], xattn_w_o[l], ln_mem_g[l], ln_mem_b[l], alpha, tm)
        h2 = h2.reshape(T, D)
        rows_t, shift_t, code_t, gates_t = _route(h2, peer_w_query[l], peer_sub_keys_1[l], peer_sub_keys_2[l],
                                                  _tile(T, 256))
        flat = lambda a: a.T.reshape(-1)
        tb = _tile(T, LANES)
        coef_t = _peer_down(flat(rows_t), flat(shift_t), h2.reshape(T, SUBLANES, LANES), gates_t,
                            _pack_table(peer_expert_down[l]), tb)
        ff = _peer_up(flat(code_t), flat(coef_t), _pack_table(peer_expert_up[l]), coef_t.shape[0], tb)
        h = _ffn_out(h2, ff.reshape(T, D), ln_ffn_g[l], ln_ffn_b[l], alpha, tm).reshape(B, S, D)
    return h
```

```python
import functools

import jax
import jax.numpy as jnp
import numpy as np
from jax import lax
from jax.experimental import pallas as pl
from jax.experimental.pallas import tpu as pltpu

f32 = jnp.float32
bf16 = jnp.bfloat16
i32 = jnp.int32

DSA_HEADS = 8
DSA_HEAD_DIM = 64
IDX_HEADS = 8
IDX_DIM = 32
TOPK_MAX = 256
GLA_HEADS = 4
GLA_DK = 64
GLA_DV = 128
GLA_GATE_RANK = 16
GLA_GATE_TEMP = 16.0
GLA_CHUNK = 64
ROPE_THETA = 500000.0
ROPE_FRACTION = 4
XATTN_HEADS = 4
PEER_N_KEYS = 128
PEER_HEADS = 8
PEER_D_KEY = 256
PEER_TOPK = 16
LN_EPS = 1e-5
RMS_EPS = 1e-6

LANES = 128
SUBLANES = 8
VMEM_LIMIT = 56 * 1024 * 1024

INT_MIN = -(2 ** 31)
NEG_INF = float("-inf")

W_DSA = DSA_HEADS * DSA_HEAD_DIM
W_IDX = IDX_HEADS * IDX_DIM
W_GQK = GLA_HEADS * GLA_DK
W_GV = GLA_HEADS * GLA_DV
MISC_KI = 0
MISC_WI = IDX_DIM
MISC_LR = IDX_DIM + IDX_HEADS


def _dot(a, b, dims=(((1,), (0,)), ((), ())), precision=None):
    return lax.dot_general(a, b, dims, precision=precision, preferred_element_type=f32)


_NN = (((1,), (0,)), ((), ()))
_NT = (((1,), (1,)), ((), ()))
_TN = (((0,), (0,)), ((), ()))


def _params(*sem):
    return pltpu.CompilerParams(dimension_semantics=sem, vmem_limit_bytes=VMEM_LIMIT)


def _layer_norm(y, g, b):
    mu = jnp.mean(y, axis=-1, keepdims=True)
    yc = y - mu
    var = jnp.mean(yc * yc, axis=-1, keepdims=True)
    return yc * lax.rsqrt(var + LN_EPS) * g + b


def _rot(xb, c, sa, sb, half):
    return xb * c + pltpu.roll(xb, LANES - half, 1) * sa + pltpu.roll(xb, half, 1) * sb


def _proj_kernel(x_ref, w_ref, gup_ref, gb_ref, cq_ref, saq_ref, sbq_ref, ci_ref, sai_ref, sbi_ref,
                 q_ref, kt_ref, v_ref, qi_ref, kit_ref, misc_ref, gq_ref, gk_ref, gv_ref, la_ref, gr_ref):
    x = x_ref[...].astype(bf16)
    tm = x.shape[0]
    cq, saq, sbq = cq_ref[...], saq_ref[...], sbq_ref[...]
    ci, sai, sbi = ci_ref[...], sai_ref[...], sbi_ref[...]
    hq = DSA_HEAD_DIM // ROPE_FRACTION // 2
    hi = IDX_DIM // ROPE_FRACTION // 2
    o = 0
    scale = DSA_HEAD_DIM ** -0.5
    for j in range(W_DSA // LANES):
        a = _dot(x, w_ref[:, o + j * LANES:o + (j + 1) * LANES])
        q_ref[:, j * LANES:(j + 1) * LANES] = (_rot(a, cq, saq, sbq, hq) * scale).astype(bf16)
    o += W_DSA
    for j in range(W_DSA // LANES):
        a = _dot(x, w_ref[:, o + j * LANES:o + (j + 1) * LANES])
        kt_ref[j * LANES:(j + 1) * LANES, :] = _rot(a, cq, saq, sbq, hq).T.astype(bf16)
    o += W_DSA
    v_ref[...] = _dot(x, w_ref[:, o:o + W_DSA]).astype(bf16)
    o += W_DSA
    for j in range(W_IDX // LANES):
        a = _dot(x, w_ref[:, o + j * LANES:o + (j + 1) * LANES])
        qi_ref[:, j * LANES:(j + 1) * LANES] = _rot(a, ci, sai, sbi, hi).astype(bf16)
    o += W_IDX
    m = _dot(x, w_ref[:, o:o + LANES])
    lane = lax.broadcasted_iota(i32, (tm, LANES), 1)
    is_ki = lane < IDX_DIM
    m = _rot(m, jnp.where(is_ki, ci, 1.0), jnp.where(is_ki, sai, 0.0), jnp.where(is_ki, sbi, 0.0), hi)
    misc_ref[...] = m
    kit_ref[...] = m.T[:IDX_DIM, :].astype(bf16)
    z = _dot(m.astype(bf16), gup_ref[...]) + gb_ref[...]
    la_ref[...] = (jnp.minimum(z, 0.0) - jnp.log1p(jnp.exp(-jnp.abs(z)))) / GLA_GATE_TEMP
    o += LANES
    gq_ref[...] = _dot(x, w_ref[:, o:o + W_GQK])
    o += W_GQK
    gk_ref[...] = _dot(x, w_ref[:, o:o + W_GQK])
    o += W_GQK
    gv_ref[...] = _dot(x, w_ref[:, o:o + W_GV]).astype(bf16)
    o += W_GV
    gr_ref[...] = _dot(x, w_ref[:, o:o + W_GV])


def _rot_tables(positions, head_dim):
    r = head_dim // ROPE_FRACTION
    half = r // 2
    inv_freq = ROPE_THETA ** (-jnp.arange(half, dtype=f32) / half)
    ang = positions.astype(f32)[..., None] * inv_freq
    cos, sin = jnp.cos(ang), jnp.sin(ang)
    lane = np.arange(LANES) % head_dim
    src = np.where(lane < half, lane, np.clip(lane - half, 0, half - 1))
    cos_l, sin_l = cos[..., src], sin[..., src]
    c = jnp.where(lane < r, cos_l, 1.0)
    sa = jnp.where(lane < half, -sin_l, 0.0)
    sb = jnp.where((lane >= half) & (lane < r), sin_l, 0.0)
    return c, sa, sb


def _proj(x, positions, w_in, gate_up, gate_bias, tm):
    B, S, D = x.shape
    splits = np.cumsum([W_DSA, W_DSA, W_DSA, W_IDX, IDX_DIM, IDX_HEADS, W_GQK, W_GQK, W_GV, GLA_GATE_RANK])
    (wq, wk, wv, wqi, wki, wwi, wgq, wgk, wgv, wlr, wgr) = jnp.split(w_in, splits.tolist(), axis=1)
    pad = jnp.zeros((D, LANES - IDX_DIM - IDX_HEADS - GLA_GATE_RANK), w_in.dtype)
    w_a = jnp.concatenate([wq, wk, wv, wqi, wki, wwi, wlr, pad, wgq, wgk, wgv, wgr], axis=1).astype(bf16)
    gup = jnp.zeros((LANES, W_GQK), f32).at[MISC_LR:MISC_LR + GLA_GATE_RANK].set(gate_up).astype(bf16)
    tabs = _rot_tables(positions, DSA_HEAD_DIM) + _rot_tables(positions, IDX_DIM)
    W = w_a.shape[1]
    tok = lambda n: pl.BlockSpec((None, tm, n), lambda b, j: (b, j, 0))
    full = lambda a: pl.BlockSpec(a.shape, lambda b, j: (0,) * a.ndim)
    tr = lambda n: pl.BlockSpec((None, n, tm), lambda b, j: (b, 0, j))
    sd = jax.ShapeDtypeStruct
    out_shape = [sd((B, S, W_DSA), bf16), sd((B, W_DSA, S), bf16), sd((B, S, W_DSA), bf16), sd((B, S, W_IDX), bf16),
                 sd((B, IDX_DIM, S), bf16), sd((B, S, LANES), f32), sd((B, S, W_GQK), f32), sd((B, S, W_GQK), f32),
                 sd((B, S, W_GV), bf16), sd((B, S, W_GQK), f32), sd((B, S, W_GV), f32)]
    out_specs = [tok(W_DSA), tr(W_DSA), tok(W_DSA), tok(W_IDX), tr(IDX_DIM), tok(LANES), tok(W_GQK), tok(W_GQK),
                 tok(W_GV), tok(W_GQK), tok(W_GV)]
    gb = gate_bias.reshape(1, W_GQK)
    return pl.pallas_call(
        _proj_kernel, grid=(B, S // tm), out_shape=out_shape, out_specs=out_specs,
        in_specs=[tok(D), full(w_a), full(gup), full(gb)] + [tok(LANES)] * 6,
        compiler_params=_params("parallel", "parallel"), name="proj",
    )(x, w_a, gup, gb, *tabs)


def _dsa_kernel(q_ref, kt_ref, v_ref, qi_ref, kit_ref, misc_ref, o_ref, key_ref, bias_ref, lg_ref, mx_ref, ls_ref,
                acc_ref, *, n_sel, idx_bits, kc):
    qb, S = q_ref.shape[0], kt_ref.shape[1]
    i = pl.program_id(1)
    nk = lax.div((i + 1) * qb + (kc - 1), kc)
    qi = qi_ref[...]
    wi = misc_ref[:, MISC_WI:MISC_WI + IDX_HEADS] * (IDX_HEADS ** -0.5)
    qpos = lax.broadcasted_iota(i32, (qb, 1), 0) + i * qb
    target = jnp.minimum(n_sel, qpos + 1).astype(f32)
    lane = lax.broadcasted_iota(i32, (qb, kc), 1)

    def chunk(c):
        return pl.ds(pl.multiple_of(c * kc, kc), kc)

    def score_chunk(c, carry):
        kit = kit_ref[:, chunk(c)]
        score = jnp.zeros((qb, kc), f32)
        for h in range(IDX_HEADS):
            d = _dot(qi[:, h * IDX_DIM:(h + 1) * IDX_DIM], kit) * (IDX_DIM ** -0.5)
            score = score + wi[:, h:h + 1] * jnp.maximum(d, 0.0)
        score = jnp.where(score == 0.0, 0.0, score)
        bits = pltpu.bitcast(score, i32)
        key = bits ^ ((bits >> 31) & jnp.int32(0x7FFFFFFF))
        key_ref[:, chunk(c)] = jnp.where(lane + c * kc <= qpos, key, INT_MIN)
        return carry

    lax.fori_loop(0, nk, score_chunk, 0)

    def count(pred):
        def body(c, acc):
            hit = jnp.where(pred(key_ref[:, chunk(c)], lane + c * kc), 1.0, 0.0)
            for j in range(kc // LANES):
                acc = acc + hit[:, j * LANES:(j + 1) * LANES]
            return acc

        acc = lax.fori_loop(0, nk, body, jnp.zeros((qb, LANES), f32))
        return jnp.sum(acc, axis=1, keepdims=True)

    base = jnp.where(count(lambda k, col: k >= 0) >= target, jnp.int32(0), jnp.int32(INT_MIN))

    def tau_bit(t, base):
        cand = base | jnp.left_shift(jnp.int32(1), 30 - t)
        return jnp.where(count(lambda k, col: k >= cand) >= target, cand, base)

    tau = lax.fori_loop(0, 31, tau_bit, base)
    excess = jnp.max(count(lambda k, col: k >= tau) - target)

    def tie_limit():
        need = target - count(lambda k, col: k > tau)

        def idx_bit(t, m):
            cand = m | jnp.left_shift(jnp.int32(1), idx_bits - 1 - t)
            return jnp.where(count(lambda k, col: (k == tau) & (col < cand)) < need, cand, m)

        return lax.fori_loop(0, idx_bits, idx_bit, jnp.zeros((qb, 1), i32))

    m_idx = lax.cond(excess > 0.0, tie_limit, lambda: jnp.full((qb, 1), S, i32))

    def bias_chunk(c, carry):
        k = key_ref[:, chunk(c)]
        sel = (k > tau) | ((k == tau) & (lane + c * kc <= m_idx))
        bias_ref[:, chunk(c)] = jnp.where(sel, 0.0, NEG_INF)
        return carry

    lax.fori_loop(0, nk, bias_chunk, 0)

    q = q_ref[...]
    pair = LANES // DSA_HEAD_DIM

    def fold(x, op, acc):
        for j in range(kc // LANES):
            acc = op(acc, x[:, j * LANES:(j + 1) * LANES])
        return acc

    heads = range(DSA_HEADS)
    head_rows = [slice(h * DSA_HEAD_DIM, (h + 1) * DSA_HEAD_DIM) for h in heads]
    mx_ref[...] = jnp.full(mx_ref.shape, NEG_INF, f32)
    ls_ref[...] = jnp.zeros_like(ls_ref)
    acc_ref[...] = jnp.zeros_like(acc_ref)

    def logits(c, carry):
        bias = bias_ref[:, chunk(c)]
        for h in heads:
            lg = _dot(q[:, head_rows[h]], kt_ref[head_rows[h], chunk(c)]) + bias
            lg_ref[h, :, chunk(c)] = lg
            mx_ref[h] = fold(lg, jnp.maximum, mx_ref[h])
        return carry

    lax.fori_loop(0, nk, logits, 0)
    ms = [jnp.max(mx_ref[h], axis=1, keepdims=True) for h in heads]

    def weigh(c, carry):
        for h in heads:
            p = jnp.exp(lg_ref[h, :, chunk(c)] - ms[h])
            ls_ref[h] = fold(p, jnp.add, ls_ref[h])
            slab = slice((h // pair) * LANES, (h // pair + 1) * LANES)
            acc_ref[h] = acc_ref[h] + _dot(p.astype(bf16), v_ref[chunk(c), slab])
        return carry

    lax.fori_loop(0, nk, weigh, 0)
    for h in heads:
        l = jnp.sum(ls_ref[h], axis=1, keepdims=True)
        off = (h % pair) * DSA_HEAD_DIM
        o_ref[:, head_rows[h]] = (acc_ref[h][:, off:off + DSA_HEAD_DIM] / l).astype(o_ref.dtype)


def _dsa(q, kt, v, qi, kit, misc, qb, kc):
    B, S, _ = q.shape
    n_sel = min(TOPK_MAX, S // 4)
    blk = lambda n: pl.BlockSpec((None, qb, n), lambda b, i: (b, i, 0))
    per_b = lambda r, c: pl.BlockSpec((None, r, c), lambda b, i: (b, 0, 0))
    kern = functools.partial(_dsa_kernel, n_sel=n_sel, idx_bits=max(1, (S - 1).bit_length()), kc=kc)
    return pl.pallas_call(
        kern, grid=(B, S // qb), out_shape=jax.ShapeDtypeStruct((B, S, W_DSA), bf16), out_specs=blk(W_DSA),
        in_specs=[blk(W_DSA), per_b(W_DSA, S), per_b(S, W_DSA), blk(W_IDX), per_b(IDX_DIM, S), blk(LANES)],
        scratch_shapes=[pltpu.VMEM((qb, S), i32), pltpu.VMEM((qb, S), f32), pltpu.VMEM((DSA_HEADS, qb, S), f32)]
        + [pltpu.VMEM((DSA_HEADS, qb, LANES), f32)] * 3,
        compiler_params=_params("parallel", "arbitrary"), name="dsa",
    )(q, kt, v, qi, kit, misc)


def _gla_kernel(gq_ref, gk_ref, gv_ref, la_ref, gr_ref, ng_ref, o_ref, state_ref):
    ct = gq_ref.shape[0]
    nch = ct // GLA_CHUNK

    @pl.when(pl.program_id(1) == 0)
    def _():
        state_ref[...] = jnp.zeros_like(state_ref)

    la = la_ref[...]
    r = lax.broadcasted_iota(i32, (ct, ct), 0)
    c = lax.broadcasted_iota(i32, (ct, ct), 1)
    same = (r // GLA_CHUNK) == (c // GLA_CHUNK)
    causal = same & (c <= r)
    hp = lax.Precision.HIGHEST
    bcum = _dot(jnp.where(causal, 1.0, 0.0), la, precision=hp)
    blast = _dot(jnp.where(same, 1.0, 0.0), la, precision=hp)
    q_dec = (gq_ref[...] * (GLA_DK ** -0.5) * jnp.exp(bcum)).astype(bf16)
    k_inv = (gk_ref[...] * jnp.exp(-bcum)).astype(bf16)
    k_end = (gk_ref[...] * jnp.exp(blast - bcum)).astype(bf16)
    decay = jnp.exp(blast)
    ng = ng_ref[...]
    for h in range(GLA_HEADS):
        ks = slice(h * GLA_DK, (h + 1) * GLA_DK)
        vs = slice(h * GLA_DV, (h + 1) * GLA_DV)
        qd, ki, ke, vh = q_dec[:, ks], k_inv[:, ks], k_end[:, ks], gv_ref[:, vs]
        attn = jnp.where(causal, _dot(qd, ki, _NT), 0.0)
        o = _dot(attn.astype(bf16), vh)
        st = state_ref[h]
        inter = []
        for n in range(nch):
            rows = slice(n * GLA_CHUNK, (n + 1) * GLA_CHUNK)
            inter.append(_dot(qd[rows], st.astype(bf16), _NT))
            st = st * decay[n * GLA_CHUNK:n * GLA_CHUNK + 1, ks] + _dot(vh[rows], ke[rows], _TN)
        state_ref[h] = st
        o = o + jnp.concatenate(inter, axis=0)
        o = o * lax.rsqrt(jnp.mean(o * o, axis=-1, keepdims=True) + RMS_EPS) * ng
        g = gr_ref[:, vs]
        o_ref[:, vs] = (o * (g * jax.nn.sigmoid(g))).astype(o_ref.dtype)


def _gla(gq, gk, gv, la, gr, norm_g, ct):
    B, S, _ = gq.shape
    blk = lambda n: pl.BlockSpec((None, ct, n), lambda b, j: (b, j, 0))
    ng = norm_g.reshape(1, GLA_DV)
    return pl.pallas_call(
        _gla_kernel, grid=(B, S // ct), out_shape=jax.ShapeDtypeStruct((B, S, W_GV), bf16), out_specs=blk(W_GV),
        in_specs=[blk(W_GQK), blk(W_GQK), blk(W_GV), blk(W_GQK), blk(W_GV), pl.BlockSpec(ng.shape, lambda b, j: (0, 0))],
        scratch_shapes=[pltpu.VMEM((GLA_HEADS, GLA_DV, GLA_DK), f32)],
        compiler_params=_params("parallel", "arbitrary"), name="gla",
    )(gq, gk, gv, la, gr, ng)


def _mix_out_kernel(x_ref, ya_ref, yb_ref, wa_ref, wb_ref, g_ref, b_ref, o_ref, *, alpha):
    mix = _dot(ya_ref[...], wa_ref[...]) + _dot(yb_ref[...], wb_ref[...])
    o_ref[...] = _layer_norm(alpha * x_ref[...] + mix, g_ref[...], b_ref[...])


def _mix_out(x2, ya, yb, w_out, g, b, alpha, tm):
    T, D = x2.shape
    wa, wb = w_out[:W_DSA].astype(bf16), w_out[W_DSA:].astype(bf16)
    tok = lambda n: pl.BlockSpec((tm, n), lambda i: (i, 0))
    full = lambda a: pl.BlockSpec(a.shape, lambda i: (0, 0))
    g, b = g.reshape(1, D), b.reshape(1, D)
    return pl.pallas_call(
        functools.partial(_mix_out_kernel, alpha=alpha), grid=(T // tm,),
        out_shape=jax.ShapeDtypeStruct((T, D), f32), out_specs=tok(D),
        in_specs=[tok(D), tok(W_DSA), tok(W_GV), full(wa), full(wb), full(g), full(b)],
        compiler_params=_params("parallel"), name="mix_out",
    )(x2, ya, yb, wa, wb, g, b)


def _mem_kv_kernel(m_ref, wk_ref, wv_ref, k_ref, v_ref):
    m = m_ref[...].astype(bf16)
    k_ref[...] = _dot(m, wk_ref[...]).astype(bf16)
    v_ref[...] = _dot(m, wv_ref[...]).astype(bf16)


def _mem_kv(mem, w_k, w_v):
    B, M, D = mem.shape
    wk, wv = w_k.astype(bf16), w_v.astype(bf16)
    blk = pl.BlockSpec((None, M, D), lambda b: (b, 0, 0))
    full = pl.BlockSpec((D, D), lambda b: (0, 0))
    sd = jax.ShapeDtypeStruct((B, M, D), bf16)
    return pl.pallas_call(_mem_kv_kernel, grid=(B,), out_shape=[sd, sd], out_specs=[blk, blk],
                          in_specs=[blk, full, full], compiler_params=_params("parallel"), name="mem_kv")(mem, wk, wv)


def _xattn_kernel(h_ref, k_ref, v_ref, wq_ref, wo_ref, g_ref, b_ref, o_ref, *, alpha):
    h = h_ref[...]
    D = h.shape[1]
    hd = D // XATTN_HEADS
    q = (_dot(h.astype(bf16), wq_ref[...]) * (hd ** -0.5)).astype(bf16)
    outs = []
    for a in range(XATTN_HEADS):
        s = slice(a * hd, (a + 1) * hd)
        lg = _dot(q[:, s], k_ref[:, s], _NT)
        p = jnp.exp(lg - jnp.max(lg, axis=1, keepdims=True))
        l = jnp.sum(p, axis=1, keepdims=True)
        outs.append((_dot(p.astype(bf16), v_ref[:, s]) / l).astype(bf16))
    ca = _dot(jnp.concatenate(outs, axis=1), wo_ref[...])
    o_ref[...] = _layer_norm(alpha * h + ca, g_ref[...], b_ref[...])


def _xattn(h1, km, vm, w_q, w_o, g, b, alpha, tm):
    B, S, D = h1.shape
    M = km.shape[1]
    wq, wo = w_q.astype(bf16), w_o.astype(bf16)
    g, b = g.reshape(1, D), b.reshape(1, D)
    tok = pl.BlockSpec((None, tm, D), lambda bi, j: (bi, j, 0))
    per_b = pl.BlockSpec((None, M, D), lambda bi, j: (bi, 0, 0))
    full = lambda a: pl.BlockSpec(a.shape, lambda bi, j: (0, 0))
    return pl.pallas_call(
        functools.partial(_xattn_kernel, alpha=alpha), grid=(B, S // tm),
        out_shape=jax.ShapeDtypeStruct((B, S, D), f32), out_specs=tok,
        in_specs=[tok, per_b, per_b, full(wq), full(wo), full(g), full(b)],
        compiler_params=_params("parallel", "parallel"), name="xattn",
    )(h1, km, vm, wq, wo, g, b)


def _top_rows(s, n_top, payload=None):
    n = s.shape[0]
    rows = lax.broadcasted_iota(i32, s.shape, 0)
    vals, picks = [], []
    for _ in range(n_top):
        m = jnp.max(s, axis=0, keepdims=True)
        am = jnp.min(jnp.where(s == m, rows, n), axis=0, keepdims=True)
        hit = rows == am
        vals.append(m)
        picks.append(am if payload is None else jnp.max(jnp.where(hit, payload, -1), axis=0, keepdims=True))
        s = jnp.where(hit, NEG_INF, s)
    return jnp.concatenate(vals, axis=0), jnp.concatenate(picks, axis=0)


def _route_kernel(h_ref, wq_ref, k1_ref, k2_ref, row_ref, ne_ref, g_ref):
    q = _dot(h_ref[...].astype(bf16), wq_ref[...])
    dk = PEER_D_KEY // 2
    k1, k2 = k1_ref[...], k2_ref[...]
    for a in range(PEER_HEADS):
        qa = q[:, a * PEER_D_KEY:a * PEER_D_KEY + dk].astype(bf16)
        qb = q[:, a * PEER_D_KEY + dk:(a + 1) * PEER_D_KEY].astype(bf16)
        v1, i1 = _top_rows(_dot(k1, qa, _NT), PEER_TOPK)
        v2, i2 = _top_rows(_dot(k2, qb, _NT), PEER_TOPK)
        cand = jnp.concatenate([v1[r:r + 1] + v2 for r in range(PEER_TOPK)], axis=0)
        cidx = jnp.concatenate([i1[r:r + 1] * PEER_N_KEYS + i2 for r in range(PEER_TOPK)], axis=0)
        top, experts = _top_rows(cand, PEER_TOPK, payload=cidx)
        p = jnp.exp(top - top[0:1])
        gates = p / jnp.sum(p, axis=0, keepdims=True)
        for half in range(PEER_TOPK // SUBLANES):
            grp = slice(half * SUBLANES, (half + 1) * SUBLANES)
            e, g = experts[grp], gates[grp]
            odd = e & 1
            n_even = SUBLANES - jnp.sum(odd, axis=0, keepdims=True)
            sub = lax.broadcasted_iota(i32, e.shape, 0)
            evens_before = jnp.zeros_like(n_even)
            e_sorted, g_sorted = jnp.zeros_like(e), jnp.zeros_like(g)
            for r in range(SUBLANES):
                odd_r = odd[r:r + 1]
                dest = jnp.where(odd_r == 1, n_even + (r - evens_before), evens_before)
                hit = sub == dest
                e_sorted = jnp.where(hit, e[r:r + 1], e_sorted)
                g_sorted = jnp.where(hit, g[r:r + 1], g_sorted)
                evens_before = evens_before + (1 - odd_r)
            rows = slice(a * PEER_TOPK + half * SUBLANES, a * PEER_TOPK + (half + 1) * SUBLANES)
            row_ref[rows, :] = (e_sorted >> 1) * SUBLANES
            g_ref[rows, :] = g_sorted
            n = a * (PEER_TOPK // SUBLANES) + half
            ne_ref[n:n + 1, :] = n_even


def _route(h2, w_query, k1, k2, tm):
    T, D = h2.shape
    wq = w_query.astype(bf16)
    k1, k2 = k1.astype(bf16), k2.astype(bf16)
    hk = PEER_HEADS * PEER_TOPK
    full = lambda a: pl.BlockSpec(a.shape, lambda i: (0, 0))
    out = pl.BlockSpec((hk, tm), lambda i: (0, i))
    n_groups = hk // SUBLANES
    sd = jax.ShapeDtypeStruct
    return pl.pallas_call(
        _route_kernel, grid=(T // tm,),
        out_shape=[sd((hk, T), i32), sd((n_groups, T), i32), sd((hk, T), f32)],
        out_specs=[out, pl.BlockSpec((n_groups, tm), lambda i: (0, i)), out],
        in_specs=[pl.BlockSpec((tm, D), lambda i: (i, 0)), full(wq), full(k1), full(k2)],
        compiler_params=_params("parallel"), name="route",
    )(h2, wq, k1, k2)


def _pack_table(tab):
    n, d = tab.shape
    assert d == SUBLANES * LANES
    u = lax.bitcast_convert_type(tab.astype(bf16), jnp.uint16).astype(jnp.uint32).reshape(n // 2, 2, SUBLANES, LANES)
    return lax.bitcast_convert_type((u[:, 0] << 16) | u[:, 1], i32).reshape(n // 2 * SUBLANES, LANES)


def _shift_patterns():
    n_even = np.arange(SUBLANES + 1)[:, None, None]
    p = np.arange(SUBLANES)[None, :, None]
    pat = np.where(p < n_even, 0, 16) + np.zeros((1, 1, LANES), np.int64)
    return jnp.asarray(pat.reshape(-1, LANES), i32)


def _unpack(w, sh_ref, n_even, p):
    shift = jnp.broadcast_to(sh_ref[pl.ds(n_even * SUBLANES + p, 1), :], w.shape)
    return pltpu.bitcast(jnp.left_shift(w, shift) & jnp.int32(-65536), f32)


def _row(tab_ref, row0, sh_ref, n_even, p):
    return _unpack(tab_ref[pl.ds(pl.multiple_of(row0, SUBLANES), SUBLANES), :], sh_ref, n_even, p)


TOKEN_BATCH = 8
GROUP_UNROLL = 4


_TREE_ORDER = (0, 4, 2, 6, 1, 5, 3, 7)


def _merge(p, q, mask, shift):
    return jnp.where(mask, p, q) + pltpu.roll(jnp.where(mask, q, p), shift, 0)


def _sublane_sums(ps):
    sub = lax.broadcasted_iota(i32, ps[0].shape, 0)
    quad = lambda a, b, o: _merge(a, b, ((sub - o) & 7) < 4, 4)
    duo = lambda a, b, o: _merge(a, b, ((sub - o) & 3) < 2, 6)
    r1 = duo(quad(ps[0], ps[1], 0), quad(ps[2], ps[3], 2), 0)
    r2 = duo(quad(ps[4], ps[5], 1), quad(ps[6], ps[7], 3), 1)
    return _merge(r1, r2, (sub & 1) == 0, 7)


def _peer_down_kernel(row_ref, ne_ref, x_ref, gt_ref, sh_ref, tab_ref, ct_ref, part_ref, actt_ref):
    hk, tb = ct_ref.shape
    n_groups = hk // SUBLANES
    lane = lax.broadcasted_iota(i32, (hk, tb), 1)

    def batch(b, carry):
        t0 = b * TOKEN_BATCH

        def token(u, c1):
            t = t0 + u
            xt = x_ref[t]

            def group(g, c2):
                k0 = pl.multiple_of(g * SUBLANES, SUBLANES)
                n_even = ne_ref[t * n_groups + g]
                ps = [_row(tab_ref, row_ref[t * hk + k0 + k], sh_ref, n_even, k) * xt for k in _TREE_ORDER]
                part_ref[u, pl.ds(k0, SUBLANES), :] = _sublane_sums(ps)
                return c2

            lax.fori_loop(0, n_groups, group, 0, unroll=4 * GROUP_UNROLL)
            return c1

        lax.fori_loop(0, TOKEN_BATCH, token, 0)
        a = actt_ref[...]
        for u in range(TOKEN_BATCH):
            a = jnp.where(lane == t0 + u, jnp.sum(part_ref[u], axis=1, keepdims=True), a)
        actt_ref[...] = a
        return carry

    lax.fori_loop(0, tb // TOKEN_BATCH, batch, 0)
    a = actt_ref[...]
    gelu = 0.5 * a * (1.0 + lax.erf(a * (2.0 ** -0.5)))
    ct_ref[...] = gt_ref[...] * gelu


def _peer_specs(tb, hk):
    per_lookup = pl.BlockSpec((tb * hk,), lambda i: (i,), memory_space=pltpu.SMEM)
    per_group = pl.BlockSpec((tb * hk // SUBLANES,), lambda i: (i,), memory_space=pltpu.SMEM)
    whole = pl.BlockSpec(memory_space=pltpu.VMEM)
    return per_lookup, per_group, whole


def _peer_down(rows, n_even, x3, gates_t, tab, tb):
    hk, T = gates_t.shape
    per_lookup, per_group, whole = _peer_specs(tb, hk)
    per_k = pl.BlockSpec((hk, tb), lambda i: (0, i))
    return pl.pallas_call(
        _peer_down_kernel, grid=(T // tb,), out_shape=jax.ShapeDtypeStruct((hk, T), f32), out_specs=per_k,
        in_specs=[per_lookup, per_group, pl.BlockSpec((tb,) + x3.shape[1:], lambda i: (i, 0, 0)), per_k, whole, whole],
        scratch_shapes=[pltpu.VMEM((TOKEN_BATCH, hk, LANES), f32), pltpu.VMEM((hk, tb), f32)],
        compiler_params=_params("arbitrary"), name="peer_down",
    )(rows, n_even, x3, gates_t, _shift_patterns(), tab)


def _peer_up_kernel(row_ref, ne_ref, ct_ref, sh_ref, tab_ref, o_ref, cx_ref):
    hk, tb = ct_ref.shape
    n_groups = hk // SUBLANES
    n_acc = 4
    vreg = o_ref.shape[1:]
    lane = lax.broadcasted_iota(i32, (hk, tb), 1)

    def batch(b, carry):
        t0 = b * TOKEN_BATCH
        ct = ct_ref[...]
        for u in range(TOKEN_BATCH):
            col = jnp.sum(jnp.where(lane == t0 + u, ct, 0.0), axis=1, keepdims=True)
            cx_ref[u] = jnp.broadcast_to(col, (hk, LANES))

        def token(u, c1):
            t = t0 + u

            def fetch(g):
                rows = row_ref.at[pl.ds(t * hk + g * SUBLANES, SUBLANES)]
                return tuple(tab_ref[pl.ds(pl.multiple_of(rows[p], SUBLANES), SUBLANES), :] for p in range(SUBLANES))

            def group(g, carry2):
                accs, packed = list(carry2[0]), carry2[1]
                nxt = fetch((g + 1) & (n_groups - 1))
                k0 = pl.multiple_of(g * SUBLANES, SUBLANES)
                n_even = ne_ref[t * n_groups + g]
                for p in range(SUBLANES):
                    coef = jnp.broadcast_to(cx_ref[u, pl.ds(k0 + p, 1), :], vreg)
                    accs[p % n_acc] = accs[p % n_acc] + _unpack(packed[p], sh_ref, n_even, p) * coef
                return tuple(accs), nxt

            accs, _ = lax.fori_loop(0, n_groups, group, ((jnp.zeros(vreg, f32),) * n_acc, fetch(0)),
                                    unroll=GROUP_UNROLL)
            o_ref[t] = (accs[0] + accs[1]) + (accs[2] + accs[3])
            return c1

        lax.fori_loop(0, TOKEN_BATCH, token, 0)
        return carry

    lax.fori_loop(0, tb // TOKEN_BATCH, batch, 0)


def _peer_up(rows, n_even, coef_t, tab, tb):
    hk, T = coef_t.shape
    per_lookup, per_group, whole = _peer_specs(tb, hk)
    return pl.pallas_call(
        _peer_up_kernel, grid=(T // tb,), out_shape=jax.ShapeDtypeStruct((T, SUBLANES, LANES), f32),
        out_specs=pl.BlockSpec((tb, SUBLANES, LANES), lambda i: (i, 0, 0)),
        in_specs=[per_lookup, per_group, pl.BlockSpec((hk, tb), lambda i: (0, i)), whole, whole],
        scratch_shapes=[pltpu.VMEM((TOKEN_BATCH, hk, LANES), f32)],
        compiler_params=_params("arbitrary"), name="peer_up",
    )(rows, n_even, coef_t, _shift_patterns(), tab)


def _ffn_out_kernel(h_ref, f_ref, g_ref, b_ref, o_ref, *, alpha):
    o_ref[...] = _layer_norm(alpha * h_ref[...] + f_ref[...], g_ref[...], b_ref[...])


def _ffn_out(h2, ff, g, b, alpha, tm):
    T, D = h2.shape
    g, b = g.reshape(1, D), b.reshape(1, D)
    tok = pl.BlockSpec((tm, D), lambda i: (i, 0))
    full = pl.BlockSpec((1, D), lambda i: (0, 0))
    return pl.pallas_call(
        functools.partial(_ffn_out_kernel, alpha=alpha), grid=(T // tm,),
        out_shape=jax.ShapeDtypeStruct((T, D), f32), out_specs=tok, in_specs=[tok, tok, full, full],
        compiler_params=_params("parallel"), name="ffn_out",
    )(h2, ff, g, b)


def _tile(n, want):
    t = min(n, want)
    assert n % t == 0, (n, t)
    return t


def kernel(x, positions, mem, w_in, gla_gate_up, gla_gate_bias, gla_norm_g, w_out, ln_mix_g, ln_mix_b, xattn_w_q, xattn_w_k, xattn_w_v, xattn_w_o, ln_mem_g, ln_mem_b, peer_w_query, peer_sub_keys_1, peer_sub_keys_2, peer_expert_down, peer_expert_up, ln_ffn_g, ln_ffn_b):
    B, S, D = x.shape
    T = B * S
    depth = w_in.shape[0]
    alpha = (2.0 * depth) ** 0.25
    tm = _tile(S, 512)
    h = x
    for l in range(depth):
        q, kt, v, qi, kit, misc, gq, gk, gv, la, gr = _proj(h, positions, w_in[l], gla_gate_up[l], gla_gate_bias[l], tm)
        y_dsa = _dsa(q, kt, v, qi, kit, misc, _tile(S, 128), _tile(S, 512))
        y_gla = _gla(gq, gk, gv, la, gr, gla_norm_g[l], tm)
        h1 = _mix_out(h.reshape(T, D), y_dsa.reshape(T, W_DSA), y_gla.reshape(T, W_GV), w_out[l],
                      ln_mix_g[l], ln_mix_b[l], alpha, tm)
        km, vm = _mem_kv(mem, xattn_w_k[l], xattn_w_v[l])
        h2 = _xattn(h1.reshape(B, S, D), km, vm, xattn_w_q[l], xattn_w_o[l], ln_mem_g[l], ln_mem_b[l], alpha, tm)
        h2 = h2.reshape(T, D)
        rows_t, ne_t, gates_t = _route(h2, peer_w_query[l], peer_sub_keys_1[l], peer_sub_keys_2[l], _tile(T, 256))
        flat = lambda a: a.T.reshape(-1)
        rows, n_even = flat(rows_t), flat(ne_t)
        tb = _tile(T, LANES)
        coef_t = _peer_down(rows, n_even, h2.reshape(T, SUBLANES, LANES), gates_t, _pack_table(peer_expert_down[l]), tb)
        ff = _peer_up(rows, n_even, coef_t, _pack_table(peer_expert_up[l]), tb)
        h = _ffn_out(h2, ff.reshape(T, D), ln_ffn_g[l], ln_ffn_b[l], alpha, tm).reshape(B, S, D)
    return h
```

```python
import functools

import jax
import jax.numpy as jnp
import numpy as np
from jax import lax
from jax.experimental import pallas as pl
from jax.experimental.pallas import tpu as pltpu

f32 = jnp.float32
bf16 = jnp.bfloat16
i32 = jnp.int32

DSA_HEADS = 8
DSA_HEAD_DIM = 64
IDX_HEADS = 8
IDX_DIM = 32
TOPK_MAX = 256
GLA_HEADS = 4
GLA_DK = 64
GLA_DV = 128
GLA_GATE_RANK = 16
GLA_GATE_TEMP = 16.0
GLA_CHUNK = 64
ROPE_THETA = 500000.0
ROPE_FRACTION = 4
XATTN_HEADS = 4
PEER_N_KEYS = 128
PEER_HEADS = 8
PEER_D_KEY = 256
PEER_TOPK = 16
LN_EPS = 1e-5
RMS_EPS = 1e-6

LANES = 128
SUBLANES = 8
VMEM_LIMIT = 56 * 1024 * 1024

INT_MIN = -(2 ** 31)
NEG_INF = float("-inf")

W_DSA = DSA_HEADS * DSA_HEAD_DIM
W_IDX = IDX_HEADS * IDX_DIM
W_GQK = GLA_HEADS * GLA_DK
W_GV = GLA_HEADS * GLA_DV
MISC_KI = 0
MISC_WI = IDX_DIM
MISC_LR = IDX_DIM + IDX_HEADS


def _dot(a, b, dims=(((1,), (0,)), ((), ())), precision=None):
    return lax.dot_general(a, b, dims, precision=precision, preferred_element_type=f32)


_NN = (((1,), (0,)), ((), ()))
_NT = (((1,), (1,)), ((), ()))
_TN = (((0,), (0,)), ((), ()))


def _params(*sem):
    return pltpu.CompilerParams(dimension_semantics=sem, vmem_limit_bytes=VMEM_LIMIT)


def _layer_norm(y, g, b):
    mu = jnp.mean(y, axis=-1, keepdims=True)
    yc = y - mu
    var = jnp.mean(yc * yc, axis=-1, keepdims=True)
    return yc * lax.rsqrt(var + LN_EPS) * g + b


def _rot(xb, c, sa, sb, half):
    return xb * c + pltpu.roll(xb, LANES - half, 1) * sa + pltpu.roll(xb, half, 1) * sb


def _proj_kernel(x_ref, w_ref, gup_ref, gb_ref, cq_ref, saq_ref, sbq_ref, ci_ref, sai_ref, sbi_ref,
                 q_ref, kt_ref, v_ref, qi_ref, kit_ref, misc_ref, gq_ref, gk_ref, gv_ref, la_ref, gr_ref):
    x = x_ref[...].astype(bf16)
    tm = x.shape[0]
    cq, saq, sbq = cq_ref[...], saq_ref[...], sbq_ref[...]
    ci, sai, sbi = ci_ref[...], sai_ref[...], sbi_ref[...]
    hq = DSA_HEAD_DIM // ROPE_FRACTION // 2
    hi = IDX_DIM // ROPE_FRACTION // 2
    o = 0
    scale = DSA_HEAD_DIM ** -0.5
    for j in range(W_DSA // LANES):
        a = _dot(x, w_ref[:, o + j * LANES:o + (j + 1) * LANES])
        q_ref[:, j * LANES:(j + 1) * LANES] = (_rot(a, cq, saq, sbq, hq) * scale).astype(bf16)
    o += W_DSA
    for j in range(W_DSA // LANES):
        a = _dot(x, w_ref[:, o + j * LANES:o + (j + 1) * LANES])
        kt_ref[j * LANES:(j + 1) * LANES, :] = _rot(a, cq, saq, sbq, hq).T.astype(bf16)
    o += W_DSA
    v_ref[...] = _dot(x, w_ref[:, o:o + W_DSA]).astype(bf16)
    o += W_DSA
    for j in range(W_IDX // LANES):
        a = _dot(x, w_ref[:, o + j * LANES:o + (j + 1) * LANES])
        qi_ref[:, j * LANES:(j + 1) * LANES] = _rot(a, ci, sai, sbi, hi).astype(bf16)
    o += W_IDX
    m = _dot(x, w_ref[:, o:o + LANES])
    lane = lax.broadcasted_iota(i32, (tm, LANES), 1)
    is_ki = lane < IDX_DIM
    m = _rot(m, jnp.where(is_ki, ci, 1.0), jnp.where(is_ki, sai, 0.0), jnp.where(is_ki, sbi, 0.0), hi)
    misc_ref[...] = m
    kit_ref[...] = m.T[:IDX_DIM, :].astype(bf16)
    z = _dot(m.astype(bf16), gup_ref[...]) + gb_ref[...]
    la_ref[...] = (jnp.minimum(z, 0.0) - jnp.log1p(jnp.exp(-jnp.abs(z)))) / GLA_GATE_TEMP
    o += LANES
    gq_ref[...] = _dot(x, w_ref[:, o:o + W_GQK])
    o += W_GQK
    gk_ref[...] = _dot(x, w_ref[:, o:o + W_GQK])
    o += W_GQK
    gv_ref[...] = _dot(x, w_ref[:, o:o + W_GV]).astype(bf16)
    o += W_GV
    gr_ref[...] = _dot(x, w_ref[:, o:o + W_GV])


def _rot_tables(positions, head_dim):
    r = head_dim // ROPE_FRACTION
    half = r // 2
    inv_freq = ROPE_THETA ** (-jnp.arange(half, dtype=f32) / half)
    ang = positions.astype(f32)[..., None] * inv_freq
    cos, sin = jnp.cos(ang), jnp.sin(ang)
    lane = np.arange(LANES) % head_dim
    src = np.where(lane < half, lane, np.clip(lane - half, 0, half - 1))
    cos_l, sin_l = cos[..., src], sin[..., src]
    c = jnp.where(lane < r, cos_l, 1.0)
    sa = jnp.where(lane < half, -sin_l, 0.0)
    sb = jnp.where((lane >= half) & (lane < r), sin_l, 0.0)
    return c, sa, sb


def _proj(x, positions, w_in, gate_up, gate_bias, tm):
    B, S, D = x.shape
    splits = np.cumsum([W_DSA, W_DSA, W_DSA, W_IDX, IDX_DIM, IDX_HEADS, W_GQK, W_GQK, W_GV, GLA_GATE_RANK])
    (wq, wk, wv, wqi, wki, wwi, wgq, wgk, wgv, wlr, wgr) = jnp.split(w_in, splits.tolist(), axis=1)
    pad = jnp.zeros((D, LANES - IDX_DIM - IDX_HEADS - GLA_GATE_RANK), w_in.dtype)
    w_a = jnp.concatenate([wq, wk, wv, wqi, wki, wwi, wlr, pad, wgq, wgk, wgv, wgr], axis=1).astype(bf16)
    gup = jnp.zeros((LANES, W_GQK), f32).at[MISC_LR:MISC_LR + GLA_GATE_RANK].set(gate_up).astype(bf16)
    tabs = _rot_tables(positions, DSA_HEAD_DIM) + _rot_tables(positions, IDX_DIM)
    W = w_a.shape[1]
    tok = lambda n: pl.BlockSpec((None, tm, n), lambda b, j: (b, j, 0))
    full = lambda a: pl.BlockSpec(a.shape, lambda b, j: (0,) * a.ndim)
    tr = lambda n: pl.BlockSpec((None, n, tm), lambda b, j: (b, 0, j))
    sd = jax.ShapeDtypeStruct
    out_shape = [sd((B, S, W_DSA), bf16), sd((B, W_DSA, S), bf16), sd((B, S, W_DSA), bf16), sd((B, S, W_IDX), bf16),
                 sd((B, IDX_DIM, S), bf16), sd((B, S, LANES), f32), sd((B, S, W_GQK), f32), sd((B, S, W_GQK), f32),
                 sd((B, S, W_GV), bf16), sd((B, S, W_GQK), f32), sd((B, S, W_GV), f32)]
    out_specs = [tok(W_DSA), tr(W_DSA), tok(W_DSA), tok(W_IDX), tr(IDX_DIM), tok(LANES), tok(W_GQK), tok(W_GQK),
                 tok(W_GV), tok(W_GQK), tok(W_GV)]
    gb = gate_bias.reshape(1, W_GQK)
    return pl.pallas_call(
        _proj_kernel, grid=(B, S // tm), out_shape=out_shape, out_specs=out_specs,
        in_specs=[tok(D), full(w_a), full(gup), full(gb)] + [tok(LANES)] * 6,
        compiler_params=_params("parallel", "parallel"), name="proj",
    )(x, w_a, gup, gb, *tabs)


def _dsa_kernel(q_ref, kt_ref, v_ref, qi_ref, kit_ref, misc_ref, o_ref, key_ref, bias_ref, lg_ref, mx_ref, ls_ref,
                acc_ref, *, n_sel, idx_bits, kc):
    qb, S = q_ref.shape[0], kt_ref.shape[1]
    i = pl.program_id(1)
    nk = lax.div((i + 1) * qb + (kc - 1), kc)
    qi = qi_ref[...]
    wi = misc_ref[:, MISC_WI:MISC_WI + IDX_HEADS] * (IDX_HEADS ** -0.5)
    qpos = lax.broadcasted_iota(i32, (qb, 1), 0) + i * qb
    target = jnp.minimum(n_sel, qpos + 1).astype(f32)
    lane = lax.broadcasted_iota(i32, (qb, kc), 1)

    def chunk(c):
        return pl.ds(pl.multiple_of(c * kc, kc), kc)

    def score_chunk(c, carry):
        kit = kit_ref[:, chunk(c)]
        score = jnp.zeros((qb, kc), f32)
        for h in range(IDX_HEADS):
            d = _dot(qi[:, h * IDX_DIM:(h + 1) * IDX_DIM], kit) * (IDX_DIM ** -0.5)
            score = score + wi[:, h:h + 1] * jnp.maximum(d, 0.0)
        score = jnp.where(score == 0.0, 0.0, score)
        bits = pltpu.bitcast(score, i32)
        key = bits ^ ((bits >> 31) & jnp.int32(0x7FFFFFFF))
        key_ref[:, chunk(c)] = jnp.where(lane + c * kc <= qpos, key, INT_MIN)
        return carry

    lax.fori_loop(0, nk, score_chunk, 0)

    def count(pred):
        def body(c, acc):
            hit = jnp.where(pred(key_ref[:, chunk(c)], lane + c * kc), 1.0, 0.0)
            for j in range(kc // LANES):
                acc = acc + hit[:, j * LANES:(j + 1) * LANES]
            return acc

        acc = lax.fori_loop(0, nk, body, jnp.zeros((qb, LANES), f32))
        return jnp.sum(acc, axis=1, keepdims=True)

    base = jnp.where(count(lambda k, col: k >= 0) >= target, jnp.int32(0), jnp.int32(INT_MIN))

    def tau_bit(t, base):
        cand = base | jnp.left_shift(jnp.int32(1), 30 - t)
        return jnp.where(count(lambda k, col: k >= cand) >= target, cand, base)

    tau = lax.fori_loop(0, 31, tau_bit, base)
    excess = jnp.max(count(lambda k, col: k >= tau) - target)

    def tie_limit():
        need = target - count(lambda k, col: k > tau)

        def idx_bit(t, m):
            cand = m | jnp.left_shift(jnp.int32(1), idx_bits - 1 - t)
            return jnp.where(count(lambda k, col: (k == tau) & (col < cand)) < need, cand, m)

        return lax.fori_loop(0, idx_bits, idx_bit, jnp.zeros((qb, 1), i32))

    m_idx = lax.cond(excess > 0.0, tie_limit, lambda: jnp.full((qb, 1), S, i32))

    def bias_chunk(c, carry):
        k = key_ref[:, chunk(c)]
        sel = (k > tau) | ((k == tau) & (lane + c * kc <= m_idx))
        bias_ref[:, chunk(c)] = jnp.where(sel, 0.0, NEG_INF)
        return carry

    lax.fori_loop(0, nk, bias_chunk, 0)

    q = q_ref[...]
    pair = LANES // DSA_HEAD_DIM

    def fold(x, op, acc):
        for j in range(kc // LANES):
            acc = op(acc, x[:, j * LANES:(j + 1) * LANES])
        return acc

    heads = range(DSA_HEADS)
    head_rows = [slice(h * DSA_HEAD_DIM, (h + 1) * DSA_HEAD_DIM) for h in heads]
    mx_ref[...] = jnp.full(mx_ref.shape, NEG_INF, f32)
    ls_ref[...] = jnp.zeros_like(ls_ref)
    acc_ref[...] = jnp.zeros_like(acc_ref)

    def logits(c, carry):
        bias = bias_ref[:, chunk(c)]
        for h in heads:
            lg = _dot(q[:, head_rows[h]], kt_ref[head_rows[h], chunk(c)]) + bias
            lg_ref[h, :, chunk(c)] = lg
            mx_ref[h] = fold(lg, jnp.maximum, mx_ref[h])
        return carry

    lax.fori_loop(0, nk, logits, 0)
    ms = [jnp.max(mx_ref[h], axis=1, keepdims=True) for h in heads]

    def weigh(c, carry):
        for h in heads:
            p = jnp.exp(lg_ref[h, :, chunk(c)] - ms[h])
            ls_ref[h] = fold(p, jnp.add, ls_ref[h])
            slab = slice((h // pair) * LANES, (h // pair + 1) * LANES)
            acc_ref[h] = acc_ref[h] + _dot(p.astype(bf16), v_ref[chunk(c), slab])
        return carry

    lax.fori_loop(0, nk, weigh, 0)
    for h in heads:
        l = jnp.sum(ls_ref[h], axis=1, keepdims=True)
        off = (h % pair) * DSA_HEAD_DIM
        o_ref[:, head_rows[h]] = (acc_ref[h][:, off:off + DSA_HEAD_DIM] / l).astype(o_ref.dtype)


def _dsa(q, kt, v, qi, kit, misc, qb, kc):
    B, S, _ = q.shape
    n_sel = min(TOPK_MAX, S // 4)
    blk = lambda n: pl.BlockSpec((None, qb, n), lambda b, i: (b, i, 0))
    per_b = lambda r, c: pl.BlockSpec((None, r, c), lambda b, i: (b, 0, 0))
    kern = functools.partial(_dsa_kernel, n_sel=n_sel, idx_bits=max(1, (S - 1).bit_length()), kc=kc)
    return pl.pallas_call(
        kern, grid=(B, S // qb), out_shape=jax.ShapeDtypeStruct((B, S, W_DSA), bf16), out_specs=blk(W_DSA),
        in_specs=[blk(W_DSA), per_b(W_DSA, S), per_b(S, W_DSA), blk(W_IDX), per_b(IDX_DIM, S), blk(LANES)],
        scratch_shapes=[pltpu.VMEM((qb, S), i32), pltpu.VMEM((qb, S), f32), pltpu.VMEM((DSA_HEADS, qb, S), f32)]
        + [pltpu.VMEM((DSA_HEADS, qb, LANES), f32)] * 3,
        compiler_params=_params("parallel", "arbitrary"), name="dsa",
    )(q, kt, v, qi, kit, misc)


def _gla_kernel(gq_ref, gk_ref, gv_ref, la_ref, gr_ref, ng_ref, o_ref, state_ref):
    ct = gq_ref.shape[0]
    nch = ct // GLA_CHUNK

    @pl.when(pl.program_id(1) == 0)
    def _():
        state_ref[...] = jnp.zeros_like(state_ref)

    la = la_ref[...]
    r = lax.broadcasted_iota(i32, (ct, ct), 0)
    c = lax.broadcasted_iota(i32, (ct, ct), 1)
    same = (r // GLA_CHUNK) == (c // GLA_CHUNK)
    causal = same & (c <= r)
    hp = lax.Precision.HIGHEST
    bcum = _dot(jnp.where(causal, 1.0, 0.0), la, precision=hp)
    blast = _dot(jnp.where(same, 1.0, 0.0), la, precision=hp)
    q_dec = (gq_ref[...] * (GLA_DK ** -0.5) * jnp.exp(bcum)).astype(bf16)
    k_inv = (gk_ref[...] * jnp.exp(-bcum)).astype(bf16)
    k_end = (gk_ref[...] * jnp.exp(blast - bcum)).astype(bf16)
    decay = jnp.exp(blast)
    ng = ng_ref[...]
    for h in range(GLA_HEADS):
        ks = slice(h * GLA_DK, (h + 1) * GLA_DK)
        vs = slice(h * GLA_DV, (h + 1) * GLA_DV)
        qd, ki, ke, vh = q_dec[:, ks], k_inv[:, ks], k_end[:, ks], gv_ref[:, vs]
        attn = jnp.where(causal, _dot(qd, ki, _NT), 0.0)
        o = _dot(attn.astype(bf16), vh)
        st = state_ref[h]
        inter = []
        for n in range(nch):
            rows = slice(n * GLA_CHUNK, (n + 1) * GLA_CHUNK)
            inter.append(_dot(qd[rows], st.astype(bf16), _NT))
            st = st * decay[n * GLA_CHUNK:n * GLA_CHUNK + 1, ks] + _dot(vh[rows], ke[rows], _TN)
        state_ref[h] = st
        o = o + jnp.concatenate(inter, axis=0)
        o = o * lax.rsqrt(jnp.mean(o * o, axis=-1, keepdims=True) + RMS_EPS) * ng
        g = gr_ref[:, vs]
        o_ref[:, vs] = (o * (g * jax.nn.sigmoid(g))).astype(o_ref.dtype)


def _gla(gq, gk, gv, la, gr, norm_g, ct):
    B, S, _ = gq.shape
    blk = lambda n: pl.BlockSpec((None, ct, n), lambda b, j: (b, j, 0))
    ng = norm_g.reshape(1, GLA_DV)
    return pl.pallas_call(
        _gla_kernel, grid=(B, S // ct), out_shape=jax.ShapeDtypeStruct((B, S, W_GV), bf16), out_specs=blk(W_GV),
        in_specs=[blk(W_GQK), blk(W_GQK), blk(W_GV), blk(W_GQK), blk(W_GV), pl.BlockSpec(ng.shape, lambda b, j: (0, 0))],
        scratch_shapes=[pltpu.VMEM((GLA_HEADS, GLA_DV, GLA_DK), f32)],
        compiler_params=_params("parallel", "arbitrary"), name="gla",
    )(gq, gk, gv, la, gr, ng)


def _mix_out_kernel(x_ref, ya_ref, yb_ref, wa_ref, wb_ref, g_ref, b_ref, o_ref, *, alpha):
    mix = _dot(ya_ref[...], wa_ref[...]) + _dot(yb_ref[...], wb_ref[...])
    o_ref[...] = _layer_norm(alpha * x_ref[...] + mix, g_ref[...], b_ref[...])


def _mix_out(x2, ya, yb, w_out, g, b, alpha, tm):
    T, D = x2.shape
    wa, wb = w_out[:W_DSA].astype(bf16), w_out[W_DSA:].astype(bf16)
    tok = lambda n: pl.BlockSpec((tm, n), lambda i: (i, 0))
    full = lambda a: pl.BlockSpec(a.shape, lambda i: (0, 0))
    g, b = g.reshape(1, D), b.reshape(1, D)
    return pl.pallas_call(
        functools.partial(_mix_out_kernel, alpha=alpha), grid=(T // tm,),
        out_shape=jax.ShapeDtypeStruct((T, D), f32), out_specs=tok(D),
        in_specs=[tok(D), tok(W_DSA), tok(W_GV), full(wa), full(wb), full(g), full(b)],
        compiler_params=_params("parallel"), name="mix_out",
    )(x2, ya, yb, wa, wb, g, b)


def _mem_kv_kernel(m_ref, wk_ref, wv_ref, k_ref, v_ref):
    m = m_ref[...].astype(bf16)
    k_ref[...] = _dot(m, wk_ref[...]).astype(bf16)
    v_ref[...] = _dot(m, wv_ref[...]).astype(bf16)


def _mem_kv(mem, w_k, w_v):
    B, M, D = mem.shape
    wk, wv = w_k.astype(bf16), w_v.astype(bf16)
    blk = pl.BlockSpec((None, M, D), lambda b: (b, 0, 0))
    full = pl.BlockSpec((D, D), lambda b: (0, 0))
    sd = jax.ShapeDtypeStruct((B, M, D), bf16)
    return pl.pallas_call(_mem_kv_kernel, grid=(B,), out_shape=[sd, sd], out_specs=[blk, blk],
                          in_specs=[blk, full, full], compiler_params=_params("parallel"), name="mem_kv")(mem, wk, wv)


def _xattn_kernel(h_ref, k_ref, v_ref, wq_ref, wo_ref, g_ref, b_ref, o_ref, *, alpha):
    h = h_ref[...]
    D = h.shape[1]
    hd = D // XATTN_HEADS
    q = (_dot(h.astype(bf16), wq_ref[...]) * (hd ** -0.5)).astype(bf16)
    outs = []
    for a in range(XATTN_HEADS):
        s = slice(a * hd, (a + 1) * hd)
        lg = _dot(q[:, s], k_ref[:, s], _NT)
        p = jnp.exp(lg - jnp.max(lg, axis=1, keepdims=True))
        l = jnp.sum(p, axis=1, keepdims=True)
        outs.append((_dot(p.astype(bf16), v_ref[:, s]) / l).astype(bf16))
    ca = _dot(jnp.concatenate(outs, axis=1), wo_ref[...])
    o_ref[...] = _layer_norm(alpha * h + ca, g_ref[...], b_ref[...])


def _xattn(h1, km, vm, w_q, w_o, g, b, alpha, tm):
    B, S, D = h1.shape
    M = km.shape[1]
    wq, wo = w_q.astype(bf16), w_o.astype(bf16)
    g, b = g.reshape(1, D), b.reshape(1, D)
    tok = pl.BlockSpec((None, tm, D), lambda bi, j: (bi, j, 0))
    per_b = pl.BlockSpec((None, M, D), lambda bi, j: (bi, 0, 0))
    full = lambda a: pl.BlockSpec(a.shape, lambda bi, j: (0, 0))
    return pl.pallas_call(
        functools.partial(_xattn_kernel, alpha=alpha), grid=(B, S // tm),
        out_shape=jax.ShapeDtypeStruct((B, S, D), f32), out_specs=tok,
        in_specs=[tok, per_b, per_b, full(wq), full(wo), full(g), full(b)],
        compiler_params=_params("parallel", "parallel"), name="xattn",
    )(h1, km, vm, wq, wo, g, b)


def _top_rows(s, n_top, ids=None, payload=None):
    if ids is None:
        ids = lax.broadcasted_iota(i32, s.shape, 0)
    vals, picks = [], []
    for _ in range(n_top):
        m = jnp.max(s, axis=0, keepdims=True)
        am = jnp.min(jnp.where(s == m, ids, jnp.int32(2 ** 30)), axis=0, keepdims=True)
        hit = ids == am
        vals.append(m)
        picks.append(am if payload is None else jnp.max(jnp.where(hit, payload, -1), axis=0, keepdims=True))
        s = jnp.where(hit, NEG_INF, s)
    return jnp.concatenate(vals, axis=0), jnp.concatenate(picks, axis=0)


def _pair_candidates(v1, i1, v2, i2):
    n = PEER_TOPK
    sub = lax.broadcasted_iota(i32, (SUBLANES,) + v1.shape[1:], 0)
    blocks = [(v1[0:1] + v2, lax.broadcasted_iota(i32, v2.shape, 0), i1[0:1] * PEER_N_KEYS + i2)]
    for a in range(1, 5):
        blocks.append((v1[a:a + 1] + v2[:SUBLANES], a * n + sub, i1[a:a + 1] * PEER_N_KEYS + i2[:SUBLANES]))
    pick = lambda x: jnp.where(sub < 2, x[5:6], jnp.where(sub < 4, x[6:7], x[7:8]))
    alt = lambda x: jnp.where((sub & 1) == 0, x[0:1], x[1:2])
    a_of = jnp.where(sub < 2, 5, jnp.where(sub < 4, 6, 7))
    blocks.append((jnp.where(sub < 6, pick(v1) + alt(v2), NEG_INF), a_of * n + (sub & 1),
                   pick(i1) * PEER_N_KEYS + alt(i2)))
    blocks.append((v1[SUBLANES:] + v2[0:1], (sub + SUBLANES) * n, i1[SUBLANES:] * PEER_N_KEYS + i2[0:1]))
    return tuple(jnp.concatenate(parts, axis=0) for parts in zip(*blocks))


def _route_kernel(h_ref, wq_ref, k1_ref, k2_ref, row_ref, ne_ref, g_ref):
    q = _dot(h_ref[...].astype(bf16), wq_ref[...])
    dk = PEER_D_KEY // 2
    k1, k2 = k1_ref[...], k2_ref[...]
    for a in range(PEER_HEADS):
        qa = q[:, a * PEER_D_KEY:a * PEER_D_KEY + dk].astype(bf16)
        qb = q[:, a * PEER_D_KEY + dk:(a + 1) * PEER_D_KEY].astype(bf16)
        v1, i1 = _top_rows(_dot(k1, qa, _NT), PEER_TOPK)
        v2, i2 = _top_rows(_dot(k2, qb, _NT), PEER_TOPK)
        cand, ids, cidx = _pair_candidates(v1, i1, v2, i2)
        top, experts = _top_rows(cand, PEER_TOPK, ids=ids, payload=cidx)
        p = jnp.exp(top - top[0:1])
        gates = p / jnp.sum(p, axis=0, keepdims=True)
        for half in range(PEER_TOPK // SUBLANES):
            grp = slice(half * SUBLANES, (half + 1) * SUBLANES)
            e, g = experts[grp], gates[grp]
            odd = e & 1
            n_even = SUBLANES - jnp.sum(odd, axis=0, keepdims=True)
            sub = lax.broadcasted_iota(i32, e.shape, 0)
            evens_before = jnp.zeros_like(n_even)
            e_sorted, g_sorted = jnp.zeros_like(e), jnp.zeros_like(g)
            for r in range(SUBLANES):
                odd_r = odd[r:r + 1]
                dest = jnp.where(odd_r == 1, n_even + (r - evens_before), evens_before)
                hit = sub == dest
                e_sorted = jnp.where(hit, e[r:r + 1], e_sorted)
                g_sorted = jnp.where(hit, g[r:r + 1], g_sorted)
                evens_before = evens_before + (1 - odd_r)
            rows = slice(a * PEER_TOPK + half * SUBLANES, a * PEER_TOPK + (half + 1) * SUBLANES)
            row_ref[rows, :] = (e_sorted >> 1) * SUBLANES
            g_ref[rows, :] = g_sorted
            n = a * (PEER_TOPK // SUBLANES) + half
            ne_ref[n:n + 1, :] = n_even


def _route(h2, w_query, k1, k2, tm):
    T, D = h2.shape
    wq = w_query.astype(bf16)
    k1, k2 = k1.astype(bf16), k2.astype(bf16)
    hk = PEER_HEADS * PEER_TOPK
    full = lambda a: pl.BlockSpec(a.shape, lambda i: (0, 0))
    out = pl.BlockSpec((hk, tm), lambda i: (0, i))
    n_groups = hk // SUBLANES
    sd = jax.ShapeDtypeStruct
    return pl.pallas_call(
        _route_kernel, grid=(T // tm,),
        out_shape=[sd((hk, T), i32), sd((n_groups, T), i32), sd((hk, T), f32)],
        out_specs=[out, pl.BlockSpec((n_groups, tm), lambda i: (0, i)), out],
        in_specs=[pl.BlockSpec((tm, D), lambda i: (i, 0)), full(wq), full(k1), full(k2)],
        compiler_params=_params("parallel"), name="route",
    )(h2, wq, k1, k2)


def _pack_table(tab):
    n, d = tab.shape
    assert d == SUBLANES * LANES
    u = lax.bitcast_convert_type(tab.astype(bf16), jnp.uint16).astype(jnp.uint32).reshape(n // 2, 2, SUBLANES, LANES)
    return lax.bitcast_convert_type((u[:, 0] << 16) | u[:, 1], i32).reshape(n // 2 * SUBLANES, LANES)


def _shift_patterns():
    n_even = np.arange(SUBLANES + 1)[:, None, None]
    p = np.arange(SUBLANES)[None, :, None]
    pat = np.where(p < n_even, 0, 16) + np.zeros((1, 1, LANES), np.int64)
    return jnp.asarray(pat.reshape(-1, LANES), i32)


def _unpack(w, sh_ref, n_even, p):
    shift = jnp.broadcast_to(sh_ref[pl.ds(n_even * SUBLANES + p, 1), :], w.shape)
    return pltpu.bitcast(jnp.left_shift(w, shift) & jnp.int32(-65536), f32)


def _row(tab_ref, row0, sh_ref, n_even, p):
    return _unpack(tab_ref[pl.ds(pl.multiple_of(row0, SUBLANES), SUBLANES), :], sh_ref, n_even, p)


TOKEN_BATCH = 8
GROUP_UNROLL = 4


_TREE_ORDER = (0, 4, 2, 6, 1, 5, 3, 7)


def _merge(p, q, mask, shift):
    return jnp.where(mask, p, q) + pltpu.roll(jnp.where(mask, q, p), shift, 0)


def _sublane_sums(ps):
    sub = lax.broadcasted_iota(i32, ps[0].shape, 0)
    quad = lambda a, b, o: _merge(a, b, ((sub - o) & 7) < 4, 4)
    duo = lambda a, b, o: _merge(a, b, ((sub - o) & 3) < 2, 6)
    r1 = duo(quad(ps[0], ps[1], 0), quad(ps[2], ps[3], 2), 0)
    r2 = duo(quad(ps[4], ps[5], 1), quad(ps[6], ps[7], 3), 1)
    return _merge(r1, r2, (sub & 1) == 0, 7)


def _peer_down_kernel(*refs):
    row_refs, (ne_ref, x_ref, gt_ref, sh_ref, tab_ref, ct_ref, part_ref, actt_ref) = refs[:SUBLANES], refs[SUBLANES:]
    hk, tb = ct_ref.shape
    n_groups = hk // SUBLANES
    lane = lax.broadcasted_iota(i32, (hk, tb), 1)

    def batch(b, carry):
        t0 = b * TOKEN_BATCH

        def token(u, c1):
            t = t0 + u
            xt = x_ref[t]

            def group(g, c2):
                k0 = pl.multiple_of(g * SUBLANES, SUBLANES)
                tg = t * n_groups + g
                n_even = ne_ref[tg]
                ps = [_row(tab_ref, row_refs[k][tg], sh_ref, n_even, k) * xt for k in _TREE_ORDER]
                part_ref[u, pl.ds(k0, SUBLANES), :] = _sublane_sums(ps)
                return c2

            lax.fori_loop(0, n_groups, group, 0, unroll=4 * GROUP_UNROLL)
            return c1

        lax.fori_loop(0, TOKEN_BATCH, token, 0)
        a = actt_ref[...]
        for u in range(TOKEN_BATCH):
            a = jnp.where(lane == t0 + u, jnp.sum(part_ref[u], axis=1, keepdims=True), a)
        actt_ref[...] = a
        return carry

    lax.fori_loop(0, tb // TOKEN_BATCH, batch, 0)
    a = actt_ref[...]
    gelu = 0.5 * a * (1.0 + lax.erf(a * (2.0 ** -0.5)))
    ct_ref[...] = gt_ref[...] * gelu


def _peer_specs(tb, hk):
    per_group = pl.BlockSpec((tb * hk // SUBLANES,), lambda i: (i,), memory_space=pltpu.SMEM)
    whole = pl.BlockSpec(memory_space=pltpu.VMEM)
    return per_group, whole


def _peer_down(rows8, n_even, x3, gates_t, tab, tb):
    hk, T = gates_t.shape
    per_group, whole = _peer_specs(tb, hk)
    per_k = pl.BlockSpec((hk, tb), lambda i: (0, i))
    return pl.pallas_call(
        _peer_down_kernel, grid=(T // tb,), out_shape=jax.ShapeDtypeStruct((hk, T), f32), out_specs=per_k,
        in_specs=[per_group] * (SUBLANES + 1) + [pl.BlockSpec((tb,) + x3.shape[1:], lambda i: (i, 0, 0)), per_k,
                                                 whole, whole],
        scratch_shapes=[pltpu.VMEM((TOKEN_BATCH, hk, LANES), f32), pltpu.VMEM((hk, tb), f32)],
        compiler_params=_params("arbitrary"), name="peer_down",
    )(*rows8, n_even, x3, gates_t, _shift_patterns(), tab)


def _peer_up_kernel(*refs):
    row_refs, (ne_ref, ct_ref, sh_ref, tab_ref, o_ref, cx_ref) = refs[:SUBLANES], refs[SUBLANES:]
    hk, tb = ct_ref.shape
    n_groups = hk // SUBLANES
    n_acc = 4
    vreg = o_ref.shape[1:]
    lane = lax.broadcasted_iota(i32, (hk, tb), 1)

    def batch(b, carry):
        t0 = b * TOKEN_BATCH
        ct = ct_ref[...]
        for u in range(TOKEN_BATCH):
            col = jnp.sum(jnp.where(lane == t0 + u, ct, 0.0), axis=1, keepdims=True)
            cx_ref[u] = jnp.broadcast_to(col, (hk, LANES))

        def token(u, c1):
            t = t0 + u

            def fetch(g):
                tg = t * n_groups + g
                return tuple(tab_ref[pl.ds(pl.multiple_of(r[tg], SUBLANES), SUBLANES), :] for r in row_refs)

            def group(g, carry2):
                accs, packed = list(carry2[0]), carry2[1]
                nxt = fetch((g + 1) & (n_groups - 1))
                k0 = pl.multiple_of(g * SUBLANES, SUBLANES)
                n_even = ne_ref[t * n_groups + g]
                for p in range(SUBLANES):
                    coef = jnp.broadcast_to(cx_ref[u, pl.ds(k0 + p, 1), :], vreg)
                    accs[p % n_acc] = accs[p % n_acc] + _unpack(packed[p], sh_ref, n_even, p) * coef
                return tuple(accs), nxt

            accs, _ = lax.fori_loop(0, n_groups, group, ((jnp.zeros(vreg, f32),) * n_acc, fetch(0)),
                                    unroll=GROUP_UNROLL)
            o_ref[t] = (accs[0] + accs[1]) + (accs[2] + accs[3])
            return c1

        lax.fori_loop(0, TOKEN_BATCH, token, 0)
        return carry

    lax.fori_loop(0, tb // TOKEN_BATCH, batch, 0)


def _peer_up(rows8, n_even, coef_t, tab, tb):
    hk, T = coef_t.shape
    per_group, whole = _peer_specs(tb, hk)
    return pl.pallas_call(
        _peer_up_kernel, grid=(T // tb,), out_shape=jax.ShapeDtypeStruct((T, SUBLANES, LANES), f32),
        out_specs=pl.BlockSpec((tb, SUBLANES, LANES), lambda i: (i, 0, 0)),
        in_specs=[per_group] * (SUBLANES + 1) + [pl.BlockSpec((hk, tb), lambda i: (0, i)), whole, whole],
        scratch_shapes=[pltpu.VMEM((TOKEN_BATCH, hk, LANES), f32)],
        compiler_params=_params("arbitrary"), name="peer_up",
    )(*rows8, n_even, coef_t, _shift_patterns(), tab)


def _ffn_out_kernel(h_ref, f_ref, g_ref, b_ref, o_ref, *, alpha):
    o_ref[...] = _layer_norm(alpha * h_ref[...] + f_ref[...], g_ref[...], b_ref[...])


def _ffn_out(h2, ff, g, b, alpha, tm):
    T, D = h2.shape
    g, b = g.reshape(1, D), b.reshape(1, D)
    tok = pl.BlockSpec((tm, D), lambda i: (i, 0))
    full = pl.BlockSpec((1, D), lambda i: (0, 0))
    return pl.pallas_call(
        functools.partial(_ffn_out_kernel, alpha=alpha), grid=(T // tm,),
        out_shape=jax.ShapeDtypeStruct((T, D), f32), out_specs=tok, in_specs=[tok, tok, full, full],
        compiler_params=_params("parallel"), name="ffn_out",
    )(h2, ff, g, b)


def _tile(n, want):
    t = min(n, want)
    assert n % t == 0, (n, t)
    return t


def kernel(x, positions, mem, w_in, gla_gate_up, gla_gate_bias, gla_norm_g, w_out, ln_mix_g, ln_mix_b, xattn_w_q, xattn_w_k, xattn_w_v, xattn_w_o, ln_mem_g, ln_mem_b, peer_w_query, peer_sub_keys_1, peer_sub_keys_2, peer_expert_down, peer_expert_up, ln_ffn_g, ln_ffn_b):
    B, S, D = x.shape
    T = B * S
    depth = w_in.shape[0]
    alpha = (2.0 * depth) ** 0.25
    tm = _tile(S, 512)
    h = x
    for l in range(depth):
        q, kt, v, qi, kit, misc, gq, gk, gv, la, gr = _proj(h, positions, w_in[l], gla_gate_up[l], gla_gate_bias[l], tm)
        y_dsa = _dsa(q, kt, v, qi, kit, misc, _tile(S, 128), _tile(S, 512))
        y_gla = _gla(gq, gk, gv, la, gr, gla_norm_g[l], tm)
        h1 = _mix_out(h.reshape(T, D), y_dsa.reshape(T, W_DSA), y_gla.reshape(T, W_GV), w_out[l],
                      ln_mix_g[l], ln_mix_b[l], alpha, tm)
        km, vm = _mem_kv(mem, xattn_w_k[l], xattn_w_v[l])
        h2 = _xattn(h1.reshape(B, S, D), km, vm, xattn_w_q[l], xattn_w_o[l], ln_mem_g[l], ln_mem_b[l], alpha, tm)
        h2 = h2.reshape(T, D)
        rows_t, ne_t, gates_t = _route(h2, peer_w_query[l], peer_sub_keys_1[l], peer_sub_keys_2[l], _tile(T, 256))
        flat = lambda a: a.T.reshape(-1)
        n_even = flat(ne_t)
        rows8 = [flat(rows_t[p::SUBLANES]) for p in range(SUBLANES)]
        tb = _tile(T, LANES)
        coef_t = _peer_down(rows8, n_even, h2.reshape(T, SUBLANES, LANES), gates_t, _pack_table(peer_expert_down[l]), tb)
        ff = _peer_up(rows8, n_even, coef_t, _pack_table(peer_expert_up[l]), tb)
        h = _ffn_out(h2, ff.reshape(T, D), ln_ffn_g[l], ln_ffn_b[l], alpha, tm).reshape(B, S, D)
    return h
```

```python
import functools

import jax
import jax.numpy as jnp
import numpy as np
from jax import lax
from jax.experimental import pallas as pl
from jax.experimental.pallas import tpu as pltpu

f32 = jnp.float32
bf16 = jnp.bfloat16
i32 = jnp.int32

DSA_HEADS = 8
DSA_HEAD_DIM = 64
IDX_HEADS = 8
IDX_DIM = 32
TOPK_MAX = 256
GLA_HEADS = 4
GLA_DK = 64
GLA_DV = 128
GLA_GATE_RANK = 16
GLA_GATE_TEMP = 16.0
GLA_CHUNK = 64
ROPE_THETA = 500000.0
ROPE_FRACTION = 4
XATTN_HEADS = 4
PEER_N_KEYS = 128
PEER_HEADS = 8
PEER_D_KEY = 256
PEER_TOPK = 16
LN_EPS = 1e-5
RMS_EPS = 1e-6

LANES = 128
SUBLANES = 8
VMEM_LIMIT = 56 * 1024 * 1024

INT_MIN = -(2 ** 31)
NEG_INF = float("-inf")

W_DSA = DSA_HEADS * DSA_HEAD_DIM
W_IDX = IDX_HEADS * IDX_DIM
W_GQK = GLA_HEADS * GLA_DK
W_GV = GLA_HEADS * GLA_DV
MISC_KI = 0
MISC_WI = IDX_DIM
MISC_LR = IDX_DIM + IDX_HEADS


def _dot(a, b, dims=(((1,), (0,)), ((), ())), precision=None):
    return lax.dot_general(a, b, dims, precision=precision, preferred_element_type=f32)


_NN = (((1,), (0,)), ((), ()))
_NT = (((1,), (1,)), ((), ()))
_TN = (((0,), (0,)), ((), ()))


def _params(*sem):
    return pltpu.CompilerParams(dimension_semantics=sem, vmem_limit_bytes=VMEM_LIMIT)


def _layer_norm(y, g, b):
    mu = jnp.mean(y, axis=-1, keepdims=True)
    yc = y - mu
    var = jnp.mean(yc * yc, axis=-1, keepdims=True)
    return yc * lax.rsqrt(var + LN_EPS) * g + b


def _rot(xb, c, sa, sb, half):
    return xb * c + pltpu.roll(xb, LANES - half, 1) * sa + pltpu.roll(xb, half, 1) * sb


def _proj_kernel(x_ref, w_ref, gup_ref, gb_ref, cq_ref, saq_ref, sbq_ref, ci_ref, sai_ref, sbi_ref,
                 q_ref, kt_ref, v_ref, qi_ref, kit_ref, misc_ref, gq_ref, gk_ref, gv_ref, la_ref, gr_ref):
    x = x_ref[...].astype(bf16)
    tm = x.shape[0]
    cq, saq, sbq = cq_ref[...], saq_ref[...], sbq_ref[...]
    ci, sai, sbi = ci_ref[...], sai_ref[...], sbi_ref[...]
    hq = DSA_HEAD_DIM // ROPE_FRACTION // 2
    hi = IDX_DIM // ROPE_FRACTION // 2
    o = 0
    scale = DSA_HEAD_DIM ** -0.5
    for j in range(W_DSA // LANES):
        a = _dot(x, w_ref[:, o + j * LANES:o + (j + 1) * LANES])
        q_ref[:, j * LANES:(j + 1) * LANES] = (_rot(a, cq, saq, sbq, hq) * scale).astype(bf16)
    o += W_DSA
    for j in range(W_DSA // LANES):
        a = _dot(x, w_ref[:, o + j * LANES:o + (j + 1) * LANES])
        kt_ref[j * LANES:(j + 1) * LANES, :] = _rot(a, cq, saq, sbq, hq).T.astype(bf16)
    o += W_DSA
    v_ref[...] = _dot(x, w_ref[:, o:o + W_DSA]).astype(bf16)
    o += W_DSA
    for j in range(W_IDX // LANES):
        a = _dot(x, w_ref[:, o + j * LANES:o + (j + 1) * LANES])
        qi_ref[:, j * LANES:(j + 1) * LANES] = _rot(a, ci, sai, sbi, hi).astype(bf16)
    o += W_IDX
    m = _dot(x, w_ref[:, o:o + LANES])
    lane = lax.broadcasted_iota(i32, (tm, LANES), 1)
    is_ki = lane < IDX_DIM
    m = _rot(m, jnp.where(is_ki, ci, 1.0), jnp.where(is_ki, sai, 0.0), jnp.where(is_ki, sbi, 0.0), hi)
    misc_ref[...] = m
    kit_ref[...] = m.T[:IDX_DIM, :].astype(bf16)
    z = _dot(m.astype(bf16), gup_ref[...]) + gb_ref[...]
    la_ref[...] = (jnp.minimum(z, 0.0) - jnp.log1p(jnp.exp(-jnp.abs(z)))) / GLA_GATE_TEMP
    o += LANES
    gq_ref[...] = _dot(x, w_ref[:, o:o + W_GQK])
    o += W_GQK
    gk_ref[...] = _dot(x, w_ref[:, o:o + W_GQK])
    o += W_GQK
    gv_ref[...] = _dot(x, w_ref[:, o:o + W_GV]).astype(bf16)
    o += W_GV
    gr_ref[...] = _dot(x, w_ref[:, o:o + W_GV])


def _rot_tables(positions, head_dim):
    r = head_dim // ROPE_FRACTION
    half = r // 2
    inv_freq = ROPE_THETA ** (-jnp.arange(half, dtype=f32) / half)
    ang = positions.astype(f32)[..., None] * inv_freq
    cos, sin = jnp.cos(ang), jnp.sin(ang)
    lane = np.arange(LANES) % head_dim
    src = np.where(lane < half, lane, np.clip(lane - half, 0, half - 1))
    cos_l, sin_l = cos[..., src], sin[..., src]
    c = jnp.where(lane < r, cos_l, 1.0)
    sa = jnp.where(lane < half, -sin_l, 0.0)
    sb = jnp.where((lane >= half) & (lane < r), sin_l, 0.0)
    return c, sa, sb


def _proj(x, positions, w_in, gate_up, gate_bias, tm):
    B, S, D = x.shape
    splits = np.cumsum([W_DSA, W_DSA, W_DSA, W_IDX, IDX_DIM, IDX_HEADS, W_GQK, W_GQK, W_GV, GLA_GATE_RANK])
    (wq, wk, wv, wqi, wki, wwi, wgq, wgk, wgv, wlr, wgr) = jnp.split(w_in, splits.tolist(), axis=1)
    pad = jnp.zeros((D, LANES - IDX_DIM - IDX_HEADS - GLA_GATE_RANK), w_in.dtype)
    w_a = jnp.concatenate([wq, wk, wv, wqi, wki, wwi, wlr, pad, wgq, wgk, wgv, wgr], axis=1).astype(bf16)
    gup = jnp.zeros((LANES, W_GQK), f32).at[MISC_LR:MISC_LR + GLA_GATE_RANK].set(gate_up).astype(bf16)
    tabs = _rot_tables(positions, DSA_HEAD_DIM) + _rot_tables(positions, IDX_DIM)
    W = w_a.shape[1]
    tok = lambda n: pl.BlockSpec((None, tm, n), lambda b, j: (b, j, 0))
    full = lambda a: pl.BlockSpec(a.shape, lambda b, j: (0,) * a.ndim)
    tr = lambda n: pl.BlockSpec((None, n, tm), lambda b, j: (b, 0, j))
    sd = jax.ShapeDtypeStruct
    out_shape = [sd((B, S, W_DSA), bf16), sd((B, W_DSA, S), bf16), sd((B, S, W_DSA), bf16), sd((B, S, W_IDX), bf16),
                 sd((B, IDX_DIM, S), bf16), sd((B, S, LANES), f32), sd((B, S, W_GQK), f32), sd((B, S, W_GQK), f32),
                 sd((B, S, W_GV), bf16), sd((B, S, W_GQK), f32), sd((B, S, W_GV), f32)]
    out_specs = [tok(W_DSA), tr(W_DSA), tok(W_DSA), tok(W_IDX), tr(IDX_DIM), tok(LANES), tok(W_GQK), tok(W_GQK),
                 tok(W_GV), tok(W_GQK), tok(W_GV)]
    gb = gate_bias.reshape(1, W_GQK)
    return pl.pallas_call(
        _proj_kernel, grid=(B, S // tm), out_shape=out_shape, out_specs=out_specs,
        in_specs=[tok(D), full(w_a), full(gup), full(gb)] + [tok(LANES)] * 6,
        compiler_params=_params("parallel", "parallel"), name="proj",
    )(x, w_a, gup, gb, *tabs)


def _dsa_kernel(q_ref, kt_ref, v_ref, qi_ref, kit_ref, misc_ref, o_ref, key_ref, bias_ref, lg_ref, mx_ref, ls_ref,
                acc_ref, *, n_sel, idx_bits, kc):
    qb, S = q_ref.shape[0], kt_ref.shape[1]
    i = pl.program_id(1)
    nk = lax.div((i + 1) * qb + (kc - 1), kc)
    qi = qi_ref[...]
    wi = misc_ref[:, MISC_WI:MISC_WI + IDX_HEADS] * (IDX_HEADS ** -0.5)
    qpos = lax.broadcasted_iota(i32, (qb, 1), 0) + i * qb
    target = jnp.minimum(n_sel, qpos + 1).astype(f32)
    lane = lax.broadcasted_iota(i32, (qb, kc), 1)

    def chunk(c):
        return pl.ds(pl.multiple_of(c * kc, kc), kc)

    def score_chunk(c, carry):
        kit = kit_ref[:, chunk(c)]
        score = jnp.zeros((qb, kc), f32)
        for h in range(IDX_HEADS):
            d = _dot(qi[:, h * IDX_DIM:(h + 1) * IDX_DIM], kit) * (IDX_DIM ** -0.5)
            score = score + wi[:, h:h + 1] * jnp.maximum(d, 0.0)
        score = jnp.where(score == 0.0, 0.0, score)
        bits = pltpu.bitcast(score, i32)
        key = bits ^ ((bits >> 31) & jnp.int32(0x7FFFFFFF))
        key_ref[:, chunk(c)] = jnp.where(lane + c * kc <= qpos, key, INT_MIN)
        return carry

    lax.fori_loop(0, nk, score_chunk, 0)

    def count(pred):
        def body(c, acc):
            hit = jnp.where(pred(key_ref[:, chunk(c)], lane + c * kc), 1.0, 0.0)
            for j in range(kc // LANES):
                acc = acc + hit[:, j * LANES:(j + 1) * LANES]
            return acc

        acc = lax.fori_loop(0, nk, body, jnp.zeros((qb, LANES), f32))
        return jnp.sum(acc, axis=1, keepdims=True)

    base = jnp.where(count(lambda k, col: k >= 0) >= target, jnp.int32(0), jnp.int32(INT_MIN))

    def tau_bit(t, base):
        cand = base | jnp.left_shift(jnp.int32(1), 30 - t)
        return jnp.where(count(lambda k, col: k >= cand) >= target, cand, base)

    tau = lax.fori_loop(0, 31, tau_bit, base)
    excess = jnp.max(count(lambda k, col: k >= tau) - target)

    def tie_limit():
        need = target - count(lambda k, col: k > tau)

        def idx_bit(t, m):
            cand = m | jnp.left_shift(jnp.int32(1), idx_bits - 1 - t)
            return jnp.where(count(lambda k, col: (k == tau) & (col < cand)) < need, cand, m)

        return lax.fori_loop(0, idx_bits, idx_bit, jnp.zeros((qb, 1), i32))

    m_idx = lax.cond(excess > 0.0, tie_limit, lambda: jnp.full((qb, 1), S, i32))

    def bias_chunk(c, carry):
        k = key_ref[:, chunk(c)]
        sel = (k > tau) | ((k == tau) & (lane + c * kc <= m_idx))
        bias_ref[:, chunk(c)] = jnp.where(sel, 0.0, NEG_INF)
        return carry

    lax.fori_loop(0, nk, bias_chunk, 0)

    q = q_ref[...]
    pair = LANES // DSA_HEAD_DIM

    def fold(x, op, acc):
        for j in range(kc // LANES):
            acc = op(acc, x[:, j * LANES:(j + 1) * LANES])
        return acc

    heads = range(DSA_HEADS)
    head_rows = [slice(h * DSA_HEAD_DIM, (h + 1) * DSA_HEAD_DIM) for h in heads]
    mx_ref[...] = jnp.full(mx_ref.shape, NEG_INF, f32)
    ls_ref[...] = jnp.zeros_like(ls_ref)
    acc_ref[...] = jnp.zeros_like(acc_ref)

    def logits(c, carry):
        bias = bias_ref[:, chunk(c)]
        for h in heads:
            lg = _dot(q[:, head_rows[h]], kt_ref[head_rows[h], chunk(c)]) + bias
            lg_ref[h, :, chunk(c)] = lg
            mx_ref[h] = fold(lg, jnp.maximum, mx_ref[h])
        return carry

    lax.fori_loop(0, nk, logits, 0)
    ms = [jnp.max(mx_ref[h], axis=1, keepdims=True) for h in heads]

    def weigh(c, carry):
        for h in heads:
            p = jnp.exp(lg_ref[h, :, chunk(c)] - ms[h])
            ls_ref[h] = fold(p, jnp.add, ls_ref[h])
            slab = slice((h // pair) * LANES, (h // pair + 1) * LANES)
            acc_ref[h] = acc_ref[h] + _dot(p.astype(bf16), v_ref[chunk(c), slab])
        return carry

    lax.fori_loop(0, nk, weigh, 0)
    for h in heads:
        l = jnp.sum(ls_ref[h], axis=1, keepdims=True)
        off = (h % pair) * DSA_HEAD_DIM
        o_ref[:, head_rows[h]] = (acc_ref[h][:, off:off + DSA_HEAD_DIM] / l).astype(o_ref.dtype)


def _dsa(q, kt, v, qi, kit, misc, qb, kc):
    B, S, _ = q.shape
    n_sel = min(TOPK_MAX, S // 4)
    blk = lambda n: pl.BlockSpec((None, qb, n), lambda b, i: (b, i, 0))
    per_b = lambda r, c: pl.BlockSpec((None, r, c), lambda b, i: (b, 0, 0))
    kern = functools.partial(_dsa_kernel, n_sel=n_sel, idx_bits=max(1, (S - 1).bit_length()), kc=kc)
    return pl.pallas_call(
        kern, grid=(B, S // qb), out_shape=jax.ShapeDtypeStruct((B, S, W_DSA), bf16), out_specs=blk(W_DSA),
        in_specs=[blk(W_DSA), per_b(W_DSA, S), per_b(S, W_DSA), blk(W_IDX), per_b(IDX_DIM, S), blk(LANES)],
        scratch_shapes=[pltpu.VMEM((qb, S), i32), pltpu.VMEM((qb, S), f32), pltpu.VMEM((DSA_HEADS, qb, S), f32)]
        + [pltpu.VMEM((DSA_HEADS, qb, LANES), f32)] * 3,
        compiler_params=_params("parallel", "arbitrary"), name="dsa",
    )(q, kt, v, qi, kit, misc)


def _gla_kernel(gq_ref, gk_ref, gv_ref, la_ref, gr_ref, ng_ref, o_ref, state_ref):
    ct = gq_ref.shape[0]
    nch = ct // GLA_CHUNK

    @pl.when(pl.program_id(1) == 0)
    def _():
        state_ref[...] = jnp.zeros_like(state_ref)

    la = la_ref[...]
    r = lax.broadcasted_iota(i32, (ct, ct), 0)
    c = lax.broadcasted_iota(i32, (ct, ct), 1)
    same = (r // GLA_CHUNK) == (c // GLA_CHUNK)
    causal = same & (c <= r)
    hp = lax.Precision.HIGHEST
    bcum = _dot(jnp.where(causal, 1.0, 0.0), la, precision=hp)
    blast = _dot(jnp.where(same, 1.0, 0.0), la, precision=hp)
    q_dec = (gq_ref[...] * (GLA_DK ** -0.5) * jnp.exp(bcum)).astype(bf16)
    k_inv = (gk_ref[...] * jnp.exp(-bcum)).astype(bf16)
    k_end = (gk_ref[...] * jnp.exp(blast - bcum)).astype(bf16)
    decay = jnp.exp(blast)
    ng = ng_ref[...]
    for h in range(GLA_HEADS):
        ks = slice(h * GLA_DK, (h + 1) * GLA_DK)
        vs = slice(h * GLA_DV, (h + 1) * GLA_DV)
        qd, ki, ke, vh = q_dec[:, ks], k_inv[:, ks], k_end[:, ks], gv_ref[:, vs]
        attn = jnp.where(causal, _dot(qd, ki, _NT), 0.0)
        o = _dot(attn.astype(bf16), vh)
        st = state_ref[h]
        inter = []
        for n in range(nch):
            rows = slice(n * GLA_CHUNK, (n + 1) * GLA_CHUNK)
            inter.append(_dot(qd[rows], st.astype(bf16), _NT))
            st = st * decay[n * GLA_CHUNK:n * GLA_CHUNK + 1, ks] + _dot(vh[rows], ke[rows], _TN)
        state_ref[h] = st
        o = o + jnp.concatenate(inter, axis=0)
        o = o * lax.rsqrt(jnp.mean(o * o, axis=-1, keepdims=True) + RMS_EPS) * ng
        g = gr_ref[:, vs]
        o_ref[:, vs] = (o * (g * jax.nn.sigmoid(g))).astype(o_ref.dtype)


def _gla(gq, gk, gv, la, gr, norm_g, ct):
    B, S, _ = gq.shape
    blk = lambda n: pl.BlockSpec((None, ct, n), lambda b, j: (b, j, 0))
    ng = norm_g.reshape(1, GLA_DV)
    return pl.pallas_call(
        _gla_kernel, grid=(B, S // ct), out_shape=jax.ShapeDtypeStruct((B, S, W_GV), bf16), out_specs=blk(W_GV),
        in_specs=[blk(W_GQK), blk(W_GQK), blk(W_GV), blk(W_GQK), blk(W_GV), pl.BlockSpec(ng.shape, lambda b, j: (0, 0))],
        scratch_shapes=[pltpu.VMEM((GLA_HEADS, GLA_DV, GLA_DK), f32)],
        compiler_params=_params("parallel", "arbitrary"), name="gla",
    )(gq, gk, gv, la, gr, ng)


def _mix_out_kernel(x_ref, ya_ref, yb_ref, wa_ref, wb_ref, g_ref, b_ref, o_ref, *, alpha):
    mix = _dot(ya_ref[...], wa_ref[...]) + _dot(yb_ref[...], wb_ref[...])
    o_ref[...] = _layer_norm(alpha * x_ref[...] + mix, g_ref[...], b_ref[...])


def _mix_out(x2, ya, yb, w_out, g, b, alpha, tm):
    T, D = x2.shape
    wa, wb = w_out[:W_DSA].astype(bf16), w_out[W_DSA:].astype(bf16)
    tok = lambda n: pl.BlockSpec((tm, n), lambda i: (i, 0))
    full = lambda a: pl.BlockSpec(a.shape, lambda i: (0, 0))
    g, b = g.reshape(1, D), b.reshape(1, D)
    return pl.pallas_call(
        functools.partial(_mix_out_kernel, alpha=alpha), grid=(T // tm,),
        out_shape=jax.ShapeDtypeStruct((T, D), f32), out_specs=tok(D),
        in_specs=[tok(D), tok(W_DSA), tok(W_GV), full(wa), full(wb), full(g), full(b)],
        compiler_params=_params("parallel"), name="mix_out",
    )(x2, ya, yb, wa, wb, g, b)


def _mem_kv_kernel(m_ref, wk_ref, wv_ref, k_ref, v_ref):
    m = m_ref[...].astype(bf16)
    k_ref[...] = _dot(m, wk_ref[...]).astype(bf16)
    v_ref[...] = _dot(m, wv_ref[...]).astype(bf16)


def _mem_kv(mem, w_k, w_v):
    B, M, D = mem.shape
    wk, wv = w_k.astype(bf16), w_v.astype(bf16)
    blk = pl.BlockSpec((None, M, D), lambda b: (b, 0, 0))
    full = pl.BlockSpec((D, D), lambda b: (0, 0))
    sd = jax.ShapeDtypeStruct((B, M, D), bf16)
    return pl.pallas_call(_mem_kv_kernel, grid=(B,), out_shape=[sd, sd], out_specs=[blk, blk],
                          in_specs=[blk, full, full], compiler_params=_params("parallel"), name="mem_kv")(mem, wk, wv)


def _xattn_kernel(h_ref, k_ref, v_ref, wq_ref, wo_ref, g_ref, b_ref, o_ref, *, alpha):
    h = h_ref[...]
    D = h.shape[1]
    hd = D // XATTN_HEADS
    q = (_dot(h.astype(bf16), wq_ref[...]) * (hd ** -0.5)).astype(bf16)
    outs = []
    for a in range(XATTN_HEADS):
        s = slice(a * hd, (a + 1) * hd)
        lg = _dot(q[:, s], k_ref[:, s], _NT)
        p = jnp.exp(lg - jnp.max(lg, axis=1, keepdims=True))
        l = jnp.sum(p, axis=1, keepdims=True)
        outs.append((_dot(p.astype(bf16), v_ref[:, s]) / l).astype(bf16))
    ca = _dot(jnp.concatenate(outs, axis=1), wo_ref[...])
    o_ref[...] = _layer_norm(alpha * h + ca, g_ref[...], b_ref[...])


def _xattn(h1, km, vm, w_q, w_o, g, b, alpha, tm):
    B, S, D = h1.shape
    M = km.shape[1]
    wq, wo = w_q.astype(bf16), w_o.astype(bf16)
    g, b = g.reshape(1, D), b.reshape(1, D)
    tok = pl.BlockSpec((None, tm, D), lambda bi, j: (bi, j, 0))
    per_b = pl.BlockSpec((None, M, D), lambda bi, j: (bi, 0, 0))
    full = lambda a: pl.BlockSpec(a.shape, lambda bi, j: (0, 0))
    return pl.pallas_call(
        functools.partial(_xattn_kernel, alpha=alpha), grid=(B, S // tm),
        out_shape=jax.ShapeDtypeStruct((B, S, D), f32), out_specs=tok,
        in_specs=[tok, per_b, per_b, full(wq), full(wo), full(g), full(b)],
        compiler_params=_params("parallel", "parallel"), name="xattn",
    )(h1, km, vm, wq, wo, g, b)


def _top_rows(s, n_top, ids=None, payload=None):
    if ids is None:
        ids = lax.broadcasted_iota(i32, s.shape, 0)
    vals, picks = [], []
    for _ in range(n_top):
        m = jnp.max(s, axis=0, keepdims=True)
        am = jnp.min(jnp.where(s == m, ids, jnp.int32(2 ** 30)), axis=0, keepdims=True)
        hit = ids == am
        vals.append(m)
        picks.append(am if payload is None else jnp.max(jnp.where(hit, payload, -1), axis=0, keepdims=True))
        s = jnp.where(hit, NEG_INF, s)
    return jnp.concatenate(vals, axis=0), jnp.concatenate(picks, axis=0)


def _pair_candidates(v1, i1, v2, i2):
    n = PEER_TOPK
    sub = lax.broadcasted_iota(i32, (SUBLANES,) + v1.shape[1:], 0)
    blocks = [(v1[0:1] + v2, lax.broadcasted_iota(i32, v2.shape, 0), i1[0:1] * PEER_N_KEYS + i2)]
    for a in range(1, 5):
        blocks.append((v1[a:a + 1] + v2[:SUBLANES], a * n + sub, i1[a:a + 1] * PEER_N_KEYS + i2[:SUBLANES]))
    pick = lambda x: jnp.where(sub < 2, x[5:6], jnp.where(sub < 4, x[6:7], x[7:8]))
    alt = lambda x: jnp.where((sub & 1) == 0, x[0:1], x[1:2])
    a_of = jnp.where(sub < 2, 5, jnp.where(sub < 4, 6, 7))
    blocks.append((jnp.where(sub < 6, pick(v1) + alt(v2), NEG_INF), a_of * n + (sub & 1),
                   pick(i1) * PEER_N_KEYS + alt(i2)))
    blocks.append((v1[SUBLANES:] + v2[0:1], (sub + SUBLANES) * n, i1[SUBLANES:] * PEER_N_KEYS + i2[0:1]))
    return tuple(jnp.concatenate(parts, axis=0) for parts in zip(*blocks))


def _route_kernel(h_ref, wq_ref, k1_ref, k2_ref, row_ref, ne_ref, g_ref):
    q = _dot(h_ref[...].astype(bf16), wq_ref[...])
    dk = PEER_D_KEY // 2
    k1, k2 = k1_ref[...], k2_ref[...]
    for a in range(PEER_HEADS):
        qa = q[:, a * PEER_D_KEY:a * PEER_D_KEY + dk].astype(bf16)
        qb = q[:, a * PEER_D_KEY + dk:(a + 1) * PEER_D_KEY].astype(bf16)
        v1, i1 = _top_rows(_dot(k1, qa, _NT), PEER_TOPK)
        v2, i2 = _top_rows(_dot(k2, qb, _NT), PEER_TOPK)
        cand, ids, cidx = _pair_candidates(v1, i1, v2, i2)
        top, experts = _top_rows(cand, PEER_TOPK, ids=ids, payload=cidx)
        p = jnp.exp(top - top[0:1])
        gates = p / jnp.sum(p, axis=0, keepdims=True)
        for half in range(PEER_TOPK // SUBLANES):
            grp = slice(half * SUBLANES, (half + 1) * SUBLANES)
            e, g = experts[grp], gates[grp]
            odd = e & 1
            n_even = SUBLANES - jnp.sum(odd, axis=0, keepdims=True)
            sub = lax.broadcasted_iota(i32, e.shape, 0)
            evens_before = jnp.zeros_like(n_even)
            e_sorted, g_sorted = jnp.zeros_like(e), jnp.zeros_like(g)
            for r in range(SUBLANES):
                odd_r = odd[r:r + 1]
                dest = jnp.where(odd_r == 1, n_even + (r - evens_before), evens_before)
                hit = sub == dest
                e_sorted = jnp.where(hit, e[r:r + 1], e_sorted)
                g_sorted = jnp.where(hit, g[r:r + 1], g_sorted)
                evens_before = evens_before + (1 - odd_r)
            rows = slice(a * PEER_TOPK + half * SUBLANES, a * PEER_TOPK + (half + 1) * SUBLANES)
            row_ref[rows, :] = (e_sorted >> 1) * SUBLANES
            g_ref[rows, :] = g_sorted
            n = a * (PEER_TOPK // SUBLANES) + half
            ne_ref[n:n + 1, :] = n_even


def _route(h2, w_query, k1, k2, tm):
    T, D = h2.shape
    wq = w_query.astype(bf16)
    k1, k2 = k1.astype(bf16), k2.astype(bf16)
    hk = PEER_HEADS * PEER_TOPK
    full = lambda a: pl.BlockSpec(a.shape, lambda i: (0, 0))
    out = pl.BlockSpec((hk, tm), lambda i: (0, i))
    n_groups = hk // SUBLANES
    sd = jax.ShapeDtypeStruct
    return pl.pallas_call(
        _route_kernel, grid=(T // tm,),
        out_shape=[sd((hk, T), i32), sd((n_groups, T), i32), sd((hk, T), f32)],
        out_specs=[out, pl.BlockSpec((n_groups, tm), lambda i: (0, i)), out],
        in_specs=[pl.BlockSpec((tm, D), lambda i: (i, 0)), full(wq), full(k1), full(k2)],
        compiler_params=_params("parallel"), name="route",
    )(h2, wq, k1, k2)


def _pack_table(tab):
    n, d = tab.shape
    assert d == SUBLANES * LANES
    u = lax.bitcast_convert_type(tab.astype(bf16), jnp.uint16).astype(jnp.uint32).reshape(n // 2, 2, SUBLANES, LANES)
    return lax.bitcast_convert_type((u[:, 0] << 16) | u[:, 1], i32).reshape(n // 2 * SUBLANES, LANES)


def _shift_patterns():
    n_even = np.arange(SUBLANES + 1)[:, None, None]
    p = np.arange(SUBLANES)[None, :, None]
    pat = np.where(p < n_even, 0, 16) + np.zeros((1, 1, LANES), np.int64)
    return jnp.asarray(pat.reshape(-1, LANES), i32)


def _unpack(w, sh_ref, n_even, p):
    shift = jnp.broadcast_to(sh_ref[pl.ds(n_even * SUBLANES + p, 1), :], w.shape)
    return pltpu.bitcast(jnp.left_shift(w, shift) & jnp.int32(-65536), f32)


def _row(tab_ref, row0, sh_ref, n_even, p):
    return _unpack(tab_ref[pl.ds(pl.multiple_of(row0, SUBLANES), SUBLANES), :], sh_ref, n_even, p)


TOKEN_BATCH = 8
GROUP_UNROLL = 4


_TREE_ORDER = (0, 4, 2, 6, 1, 5, 3, 7)


def _packed(w):
    return pltpu.bitcast(w, bf16)


def _merge(p, q, mask, shift):
    moved = pltpu.roll(jnp.where(mask, q, p), shift, 0)
    return pltpu.bitcast(_packed(jnp.where(mask, p, q)) + _packed(moved), i32)


def _sublane_sums(ps):
    sub = lax.broadcasted_iota(i32, ps[0].shape, 0)
    quad = lambda a, b, o: _merge(a, b, ((sub - o) & 7) < 4, 4)
    duo = lambda a, b, o: _merge(a, b, ((sub - o) & 3) < 2, 6)
    r1 = duo(quad(ps[0], ps[1], 0), quad(ps[2], ps[3], 2), 0)
    r2 = duo(quad(ps[4], ps[5], 1), quad(ps[6], ps[7], 3), 1)
    return _merge(r1, r2, (sub & 1) == 0, 7)


def _peer_down_kernel(*refs):
    row_refs, (ne_ref, x_ref, gt_ref, sh_ref, tab_ref, ct_ref, part_ref, actt_ref) = refs[:SUBLANES], refs[SUBLANES:]
    hk, tb = ct_ref.shape
    n_groups = hk // SUBLANES
    lane = lax.broadcasted_iota(i32, (hk, tb), 1)

    def batch(b, carry):
        t0 = b * TOKEN_BATCH

        def token(u, c1):
            t = t0 + u
            xh = pltpu.bitcast(x_ref[t].astype(bf16).astype(f32), i32)
            xw = _packed(xh | lax.shift_right_logical(xh, 16))

            def group(g, c2):
                k0 = pl.multiple_of(g * SUBLANES, SUBLANES)
                tg = t * n_groups + g
                ps = []
                for k in _TREE_ORDER:
                    w = tab_ref[pl.ds(pl.multiple_of(row_refs[k][tg], SUBLANES), SUBLANES), :]
                    ps.append(pltpu.bitcast(_packed(w) * xw, i32))
                sums = _sublane_sums(ps)
                shift = sh_ref[pl.ds(pl.multiple_of(ne_ref[tg] * SUBLANES, SUBLANES), SUBLANES), :]
                part_ref[u, pl.ds(k0, SUBLANES), :] = pltpu.bitcast(jnp.left_shift(sums, shift) & jnp.int32(-65536), f32)
                return c2

            lax.fori_loop(0, n_groups, group, 0, unroll=4 * GROUP_UNROLL)
            return c1

        lax.fori_loop(0, TOKEN_BATCH, token, 0)
        a = actt_ref[...]
        for u in range(TOKEN_BATCH):
            a = jnp.where(lane == t0 + u, jnp.sum(part_ref[u], axis=1, keepdims=True), a)
        actt_ref[...] = a
        return carry

    lax.fori_loop(0, tb // TOKEN_BATCH, batch, 0)
    a = actt_ref[...]
    gelu = 0.5 * a * (1.0 + lax.erf(a * (2.0 ** -0.5)))
    ct_ref[...] = gt_ref[...] * gelu


def _peer_specs(tb, hk):
    per_group = pl.BlockSpec((tb * hk // SUBLANES,), lambda i: (i,), memory_space=pltpu.SMEM)
    whole = pl.BlockSpec(memory_space=pltpu.VMEM)
    return per_group, whole


def _peer_down(rows8, n_even, x3, gates_t, tab, tb):
    hk, T = gates_t.shape
    per_group, whole = _peer_specs(tb, hk)
    per_k = pl.BlockSpec((hk, tb), lambda i: (0, i))
    return pl.pallas_call(
        _peer_down_kernel, grid=(T // tb,), out_shape=jax.ShapeDtypeStruct((hk, T), f32), out_specs=per_k,
        in_specs=[per_group] * (SUBLANES + 1) + [pl.BlockSpec((tb,) + x3.shape[1:], lambda i: (i, 0, 0)), per_k,
                                                 whole, whole],
        scratch_shapes=[pltpu.VMEM((TOKEN_BATCH, hk, LANES), f32), pltpu.VMEM((hk, tb), f32)],
        compiler_params=_params("arbitrary"), name="peer_down",
    )(*rows8, n_even, x3, gates_t, _shift_patterns(), tab)


def _peer_up_kernel(*refs):
    row_refs, (ne_ref, ct_ref, sh_ref, tab_ref, o_ref, cx_ref) = refs[:SUBLANES], refs[SUBLANES:]
    hk, tb = ct_ref.shape
    n_groups = hk // SUBLANES
    n_acc = 4
    vreg = o_ref.shape[1:]
    lane = lax.broadcasted_iota(i32, (hk, tb), 1)

    def batch(b, carry):
        t0 = b * TOKEN_BATCH
        ct = ct_ref[...]
        for u in range(TOKEN_BATCH):
            col = jnp.sum(jnp.where(lane == t0 + u, ct, 0.0), axis=1, keepdims=True)
            cx_ref[u] = jnp.broadcast_to(col, (hk, LANES))

        def token(u, c1):
            t = t0 + u

            def fetch(g):
                tg = t * n_groups + g
                return tuple(tab_ref[pl.ds(pl.multiple_of(r[tg], SUBLANES), SUBLANES), :] for r in row_refs)

            def group(g, carry2):
                accs, packed = list(carry2[0]), carry2[1]
                nxt = fetch((g + 1) & (n_groups - 1))
                k0 = pl.multiple_of(g * SUBLANES, SUBLANES)
                n_even = ne_ref[t * n_groups + g]
                for p in range(SUBLANES):
                    coef = jnp.broadcast_to(cx_ref[u, pl.ds(k0 + p, 1), :], vreg)
                    accs[p % n_acc] = accs[p % n_acc] + _unpack(packed[p], sh_ref, n_even, p) * coef
                return tuple(accs), nxt

            accs, _ = lax.fori_loop(0, n_groups, group, ((jnp.zeros(vreg, f32),) * n_acc, fetch(0)),
                                    unroll=GROUP_UNROLL)
            o_ref[t] = (accs[0] + accs[1]) + (accs[2] + accs[3])
            return c1

        lax.fori_loop(0, TOKEN_BATCH, token, 0)
        return carry

    lax.fori_loop(0, tb // TOKEN_BATCH, batch, 0)


def _peer_up(rows8, n_even, coef_t, tab, tb):
    hk, T = coef_t.shape
    per_group, whole = _peer_specs(tb, hk)
    return pl.pallas_call(
        _peer_up_kernel, grid=(T // tb,), out_shape=jax.ShapeDtypeStruct((T, SUBLANES, LANES), f32),
        out_specs=pl.BlockSpec((tb, SUBLANES, LANES), lambda i: (i, 0, 0)),
        in_specs=[per_group] * (SUBLANES + 1) + [pl.BlockSpec((hk, tb), lambda i: (0, i)), whole, whole],
        scratch_shapes=[pltpu.VMEM((TOKEN_BATCH, hk, LANES), f32)],
        compiler_params=_params("arbitrary"), name="peer_up",
    )(*rows8, n_even, coef_t, _shift_patterns(), tab)


def _ffn_out_kernel(h_ref, f_ref, g_ref, b_ref, o_ref, *, alpha):
    o_ref[...] = _layer_norm(alpha * h_ref[...] + f_ref[...], g_ref[...], b_ref[...])


def _ffn_out(h2, ff, g, b, alpha, tm):
    T, D = h2.shape
    g, b = g.reshape(1, D), b.reshape(1, D)
    tok = pl.BlockSpec((tm, D), lambda i: (i, 0))
    full = pl.BlockSpec((1, D), lambda i: (0, 0))
    return pl.pallas_call(
        functools.partial(_ffn_out_kernel, alpha=alpha), grid=(T // tm,),
        out_shape=jax.ShapeDtypeStruct((T, D), f32), out_specs=tok, in_specs=[tok, tok, full, full],
        compiler_params=_params("parallel"), name="ffn_out",
    )(h2, ff, g, b)


def _tile(n, want):
    t = min(n, want)
    assert n % t == 0, (n, t)
    return t


def kernel(x, positions, mem, w_in, gla_gate_up, gla_gate_bias, gla_norm_g, w_out, ln_mix_g, ln_mix_b, xattn_w_q, xattn_w_k, xattn_w_v, xattn_w_o, ln_mem_g, ln_mem_b, peer_w_query, peer_sub_keys_1, peer_sub_keys_2, peer_expert_down, peer_expert_up, ln_ffn_g, ln_ffn_b):
    B, S, D = x.shape
    T = B * S
    depth = w_in.shape[0]
    alpha = (2.0 * depth) ** 0.25
    tm = _tile(S, 512)
    h = x
    for l in range(depth):
        q, kt, v, qi, kit, misc, gq, gk, gv, la, gr = _proj(h, positions, w_in[l], gla_gate_up[l], gla_gate_bias[l], tm)
        y_dsa = _dsa(q, kt, v, qi, kit, misc, _tile(S, 128), _tile(S, 512))
        y_gla = _gla(gq, gk, gv, la, gr, gla_norm_g[l], tm)
        h1 = _mix_out(h.reshape(T, D), y_dsa.reshape(T, W_DSA), y_gla.reshape(T, W_GV), w_out[l],
                      ln_mix_g[l], ln_mix_b[l], alpha, tm)
        km, vm = _mem_kv(mem, xattn_w_k[l], xattn_w_v[l])
        h2 = _xattn(h1.reshape(B, S, D), km, vm, xattn_w_q[l], xattn_w_o[l], ln_mem_g[l], ln_mem_b[l], alpha, tm)
        h2 = h2.reshape(T, D)
        rows_t, ne_t, gates_t = _route(h2, peer_w_query[l], peer_sub_keys_1[l], peer_sub_keys_2[l], _tile(T, 256))
        flat = lambda a: a.T.reshape(-1)
        n_even = flat(ne_t)
        rows8 = [flat(rows_t[p::SUBLANES]) for p in range(SUBLANES)]
        tb = _tile(T, LANES)
        coef_t = _peer_down(rows8, n_even, h2.reshape(T, SUBLANES, LANES), gates_t, _pack_table(peer_expert_down[l]), tb)
        ff = _peer_up(rows8, n_even, coef_t, _pack_table(peer_expert_up[l]), tb)
        h = _ffn_out(h2, ff.reshape(T, D), ln_ffn_g[l], ln_ffn_b[l], alpha, tm).reshape(B, S, D)
    return h
```

```python
import functools

import jax
import jax.numpy as jnp
import numpy as np
from jax import lax
from jax.experimental import pallas as pl
from jax.experimental.pallas import tpu as pltpu

f32 = jnp.float32
bf16 = jnp.bfloat16
i32 = jnp.int32

DSA_HEADS = 8
DSA_HEAD_DIM = 64
IDX_HEADS = 8
IDX_DIM = 32
TOPK_MAX = 256
GLA_HEADS = 4
GLA_DK = 64
GLA_DV = 128
GLA_GATE_RANK = 16
GLA_GATE_TEMP = 16.0
GLA_CHUNK = 64
ROPE_THETA = 500000.0
ROPE_FRACTION = 4
XATTN_HEADS = 4
PEER_N_KEYS = 128
PEER_HEADS = 8
PEER_D_KEY = 256
PEER_TOPK = 16
LN_EPS = 1e-5
RMS_EPS = 1e-6

LANES = 128
SUBLANES = 8
VMEM_LIMIT = 56 * 1024 * 1024

INT_MIN = -(2 ** 31)
NEG_INF = float("-inf")

W_DSA = DSA_HEADS * DSA_HEAD_DIM
W_IDX = IDX_HEADS * IDX_DIM
W_GQK = GLA_HEADS * GLA_DK
W_GV = GLA_HEADS * GLA_DV
MISC_KI = 0
MISC_WI = IDX_DIM
MISC_LR = IDX_DIM + IDX_HEADS


def _dot(a, b, dims=(((1,), (0,)), ((), ())), precision=None):
    return lax.dot_general(a, b, dims, precision=precision, preferred_element_type=f32)


_NN = (((1,), (0,)), ((), ()))
_NT = (((1,), (1,)), ((), ()))
_TN = (((0,), (0,)), ((), ()))


def _params(*sem):
    return pltpu.CompilerParams(dimension_semantics=sem, vmem_limit_bytes=VMEM_LIMIT)


def _layer_norm(y, g, b):
    mu = jnp.mean(y, axis=-1, keepdims=True)
    yc = y - mu
    var = jnp.mean(yc * yc, axis=-1, keepdims=True)
    return yc * lax.rsqrt(var + LN_EPS) * g + b


def _rot(xb, c, sa, sb, half):
    return xb * c + pltpu.roll(xb, LANES - half, 1) * sa + pltpu.roll(xb, half, 1) * sb


def _proj_kernel(x_ref, w_ref, gup_ref, gb_ref, cq_ref, saq_ref, sbq_ref, ci_ref, sai_ref, sbi_ref,
                 q_ref, kt_ref, v_ref, qi_ref, kit_ref, misc_ref, gq_ref, gk_ref, gv_ref, la_ref, gr_ref):
    x = x_ref[...].astype(bf16)
    tm = x.shape[0]
    cq, saq, sbq = cq_ref[...], saq_ref[...], sbq_ref[...]
    ci, sai, sbi = ci_ref[...], sai_ref[...], sbi_ref[...]
    hq = DSA_HEAD_DIM // ROPE_FRACTION // 2
    hi = IDX_DIM // ROPE_FRACTION // 2
    o = 0
    scale = DSA_HEAD_DIM ** -0.5
    for j in range(W_DSA // LANES):
        a = _dot(x, w_ref[:, o + j * LANES:o + (j + 1) * LANES])
        q_ref[:, j * LANES:(j + 1) * LANES] = (_rot(a, cq, saq, sbq, hq) * scale).astype(bf16)
    o += W_DSA
    for j in range(W_DSA // LANES):
        a = _dot(x, w_ref[:, o + j * LANES:o + (j + 1) * LANES])
        kt_ref[j * LANES:(j + 1) * LANES, :] = _rot(a, cq, saq, sbq, hq).T.astype(bf16)
    o += W_DSA
    v_ref[...] = _dot(x, w_ref[:, o:o + W_DSA]).astype(bf16)
    o += W_DSA
    for j in range(W_IDX // LANES):
        a = _dot(x, w_ref[:, o + j * LANES:o + (j + 1) * LANES])
        qi_ref[:, j * LANES:(j + 1) * LANES] = _rot(a, ci, sai, sbi, hi).astype(bf16)
    o += W_IDX
    m = _dot(x, w_ref[:, o:o + LANES])
    lane = lax.broadcasted_iota(i32, (tm, LANES), 1)
    is_ki = lane < IDX_DIM
    m = _rot(m, jnp.where(is_ki, ci, 1.0), jnp.where(is_ki, sai, 0.0), jnp.where(is_ki, sbi, 0.0), hi)
    misc_ref[...] = m
    kit_ref[...] = m.T[:IDX_DIM, :].astype(bf16)
    z = _dot(m.astype(bf16), gup_ref[...]) + gb_ref[...]
    la_ref[...] = (jnp.minimum(z, 0.0) - jnp.log1p(jnp.exp(-jnp.abs(z)))) / GLA_GATE_TEMP
    o += LANES
    gq_ref[...] = _dot(x, w_ref[:, o:o + W_GQK])
    o += W_GQK
    gk_ref[...] = _dot(x, w_ref[:, o:o + W_GQK])
    o += W_GQK
    gv_ref[...] = _dot(x, w_ref[:, o:o + W_GV]).astype(bf16)
    o += W_GV
    gr_ref[...] = _dot(x, w_ref[:, o:o + W_GV])


def _rot_tables(positions, head_dim):
    r = head_dim // ROPE_FRACTION
    half = r // 2
    inv_freq = ROPE_THETA ** (-jnp.arange(half, dtype=f32) / half)
    ang = positions.astype(f32)[..., None] * inv_freq
    cos, sin = jnp.cos(ang), jnp.sin(ang)
    lane = np.arange(LANES) % head_dim
    src = np.where(lane < half, lane, np.clip(lane - half, 0, half - 1))
    cos_l, sin_l = cos[..., src], sin[..., src]
    c = jnp.where(lane < r, cos_l, 1.0)
    sa = jnp.where(lane < half, -sin_l, 0.0)
    sb = jnp.where((lane >= half) & (lane < r), sin_l, 0.0)
    return c, sa, sb


def _proj(x, positions, w_in, gate_up, gate_bias, tm):
    B, S, D = x.shape
    splits = np.cumsum([W_DSA, W_DSA, W_DSA, W_IDX, IDX_DIM, IDX_HEADS, W_GQK, W_GQK, W_GV, GLA_GATE_RANK])
    (wq, wk, wv, wqi, wki, wwi, wgq, wgk, wgv, wlr, wgr) = jnp.split(w_in, splits.tolist(), axis=1)
    pad = jnp.zeros((D, LANES - IDX_DIM - IDX_HEADS - GLA_GATE_RANK), w_in.dtype)
    w_a = jnp.concatenate([wq, wk, wv, wqi, wki, wwi, wlr, pad, wgq, wgk, wgv, wgr], axis=1).astype(bf16)
    gup = jnp.zeros((LANES, W_GQK), f32).at[MISC_LR:MISC_LR + GLA_GATE_RANK].set(gate_up).astype(bf16)
    tabs = _rot_tables(positions, DSA_HEAD_DIM) + _rot_tables(positions, IDX_DIM)
    W = w_a.shape[1]
    tok = lambda n: pl.BlockSpec((None, tm, n), lambda b, j: (b, j, 0))
    full = lambda a: pl.BlockSpec(a.shape, lambda b, j: (0,) * a.ndim)
    tr = lambda n: pl.BlockSpec((None, n, tm), lambda b, j: (b, 0, j))
    sd = jax.ShapeDtypeStruct
    out_shape = [sd((B, S, W_DSA), bf16), sd((B, W_DSA, S), bf16), sd((B, S, W_DSA), bf16), sd((B, S, W_IDX), bf16),
                 sd((B, IDX_DIM, S), bf16), sd((B, S, LANES), f32), sd((B, S, W_GQK), f32), sd((B, S, W_GQK), f32),
                 sd((B, S, W_GV), bf16), sd((B, S, W_GQK), f32), sd((B, S, W_GV), f32)]
    out_specs = [tok(W_DSA), tr(W_DSA), tok(W_DSA), tok(W_IDX), tr(IDX_DIM), tok(LANES), tok(W_GQK), tok(W_GQK),
                 tok(W_GV), tok(W_GQK), tok(W_GV)]
    gb = gate_bias.reshape(1, W_GQK)
    return pl.pallas_call(
        _proj_kernel, grid=(B, S // tm), out_shape=out_shape, out_specs=out_specs,
        in_specs=[tok(D), full(w_a), full(gup), full(gb)] + [tok(LANES)] * 6,
        compiler_params=_params("parallel", "parallel"), name="proj",
    )(x, w_a, gup, gb, *tabs)


def _dsa_kernel(q_ref, kt_ref, v_ref, qi_ref, kit_ref, misc_ref, o_ref, key_ref, bias_ref, lg_ref, mx_ref, ls_ref,
                acc_ref, *, n_sel, idx_bits, kc):
    qb, S = q_ref.shape[0], kt_ref.shape[1]
    i = pl.program_id(1)
    nk = lax.div((i + 1) * qb + (kc - 1), kc)
    qi = qi_ref[...]
    wi = misc_ref[:, MISC_WI:MISC_WI + IDX_HEADS] * (IDX_HEADS ** -0.5)
    qpos = lax.broadcasted_iota(i32, (qb, 1), 0) + i * qb
    target = jnp.minimum(n_sel, qpos + 1).astype(f32)
    lane = lax.broadcasted_iota(i32, (qb, kc), 1)

    def chunk(c):
        return pl.ds(pl.multiple_of(c * kc, kc), kc)

    def score_chunk(c, carry):
        kit = kit_ref[:, chunk(c)]
        score = jnp.zeros((qb, kc), f32)
        for h in range(IDX_HEADS):
            d = _dot(qi[:, h * IDX_DIM:(h + 1) * IDX_DIM], kit) * (IDX_DIM ** -0.5)
            score = score + wi[:, h:h + 1] * jnp.maximum(d, 0.0)
        score = jnp.where(score == 0.0, 0.0, score)
        bits = pltpu.bitcast(score, i32)
        key = bits ^ ((bits >> 31) & jnp.int32(0x7FFFFFFF))
        key_ref[:, chunk(c)] = jnp.where(lane + c * kc <= qpos, key, INT_MIN)
        return carry

    lax.fori_loop(0, nk, score_chunk, 0)

    def count(pred):
        def body(c, acc):
            hit = jnp.where(pred(key_ref[:, chunk(c)], lane + c * kc), 1.0, 0.0)
            for j in range(kc // LANES):
                acc = acc + hit[:, j * LANES:(j + 1) * LANES]
            return acc

        acc = lax.fori_loop(0, nk, body, jnp.zeros((qb, LANES), f32))
        return jnp.sum(acc, axis=1, keepdims=True)

    base = jnp.where(count(lambda k, col: k >= 0) >= target, jnp.int32(0), jnp.int32(INT_MIN))

    def tau_bit(t, base):
        cand = base | jnp.left_shift(jnp.int32(1), 30 - t)
        return jnp.where(count(lambda k, col: k >= cand) >= target, cand, base)

    tau = lax.fori_loop(0, 31, tau_bit, base)
    excess = jnp.max(count(lambda k, col: k >= tau) - target)

    def tie_limit():
        need = target - count(lambda k, col: k > tau)

        def idx_bit(t, m):
            cand = m | jnp.left_shift(jnp.int32(1), idx_bits - 1 - t)
            return jnp.where(count(lambda k, col: (k == tau) & (col < cand)) < need, cand, m)

        return lax.fori_loop(0, idx_bits, idx_bit, jnp.zeros((qb, 1), i32))

    m_idx = lax.cond(excess > 0.0, tie_limit, lambda: jnp.full((qb, 1), S, i32))

    def bias_chunk(c, carry):
        k = key_ref[:, chunk(c)]
        sel = (k > tau) | ((k == tau) & (lane + c * kc <= m_idx))
        bias_ref[:, chunk(c)] = jnp.where(sel, 0.0, NEG_INF)
        return carry

    lax.fori_loop(0, nk, bias_chunk, 0)

    q = q_ref[...]
    pair = LANES // DSA_HEAD_DIM

    def fold(x, op, acc):
        for j in range(kc // LANES):
            acc = op(acc, x[:, j * LANES:(j + 1) * LANES])
        return acc

    heads = range(DSA_HEADS)
    head_rows = [slice(h * DSA_HEAD_DIM, (h + 1) * DSA_HEAD_DIM) for h in heads]
    mx_ref[...] = jnp.full(mx_ref.shape, NEG_INF, f32)
    ls_ref[...] = jnp.zeros_like(ls_ref)
    acc_ref[...] = jnp.zeros_like(acc_ref)

    def logits(c, carry):
        bias = bias_ref[:, chunk(c)]
        for h in heads:
            lg = _dot(q[:, head_rows[h]], kt_ref[head_rows[h], chunk(c)]) + bias
            lg_ref[h, :, chunk(c)] = lg
            mx_ref[h] = fold(lg, jnp.maximum, mx_ref[h])
        return carry

    lax.fori_loop(0, nk, logits, 0)
    ms = [jnp.max(mx_ref[h], axis=1, keepdims=True) for h in heads]

    def weigh(c, carry):
        for h in heads:
            p = jnp.exp(lg_ref[h, :, chunk(c)] - ms[h])
            ls_ref[h] = fold(p, jnp.add, ls_ref[h])
            slab = slice((h // pair) * LANES, (h // pair + 1) * LANES)
            acc_ref[h] = acc_ref[h] + _dot(p.astype(bf16), v_ref[chunk(c), slab])
        return carry

    lax.fori_loop(0, nk, weigh, 0)
    for h in heads:
        l = jnp.sum(ls_ref[h], axis=1, keepdims=True)
        off = (h % pair) * DSA_HEAD_DIM
        o_ref[:, head_rows[h]] = (acc_ref[h][:, off:off + DSA_HEAD_DIM] / l).astype(o_ref.dtype)


def _dsa(q, kt, v, qi, kit, misc, qb, kc):
    B, S, _ = q.shape
    n_sel = min(TOPK_MAX, S // 4)
    blk = lambda n: pl.BlockSpec((None, qb, n), lambda b, i: (b, i, 0))
    per_b = lambda r, c: pl.BlockSpec((None, r, c), lambda b, i: (b, 0, 0))
    kern = functools.partial(_dsa_kernel, n_sel=n_sel, idx_bits=max(1, (S - 1).bit_length()), kc=kc)
    return pl.pallas_call(
        kern, grid=(B, S // qb), out_shape=jax.ShapeDtypeStruct((B, S, W_DSA), bf16), out_specs=blk(W_DSA),
        in_specs=[blk(W_DSA), per_b(W_DSA, S), per_b(S, W_DSA), blk(W_IDX), per_b(IDX_DIM, S), blk(LANES)],
        scratch_shapes=[pltpu.VMEM((qb, S), i32), pltpu.VMEM((qb, S), f32), pltpu.VMEM((DSA_HEADS, qb, S), f32)]
        + [pltpu.VMEM((DSA_HEADS, qb, LANES), f32)] * 3,
        compiler_params=_params("parallel", "arbitrary"), name="dsa",
    )(q, kt, v, qi, kit, misc)


def _gla_kernel(gq_ref, gk_ref, gv_ref, la_ref, gr_ref, ng_ref, o_ref, state_ref):
    ct = gq_ref.shape[0]
    nch = ct // GLA_CHUNK

    @pl.when(pl.program_id(1) == 0)
    def _():
        state_ref[...] = jnp.zeros_like(state_ref)

    la = la_ref[...]
    r = lax.broadcasted_iota(i32, (ct, ct), 0)
    c = lax.broadcasted_iota(i32, (ct, ct), 1)
    same = (r // GLA_CHUNK) == (c // GLA_CHUNK)
    causal = same & (c <= r)
    hp = lax.Precision.HIGHEST
    bcum = _dot(jnp.where(causal, 1.0, 0.0), la, precision=hp)
    blast = _dot(jnp.where(same, 1.0, 0.0), la, precision=hp)
    q_dec = (gq_ref[...] * (GLA_DK ** -0.5) * jnp.exp(bcum)).astype(bf16)
    k_inv = (gk_ref[...] * jnp.exp(-bcum)).astype(bf16)
    k_end = (gk_ref[...] * jnp.exp(blast - bcum)).astype(bf16)
    decay = jnp.exp(blast)
    ng = ng_ref[...]
    for h in range(GLA_HEADS):
        ks = slice(h * GLA_DK, (h + 1) * GLA_DK)
        vs = slice(h * GLA_DV, (h + 1) * GLA_DV)
        qd, ki, ke, vh = q_dec[:, ks], k_inv[:, ks], k_end[:, ks], gv_ref[:, vs]
        attn = jnp.where(causal, _dot(qd, ki, _NT), 0.0)
        o = _dot(attn.astype(bf16), vh)
        st = state_ref[h]
        inter = []
        for n in range(nch):
            rows = slice(n * GLA_CHUNK, (n + 1) * GLA_CHUNK)
            inter.append(_dot(qd[rows], st.astype(bf16), _NT))
            st = st * decay[n * GLA_CHUNK:n * GLA_CHUNK + 1, ks] + _dot(vh[rows], ke[rows], _TN)
        state_ref[h] = st
        o = o + jnp.concatenate(inter, axis=0)
        o = o * lax.rsqrt(jnp.mean(o * o, axis=-1, keepdims=True) + RMS_EPS) * ng
        g = gr_ref[:, vs]
        o_ref[:, vs] = (o * (g * jax.nn.sigmoid(g))).astype(o_ref.dtype)


def _gla(gq, gk, gv, la, gr, norm_g, ct):
    B, S, _ = gq.shape
    blk = lambda n: pl.BlockSpec((None, ct, n), lambda b, j: (b, j, 0))
    ng = norm_g.reshape(1, GLA_DV)
    return pl.pallas_call(
        _gla_kernel, grid=(B, S // ct), out_shape=jax.ShapeDtypeStruct((B, S, W_GV), bf16), out_specs=blk(W_GV),
        in_specs=[blk(W_GQK), blk(W_GQK), blk(W_GV), blk(W_GQK), blk(W_GV), pl.BlockSpec(ng.shape, lambda b, j: (0, 0))],
        scratch_shapes=[pltpu.VMEM((GLA_HEADS, GLA_DV, GLA_DK), f32)],
        compiler_params=_params("parallel", "arbitrary"), name="gla",
    )(gq, gk, gv, la, gr, ng)


def _mix_out_kernel(x_ref, ya_ref, yb_ref, wa_ref, wb_ref, g_ref, b_ref, o_ref, *, alpha):
    mix = _dot(ya_ref[...], wa_ref[...]) + _dot(yb_ref[...], wb_ref[...])
    o_ref[...] = _layer_norm(alpha * x_ref[...] + mix, g_ref[...], b_ref[...])


def _mix_out(x2, ya, yb, w_out, g, b, alpha, tm):
    T, D = x2.shape
    wa, wb = w_out[:W_DSA].astype(bf16), w_out[W_DSA:].astype(bf16)
    tok = lambda n: pl.BlockSpec((tm, n), lambda i: (i, 0))
    full = lambda a: pl.BlockSpec(a.shape, lambda i: (0, 0))
    g, b = g.reshape(1, D), b.reshape(1, D)
    return pl.pallas_call(
        functools.partial(_mix_out_kernel, alpha=alpha), grid=(T // tm,),
        out_shape=jax.ShapeDtypeStruct((T, D), f32), out_specs=tok(D),
        in_specs=[tok(D), tok(W_DSA), tok(W_GV), full(wa), full(wb), full(g), full(b)],
        compiler_params=_params("parallel"), name="mix_out",
    )(x2, ya, yb, wa, wb, g, b)


def _mem_kv_kernel(m_ref, wk_ref, wv_ref, k_ref, v_ref):
    m = m_ref[...].astype(bf16)
    k_ref[...] = _dot(m, wk_ref[...]).astype(bf16)
    v_ref[...] = _dot(m, wv_ref[...]).astype(bf16)


def _mem_kv(mem, w_k, w_v):
    B, M, D = mem.shape
    wk, wv = w_k.astype(bf16), w_v.astype(bf16)
    blk = pl.BlockSpec((None, M, D), lambda b: (b, 0, 0))
    full = pl.BlockSpec((D, D), lambda b: (0, 0))
    sd = jax.ShapeDtypeStruct((B, M, D), bf16)
    return pl.pallas_call(_mem_kv_kernel, grid=(B,), out_shape=[sd, sd], out_specs=[blk, blk],
                          in_specs=[blk, full, full], compiler_params=_params("parallel"), name="mem_kv")(mem, wk, wv)


def _xattn_kernel(h_ref, k_ref, v_ref, wq_ref, wo_ref, g_ref, b_ref, o_ref, *, alpha):
    h = h_ref[...]
    D = h.shape[1]
    hd = D // XATTN_HEADS
    q = (_dot(h.astype(bf16), wq_ref[...]) * (hd ** -0.5)).astype(bf16)
    outs = []
    for a in range(XATTN_HEADS):
        s = slice(a * hd, (a + 1) * hd)
        lg = _dot(q[:, s], k_ref[:, s], _NT)
        p = jnp.exp(lg - jnp.max(lg, axis=1, keepdims=True))
        l = jnp.sum(p, axis=1, keepdims=True)
        outs.append((_dot(p.astype(bf16), v_ref[:, s]) / l).astype(bf16))
    ca = _dot(jnp.concatenate(outs, axis=1), wo_ref[...])
    o_ref[...] = _layer_norm(alpha * h + ca, g_ref[...], b_ref[...])


def _xattn(h1, km, vm, w_q, w_o, g, b, alpha, tm):
    B, S, D = h1.shape
    M = km.shape[1]
    wq, wo = w_q.astype(bf16), w_o.astype(bf16)
    g, b = g.reshape(1, D), b.reshape(1, D)
    tok = pl.BlockSpec((None, tm, D), lambda bi, j: (bi, j, 0))
    per_b = pl.BlockSpec((None, M, D), lambda bi, j: (bi, 0, 0))
    full = lambda a: pl.BlockSpec(a.shape, lambda bi, j: (0, 0))
    return pl.pallas_call(
        functools.partial(_xattn_kernel, alpha=alpha), grid=(B, S // tm),
        out_shape=jax.ShapeDtypeStruct((B, S, D), f32), out_specs=tok,
        in_specs=[tok, per_b, per_b, full(wq), full(wo), full(g), full(b)],
        compiler_params=_params("parallel", "parallel"), name="xattn",
    )(h1, km, vm, wq, wo, g, b)


def _top_rows(s, n_top, ids=None, payload=None):
    if ids is None:
        ids = lax.broadcasted_iota(i32, s.shape, 0)
    vals, picks = [], []
    for _ in range(n_top):
        m = jnp.max(s, axis=0, keepdims=True)
        am = jnp.min(jnp.where(s == m, ids, jnp.int32(2 ** 30)), axis=0, keepdims=True)
        hit = ids == am
        vals.append(m)
        picks.append(am if payload is None else jnp.max(jnp.where(hit, payload, -1), axis=0, keepdims=True))
        s = jnp.where(hit, NEG_INF, s)
    return jnp.concatenate(vals, axis=0), jnp.concatenate(picks, axis=0)


def _pair_candidates(v1, i1, v2, i2):
    n = PEER_TOPK
    sub = lax.broadcasted_iota(i32, (SUBLANES,) + v1.shape[1:], 0)
    blocks = [(v1[0:1] + v2, lax.broadcasted_iota(i32, v2.shape, 0), i1[0:1] * PEER_N_KEYS + i2)]
    for a in range(1, 5):
        blocks.append((v1[a:a + 1] + v2[:SUBLANES], a * n + sub, i1[a:a + 1] * PEER_N_KEYS + i2[:SUBLANES]))
    pick = lambda x: jnp.where(sub < 2, x[5:6], jnp.where(sub < 4, x[6:7], x[7:8]))
    alt = lambda x: jnp.where((sub & 1) == 0, x[0:1], x[1:2])
    a_of = jnp.where(sub < 2, 5, jnp.where(sub < 4, 6, 7))
    blocks.append((jnp.where(sub < 6, pick(v1) + alt(v2), NEG_INF), a_of * n + (sub & 1),
                   pick(i1) * PEER_N_KEYS + alt(i2)))
    blocks.append((v1[SUBLANES:] + v2[0:1], (sub + SUBLANES) * n, i1[SUBLANES:] * PEER_N_KEYS + i2[0:1]))
    return tuple(jnp.concatenate(parts, axis=0) for parts in zip(*blocks))


def _route_kernel(h_ref, wq_ref, k1_ref, k2_ref, row_ref, ne_ref, g_ref):
    q = _dot(h_ref[...].astype(bf16), wq_ref[...])
    dk = PEER_D_KEY // 2
    k1, k2 = k1_ref[...], k2_ref[...]
    for a in range(PEER_HEADS):
        qa = q[:, a * PEER_D_KEY:a * PEER_D_KEY + dk].astype(bf16)
        qb = q[:, a * PEER_D_KEY + dk:(a + 1) * PEER_D_KEY].astype(bf16)
        v1, i1 = _top_rows(_dot(k1, qa, _NT), PEER_TOPK)
        v2, i2 = _top_rows(_dot(k2, qb, _NT), PEER_TOPK)
        cand, ids, cidx = _pair_candidates(v1, i1, v2, i2)
        top, experts = _top_rows(cand, PEER_TOPK, ids=ids, payload=cidx)
        p = jnp.exp(top - top[0:1])
        gates = p / jnp.sum(p, axis=0, keepdims=True)
        for half in range(PEER_TOPK // SUBLANES):
            grp = slice(half * SUBLANES, (half + 1) * SUBLANES)
            e, g = experts[grp], gates[grp]
            odd = e & 1
            n_even = SUBLANES - jnp.sum(odd, axis=0, keepdims=True)
            sub = lax.broadcasted_iota(i32, e.shape, 0)
            evens_before = jnp.zeros_like(n_even)
            e_sorted, g_sorted = jnp.zeros_like(e), jnp.zeros_like(g)
            for r in range(SUBLANES):
                odd_r = odd[r:r + 1]
                dest = jnp.where(odd_r == 1, n_even + (r - evens_before), evens_before)
                hit = sub == dest
                e_sorted = jnp.where(hit, e[r:r + 1], e_sorted)
                g_sorted = jnp.where(hit, g[r:r + 1], g_sorted)
                evens_before = evens_before + (1 - odd_r)
            rows = slice(a * PEER_TOPK + half * SUBLANES, a * PEER_TOPK + (half + 1) * SUBLANES)
            row_ref[rows, :] = (e_sorted >> 1) * SUBLANES
            g_ref[rows, :] = g_sorted
            n = a * (PEER_TOPK // SUBLANES) + half
            ne_ref[n:n + 1, :] = n_even


def _route(h2, w_query, k1, k2, tm):
    T, D = h2.shape
    wq = w_query.astype(bf16)
    k1, k2 = k1.astype(bf16), k2.astype(bf16)
    hk = PEER_HEADS * PEER_TOPK
    full = lambda a: pl.BlockSpec(a.shape, lambda i: (0, 0))
    out = pl.BlockSpec((hk, tm), lambda i: (0, i))
    n_groups = hk // SUBLANES
    sd = jax.ShapeDtypeStruct
    return pl.pallas_call(
        _route_kernel, grid=(T // tm,),
        out_shape=[sd((hk, T), i32), sd((n_groups, T), i32), sd((hk, T), f32)],
        out_specs=[out, pl.BlockSpec((n_groups, tm), lambda i: (0, i)), out],
        in_specs=[pl.BlockSpec((tm, D), lambda i: (i, 0)), full(wq), full(k1), full(k2)],
        compiler_params=_params("parallel"), name="route",
    )(h2, wq, k1, k2)


def _pack_table(tab):
    n, d = tab.shape
    assert d == SUBLANES * LANES
    u = lax.bitcast_convert_type(tab.astype(bf16), jnp.uint16).astype(jnp.uint32).reshape(n // 2, 2, SUBLANES, LANES)
    return lax.bitcast_convert_type((u[:, 0] << 16) | u[:, 1], i32).reshape(n // 2 * SUBLANES, LANES)


def _shift_patterns():
    n_even = np.arange(SUBLANES + 1)[:, None, None]
    p = np.arange(SUBLANES)[None, :, None]
    pat = np.where(p < n_even, 0, 16) + np.zeros((1, 1, LANES), np.int64)
    return jnp.asarray(pat.reshape(-1, LANES), i32)


def _unpack(w, sh_ref, n_even, p):
    shift = jnp.broadcast_to(sh_ref[pl.ds(n_even * SUBLANES + p, 1), :], w.shape)
    return pltpu.bitcast(jnp.left_shift(w, shift) & jnp.int32(-65536), f32)


def _row(tab_ref, row0, sh_ref, n_even, p):
    return _unpack(tab_ref[pl.ds(pl.multiple_of(row0, SUBLANES), SUBLANES), :], sh_ref, n_even, p)


TOKEN_BATCH = 16
GROUP_UNROLL = 4


_TREE_ORDER = (0, 4, 2, 6, 1, 5, 3, 7)


def _packed(w):
    return pltpu.bitcast(w, bf16)


def _merge(p, q, mask, shift):
    moved = pltpu.roll(jnp.where(mask, q, p), shift, 0)
    return pltpu.bitcast(_packed(jnp.where(mask, p, q)) + _packed(moved), i32)


def _sublane_sums(ps):
    sub = lax.broadcasted_iota(i32, ps[0].shape, 0)
    quad = lambda a, b, o: _merge(a, b, ((sub - o) & 7) < 4, 4)
    duo = lambda a, b, o: _merge(a, b, ((sub - o) & 3) < 2, 6)
    r1 = duo(quad(ps[0], ps[1], 0), quad(ps[2], ps[3], 2), 0)
    r2 = duo(quad(ps[4], ps[5], 1), quad(ps[6], ps[7], 3), 1)
    return _merge(r1, r2, (sub & 1) == 0, 7)


def _peer_down_kernel(*refs):
    row_refs, (ne_ref, x_ref, gt_ref, sh_ref, tab_ref, ct_ref, part_ref, actt_ref) = refs[:SUBLANES], refs[SUBLANES:]
    hk, tb = ct_ref.shape
    n_groups = hk // SUBLANES
    lane = lax.broadcasted_iota(i32, (hk, tb), 1)

    def batch(b, carry):
        t0 = b * TOKEN_BATCH

        def token(u, c1):
            t = t0 + u
            xh = pltpu.bitcast(x_ref[t].astype(bf16).astype(f32), i32)
            xw = _packed(xh | lax.shift_right_logical(xh, 16))

            def group(g, c2):
                k0 = pl.multiple_of(g * SUBLANES, SUBLANES)
                tg = t * n_groups + g
                ps = []
                for k in _TREE_ORDER:
                    w = tab_ref[pl.ds(pl.multiple_of(row_refs[k][tg], SUBLANES), SUBLANES), :]
                    ps.append(pltpu.bitcast(_packed(w) * xw, i32))
                sums = _sublane_sums(ps)
                shift = sh_ref[pl.ds(pl.multiple_of(ne_ref[tg] * SUBLANES, SUBLANES), SUBLANES), :]
                part_ref[u, pl.ds(k0, SUBLANES), :] = pltpu.bitcast(jnp.left_shift(sums, shift) & jnp.int32(-65536), f32)
                return c2

            lax.fori_loop(0, n_groups, group, 0, unroll=4 * GROUP_UNROLL)
            return c1

        lax.fori_loop(0, TOKEN_BATCH, token, 0)
        a = actt_ref[...]
        for u in range(TOKEN_BATCH):
            a = jnp.where(lane == t0 + u, jnp.sum(part_ref[u], axis=1, keepdims=True), a)
        actt_ref[...] = a
        return carry

    lax.fori_loop(0, tb // TOKEN_BATCH, batch, 0)
    a = actt_ref[...]
    gelu = 0.5 * a * (1.0 + lax.erf(a * (2.0 ** -0.5)))
    ct_ref[...] = gt_ref[...] * gelu


def _peer_specs(tb, hk):
    per_group = pl.BlockSpec((tb * hk // SUBLANES,), lambda i: (i,), memory_space=pltpu.SMEM)
    whole = pl.BlockSpec(memory_space=pltpu.VMEM)
    return per_group, whole


def _peer_down(rows8, n_even, x3, gates_t, tab, tb):
    hk, T = gates_t.shape
    per_group, whole = _peer_specs(tb, hk)
    per_k = pl.BlockSpec((hk, tb), lambda i: (0, i))
    return pl.pallas_call(
        _peer_down_kernel, grid=(T // tb,), out_shape=jax.ShapeDtypeStruct((hk, T), f32), out_specs=per_k,
        in_specs=[per_group] * (SUBLANES + 1) + [pl.BlockSpec((tb,) + x3.shape[1:], lambda i: (i, 0, 0)), per_k,
                                                 whole, whole],
        scratch_shapes=[pltpu.VMEM((TOKEN_BATCH, hk, LANES), f32), pltpu.VMEM((hk, tb), f32)],
        compiler_params=_params("arbitrary"), name="peer_down",
    )(*rows8, n_even, x3, gates_t, _shift_patterns(), tab)


def _peer_up_kernel(*refs):
    row_refs, (ne_ref, ct_ref, sh_ref, tab_ref, o_ref, cx_ref) = refs[:SUBLANES], refs[SUBLANES:]
    hk, tb = ct_ref.shape
    n_groups = hk // SUBLANES
    n_acc = 4
    vreg = o_ref.shape[1:]
    lane = lax.broadcasted_iota(i32, (hk, tb), 1)

    def batch(b, carry):
        t0 = b * TOKEN_BATCH
        ct = ct_ref[...]
        for u in range(TOKEN_BATCH):
            col = jnp.sum(jnp.where(lane == t0 + u, ct, 0.0), axis=1, keepdims=True)
            cx_ref[u] = jnp.broadcast_to(col, (hk, LANES))

        def token(u, c1):
            t = t0 + u

            def fetch(g):
                tg = t * n_groups + g
                return tuple(tab_ref[pl.ds(pl.multiple_of(r[tg], SUBLANES), SUBLANES), :] for r in row_refs)

            def group(g, accs):
                accs, packed = list(accs), fetch(g)
                k0 = pl.multiple_of(g * SUBLANES, SUBLANES)
                n_even = ne_ref[t * n_groups + g]
                for p in range(SUBLANES):
                    coef = jnp.broadcast_to(cx_ref[u, pl.ds(k0 + p, 1), :], vreg)
                    accs[p % n_acc] = accs[p % n_acc] + _unpack(packed[p], sh_ref, n_even, p) * coef
                return tuple(accs)

            accs = lax.fori_loop(0, n_groups, group, (jnp.zeros(vreg, f32),) * n_acc, unroll=4 * GROUP_UNROLL)
            o_ref[t] = (accs[0] + accs[1]) + (accs[2] + accs[3])
            return c1

        lax.fori_loop(0, TOKEN_BATCH, token, 0)
        return carry

    lax.fori_loop(0, tb // TOKEN_BATCH, batch, 0)


def _peer_up(rows8, n_even, coef_t, tab, tb):
    hk, T = coef_t.shape
    per_group, whole = _peer_specs(tb, hk)
    return pl.pallas_call(
        _peer_up_kernel, grid=(T // tb,), out_shape=jax.ShapeDtypeStruct((T, SUBLANES, LANES), f32),
        out_specs=pl.BlockSpec((tb, SUBLANES, LANES), lambda i: (i, 0, 0)),
        in_specs=[per_group] * (SUBLANES + 1) + [pl.BlockSpec((hk, tb), lambda i: (0, i)), whole, whole],
        scratch_shapes=[pltpu.VMEM((TOKEN_BATCH, hk, LANES), f32)],
        compiler_params=_params("arbitrary"), name="peer_up",
    )(*rows8, n_even, coef_t, _shift_patterns(), tab)


def _ffn_out_kernel(h_ref, f_ref, g_ref, b_ref, o_ref, *, alpha):
    o_ref[...] = _layer_norm(alpha * h_ref[...] + f_ref[...], g_ref[...], b_ref[...])


def _ffn_out(h2, ff, g, b, alpha, tm):
    T, D = h2.shape
    g, b = g.reshape(1, D), b.reshape(1, D)
    tok = pl.BlockSpec((tm, D), lambda i: (i, 0))
    full = pl.BlockSpec((1, D), lambda i: (0, 0))
    return pl.pallas_call(
        functools.partial(_ffn_out_kernel, alpha=alpha), grid=(T // tm,),
        out_shape=jax.ShapeDtypeStruct((T, D), f32), out_specs=tok, in_specs=[tok, tok, full, full],
        compiler_params=_params("parallel"), name="ffn_out",
    )(h2, ff, g, b)


def _tile(n, want):
    t = min(n, want)
    assert n % t == 0, (n, t)
    return t


def kernel(x, positions, mem, w_in, gla_gate_up, gla_gate_bias, gla_norm_g, w_out, ln_mix_g, ln_mix_b, xattn_w_q, xattn_w_k, xattn_w_v, xattn_w_o, ln_mem_g, ln_mem_b, peer_w_query, peer_sub_keys_1, peer_sub_keys_2, peer_expert_down, peer_expert_up, ln_ffn_g, ln_ffn_b):
    B, S, D = x.shape
    T = B * S
    depth = w_in.shape[0]
    alpha = (2.0 * depth) ** 0.25
    tm = _tile(S, 512)
    h = x
    for l in range(depth):
        q, kt, v, qi, kit, misc, gq, gk, gv, la, gr = _proj(h, positions, w_in[l], gla_gate_up[l], gla_gate_bias[l], tm)
        y_dsa = _dsa(q, kt, v, qi, kit, misc, _tile(S, 128), _tile(S, 512))
        y_gla = _gla(gq, gk, gv, la, gr, gla_norm_g[l], tm)
        h1 = _mix_out(h.reshape(T, D), y_dsa.reshape(T, W_DSA), y_gla.reshape(T, W_GV), w_out[l],
                      ln_mix_g[l], ln_mix_b[l], alpha, tm)
        km, vm = _mem_kv(mem, xattn_w_k[l], xattn_w_v[l])
        h2 = _xattn(h1.reshape(B, S, D), km, vm, xattn_w_q[l], xattn_w_o[l], ln_mem_g[l], ln_mem_b[l], alpha, tm)
        h2 = h2.reshape(T, D)
        rows_t, ne_t, gates_t = _route(h2, peer_w_query[l], peer_sub_keys_1[l], peer_sub_keys_2[l], _tile(T, 256))
        flat = lambda a: a.T.reshape(-1)
        n_even = flat(ne_t)
        rows8 = [flat(rows_t[p::SUBLANES]) for p in range(SUBLANES)]
        tb = _tile(T, LANES)
        coef_t = _peer_down(rows8, n_even, h2.reshape(T, SUBLANES, LANES), gates_t, _pack_table(peer_expert_down[l]), tb)
        ff = _peer_up(rows8, n_even, coef_t, _pack_table(peer_expert_up[l]), tb)
        h = _ffn_out(h2, ff.reshape(T, D), ln_ffn_g[l], ln_ffn_b[l], alpha, tm).reshape(B, S, D)
    return h
```

```python
import functools

import jax
import jax.numpy as jnp
import numpy as np
from jax import lax
from jax.experimental import pallas as pl
from jax.experimental.pallas import tpu as pltpu

f32 = jnp.float32
bf16 = jnp.bfloat16
i32 = jnp.int32

DSA_HEADS = 8
DSA_HEAD_DIM = 64
IDX_HEADS = 8
IDX_DIM = 32
TOPK_MAX = 256
GLA_HEADS = 4
GLA_DK = 64
GLA_DV = 128
GLA_GATE_RANK = 16
GLA_GATE_TEMP = 16.0
GLA_CHUNK = 64
ROPE_THETA = 500000.0
ROPE_FRACTION = 4
XATTN_HEADS = 4
PEER_N_KEYS = 128
PEER_HEADS = 8
PEER_D_KEY = 256
PEER_TOPK = 16
LN_EPS = 1e-5
RMS_EPS = 1e-6

LANES = 128
SUBLANES = 8
VMEM_LIMIT = 56 * 1024 * 1024

INT_MIN = -(2 ** 31)
NEG_INF = float("-inf")

W_DSA = DSA_HEADS * DSA_HEAD_DIM
W_IDX = IDX_HEADS * IDX_DIM
W_GQK = GLA_HEADS * GLA_DK
W_GV = GLA_HEADS * GLA_DV
MISC_KI = 0
MISC_WI = IDX_DIM
MISC_LR = IDX_DIM + IDX_HEADS


def _dot(a, b, dims=(((1,), (0,)), ((), ())), precision=None):
    return lax.dot_general(a, b, dims, precision=precision, preferred_element_type=f32)


_NN = (((1,), (0,)), ((), ()))
_NT = (((1,), (1,)), ((), ()))
_TN = (((0,), (0,)), ((), ()))


def _params(*sem):
    return pltpu.CompilerParams(dimension_semantics=sem, vmem_limit_bytes=VMEM_LIMIT)


def _layer_norm(y, g, b):
    mu = jnp.mean(y, axis=-1, keepdims=True)
    yc = y - mu
    var = jnp.mean(yc * yc, axis=-1, keepdims=True)
    return yc * lax.rsqrt(var + LN_EPS) * g + b


def _rot(xb, c, sa, sb, half):
    return xb * c + pltpu.roll(xb, LANES - half, 1) * sa + pltpu.roll(xb, half, 1) * sb


def _proj_kernel(x_ref, w_ref, gup_ref, gb_ref, cq_ref, saq_ref, sbq_ref, ci_ref, sai_ref, sbi_ref,
                 q_ref, kt_ref, v_ref, qi_ref, kit_ref, misc_ref, gq_ref, gk_ref, gv_ref, la_ref, gr_ref):
    x = x_ref[...].astype(bf16)
    tm = x.shape[0]
    cq, saq, sbq = cq_ref[...], saq_ref[...], sbq_ref[...]
    ci, sai, sbi = ci_ref[...], sai_ref[...], sbi_ref[...]
    hq = DSA_HEAD_DIM // ROPE_FRACTION // 2
    hi = IDX_DIM // ROPE_FRACTION // 2
    o = 0
    scale = DSA_HEAD_DIM ** -0.5
    for j in range(W_DSA // LANES):
        a = _dot(x, w_ref[:, o + j * LANES:o + (j + 1) * LANES])
        q_ref[:, j * LANES:(j + 1) * LANES] = (_rot(a, cq, saq, sbq, hq) * scale).astype(bf16)
    o += W_DSA
    for j in range(W_DSA // LANES):
        a = _dot(x, w_ref[:, o + j * LANES:o + (j + 1) * LANES])
        kt_ref[j * LANES:(j + 1) * LANES, :] = _rot(a, cq, saq, sbq, hq).T.astype(bf16)
    o += W_DSA
    v_ref[...] = _dot(x, w_ref[:, o:o + W_DSA]).astype(bf16)
    o += W_DSA
    for j in range(W_IDX // LANES):
        a = _dot(x, w_ref[:, o + j * LANES:o + (j + 1) * LANES])
        qi_ref[:, j * LANES:(j + 1) * LANES] = _rot(a, ci, sai, sbi, hi).astype(bf16)
    o += W_IDX
    m = _dot(x, w_ref[:, o:o + LANES])
    lane = lax.broadcasted_iota(i32, (tm, LANES), 1)
    is_ki = lane < IDX_DIM
    m = _rot(m, jnp.where(is_ki, ci, 1.0), jnp.where(is_ki, sai, 0.0), jnp.where(is_ki, sbi, 0.0), hi)
    misc_ref[...] = m
    kit_ref[...] = m.T[:IDX_DIM, :].astype(bf16)
    z = _dot(m.astype(bf16), gup_ref[...]) + gb_ref[...]
    la_ref[...] = (jnp.minimum(z, 0.0) - jnp.log1p(jnp.exp(-jnp.abs(z)))) / GLA_GATE_TEMP
    o += LANES
    gq_ref[...] = _dot(x, w_ref[:, o:o + W_GQK])
    o += W_GQK
    gk_ref[...] = _dot(x, w_ref[:, o:o + W_GQK])
    o += W_GQK
    gv_ref[...] = _dot(x, w_ref[:, o:o + W_GV]).astype(bf16)
    o += W_GV
    gr_ref[...] = _dot(x, w_ref[:, o:o + W_GV])


def _rot_tables(positions, head_dim):
    r = head_dim // ROPE_FRACTION
    half = r // 2
    inv_freq = ROPE_THETA ** (-jnp.arange(half, dtype=f32) / half)
    ang = positions.astype(f32)[..., None] * inv_freq
    cos, sin = jnp.cos(ang), jnp.sin(ang)
    lane = np.arange(LANES) % head_dim
    src = np.where(lane < half, lane, np.clip(lane - half, 0, half - 1))
    cos_l, sin_l = cos[..., src], sin[..., src]
    c = jnp.where(lane < r, cos_l, 1.0)
    sa = jnp.where(lane < half, -sin_l, 0.0)
    sb = jnp.where((lane >= half) & (lane < r), sin_l, 0.0)
    return c, sa, sb


def _proj(x, positions, w_in, gate_up, gate_bias, tm):
    B, S, D = x.shape
    splits = np.cumsum([W_DSA, W_DSA, W_DSA, W_IDX, IDX_DIM, IDX_HEADS, W_GQK, W_GQK, W_GV, GLA_GATE_RANK])
    (wq, wk, wv, wqi, wki, wwi, wgq, wgk, wgv, wlr, wgr) = jnp.split(w_in, splits.tolist(), axis=1)
    pad = jnp.zeros((D, LANES - IDX_DIM - IDX_HEADS - GLA_GATE_RANK), w_in.dtype)
    w_a = jnp.concatenate([wq, wk, wv, wqi, wki, wwi, wlr, pad, wgq, wgk, wgv, wgr], axis=1).astype(bf16)
    gup = jnp.zeros((LANES, W_GQK), f32).at[MISC_LR:MISC_LR + GLA_GATE_RANK].set(gate_up).astype(bf16)
    tabs = _rot_tables(positions, DSA_HEAD_DIM) + _rot_tables(positions, IDX_DIM)
    W = w_a.shape[1]
    tok = lambda n: pl.BlockSpec((None, tm, n), lambda b, j: (b, j, 0))
    full = lambda a: pl.BlockSpec(a.shape, lambda b, j: (0,) * a.ndim)
    tr = lambda n: pl.BlockSpec((None, n, tm), lambda b, j: (b, 0, j))
    sd = jax.ShapeDtypeStruct
    out_shape = [sd((B, S, W_DSA), bf16), sd((B, W_DSA, S), bf16), sd((B, S, W_DSA), bf16), sd((B, S, W_IDX), bf16),
                 sd((B, IDX_DIM, S), bf16), sd((B, S, LANES), f32), sd((B, S, W_GQK), f32), sd((B, S, W_GQK), f32),
                 sd((B, S, W_GV), bf16), sd((B, S, W_GQK), f32), sd((B, S, W_GV), f32)]
    out_specs = [tok(W_DSA), tr(W_DSA), tok(W_DSA), tok(W_IDX), tr(IDX_DIM), tok(LANES), tok(W_GQK), tok(W_GQK),
                 tok(W_GV), tok(W_GQK), tok(W_GV)]
    gb = gate_bias.reshape(1, W_GQK)
    return pl.pallas_call(
        _proj_kernel, grid=(B, S // tm), out_shape=out_shape, out_specs=out_specs,
        in_specs=[tok(D), full(w_a), full(gup), full(gb)] + [tok(LANES)] * 6,
        compiler_params=_params("parallel", "parallel"), name="proj",
    )(x, w_a, gup, gb, *tabs)


def _dsa_kernel(q_ref, kt_ref, v_ref, qi_ref, kit_ref, misc_ref, o_ref, key_ref, keyt_ref, bias_ref, lg_ref, mx_ref,
                ls_ref, acc_ref, *, n_sel, idx_bits, kc):
    qb, S = q_ref.shape[0], kt_ref.shape[1]
    i = pl.program_id(1)
    nk = lax.div((i + 1) * qb + (kc - 1), kc)
    assert qb == LANES
    qi = qi_ref[...]
    wi = misc_ref[:, MISC_WI:MISC_WI + IDX_HEADS] * (IDX_HEADS ** -0.5)
    qpos = lax.broadcasted_iota(i32, (qb, 1), 0) + i * qb
    lane = lax.broadcasted_iota(i32, (qb, kc), 1)

    def chunk(c):
        return pl.ds(pl.multiple_of(c * kc, kc), kc)

    def score_chunk(c, carry):
        kit = kit_ref[:, chunk(c)]
        score = jnp.zeros((qb, kc), f32)
        for h in range(IDX_HEADS):
            d = _dot(qi[:, h * IDX_DIM:(h + 1) * IDX_DIM], kit) * (IDX_DIM ** -0.5)
            score = score + wi[:, h:h + 1] * jnp.maximum(d, 0.0)
        score = jnp.where(score == 0.0, 0.0, score)
        bits = pltpu.bitcast(score, i32)
        key = bits ^ ((bits >> 31) & jnp.int32(0x7FFFFFFF))
        key = jnp.where(lane + c * kc <= qpos, key, INT_MIN)
        key_ref[:, chunk(c)] = key
        for j in range(kc // LANES):
            keyt_ref[pl.ds(pl.multiple_of(c * kc + j * LANES, LANES), LANES), :] = key[:, j * LANES:(j + 1) * LANES].T
        return carry

    lax.fori_loop(0, nk, score_chunk, 0)

    qpos_row = lax.broadcasted_iota(i32, (1, qb), 1) + i * qb
    target = jnp.minimum(n_sel, qpos_row + 1).astype(f32)
    key_row = lax.broadcasted_iota(i32, (kc, qb), 0)
    n_part = 4

    def count(pred):
        def body(c, accs):
            hit = jnp.where(pred(keyt_ref[chunk(c), :], key_row + c * kc), 1.0, 0.0)
            accs = list(accs)
            for j in range(kc // SUBLANES):
                accs[j % n_part] = accs[j % n_part] + hit[j * SUBLANES:(j + 1) * SUBLANES]
            return tuple(accs)

        accs = lax.fori_loop(0, nk, body, (jnp.zeros((SUBLANES, qb), f32),) * n_part)
        return jnp.sum((accs[0] + accs[1]) + (accs[2] + accs[3]), axis=0, keepdims=True)

    base = jnp.where(count(lambda k, r: k >= 0) >= target, jnp.int32(0), jnp.int32(INT_MIN))

    def tau_bit(t, base):
        cand = base | jnp.left_shift(jnp.int32(1), 30 - t)
        return jnp.where(count(lambda k, r: k >= cand) >= target, cand, base)

    tau_row = lax.fori_loop(0, 31, tau_bit, base)
    excess = jnp.max(count(lambda k, r: k >= tau_row) - target)

    def tie_limit():
        need = target - count(lambda k, r: k > tau_row)

        def idx_bit(t, m):
            cand = m | jnp.left_shift(jnp.int32(1), idx_bits - 1 - t)
            return jnp.where(count(lambda k, r: (k == tau_row) & (r < cand)) < need, cand, m)

        return lax.fori_loop(0, idx_bits, idx_bit, jnp.zeros((1, qb), i32))

    m_row = lax.cond(excess > 0.0, tie_limit, lambda: jnp.full((1, qb), S, i32))
    to_col = lambda row: jnp.broadcast_to(row, (qb, qb)).T[:, 0:1]
    tau, m_idx = to_col(tau_row), to_col(m_row)

    def bias_chunk(c, carry):
        k = key_ref[:, chunk(c)]
        sel = (k > tau) | ((k == tau) & (lane + c * kc <= m_idx))
        bias_ref[:, chunk(c)] = jnp.where(sel, 0.0, NEG_INF)
        return carry

    lax.fori_loop(0, nk, bias_chunk, 0)

    q = q_ref[...]
    pair = LANES // DSA_HEAD_DIM

    def fold(x, op, acc):
        for j in range(kc // LANES):
            acc = op(acc, x[:, j * LANES:(j + 1) * LANES])
        return acc

    heads = range(DSA_HEADS)
    head_rows = [slice(h * DSA_HEAD_DIM, (h + 1) * DSA_HEAD_DIM) for h in heads]
    mx_ref[...] = jnp.full(mx_ref.shape, NEG_INF, f32)
    ls_ref[...] = jnp.zeros_like(ls_ref)
    acc_ref[...] = jnp.zeros_like(acc_ref)

    def logits(c, carry):
        bias = bias_ref[:, chunk(c)]
        for h in heads:
            lg = _dot(q[:, head_rows[h]], kt_ref[head_rows[h], chunk(c)]) + bias
            lg_ref[h, :, chunk(c)] = lg
            mx_ref[h] = fold(lg, jnp.maximum, mx_ref[h])
        return carry

    lax.fori_loop(0, nk, logits, 0)
    ms = [jnp.max(mx_ref[h], axis=1, keepdims=True) for h in heads]

    def weigh(c, carry):
        for h in heads:
            p = jnp.exp(lg_ref[h, :, chunk(c)] - ms[h])
            ls_ref[h] = fold(p, jnp.add, ls_ref[h])
            slab = slice((h // pair) * LANES, (h // pair + 1) * LANES)
            acc_ref[h] = acc_ref[h] + _dot(p.astype(bf16), v_ref[chunk(c), slab])
        return carry

    lax.fori_loop(0, nk, weigh, 0)
    for h in heads:
        l = jnp.sum(ls_ref[h], axis=1, keepdims=True)
        off = (h % pair) * DSA_HEAD_DIM
        o_ref[:, head_rows[h]] = (acc_ref[h][:, off:off + DSA_HEAD_DIM] / l).astype(o_ref.dtype)


def _dsa(q, kt, v, qi, kit, misc, qb, kc):
    B, S, _ = q.shape
    n_sel = min(TOPK_MAX, S // 4)
    blk = lambda n: pl.BlockSpec((None, qb, n), lambda b, i: (b, i, 0))
    per_b = lambda r, c: pl.BlockSpec((None, r, c), lambda b, i: (b, 0, 0))
    kern = functools.partial(_dsa_kernel, n_sel=n_sel, idx_bits=max(1, (S - 1).bit_length()), kc=kc)
    return pl.pallas_call(
        kern, grid=(B, S // qb), out_shape=jax.ShapeDtypeStruct((B, S, W_DSA), bf16), out_specs=blk(W_DSA),
        in_specs=[blk(W_DSA), per_b(W_DSA, S), per_b(S, W_DSA), blk(W_IDX), per_b(IDX_DIM, S), blk(LANES)],
        scratch_shapes=[pltpu.VMEM((qb, S), i32), pltpu.VMEM((S, qb), i32), pltpu.VMEM((qb, S), f32),
                        pltpu.VMEM((DSA_HEADS, qb, S), f32)]
        + [pltpu.VMEM((DSA_HEADS, qb, LANES), f32)] * 3,
        compiler_params=_params("parallel", "arbitrary"), name="dsa",
    )(q, kt, v, qi, kit, misc)


def _gla_kernel(gq_ref, gk_ref, gv_ref, la_ref, gr_ref, ng_ref, o_ref, state_ref):
    ct = gq_ref.shape[0]
    nch = ct // GLA_CHUNK

    @pl.when(pl.program_id(1) == 0)
    def _():
        state_ref[...] = jnp.zeros_like(state_ref)

    la = la_ref[...]
    r = lax.broadcasted_iota(i32, (ct, ct), 0)
    c = lax.broadcasted_iota(i32, (ct, ct), 1)
    same = (r // GLA_CHUNK) == (c // GLA_CHUNK)
    causal = same & (c <= r)
    hp = lax.Precision.HIGHEST
    bcum = _dot(jnp.where(causal, 1.0, 0.0), la, precision=hp)
    blast = _dot(jnp.where(same, 1.0, 0.0), la, precision=hp)
    q_dec = (gq_ref[...] * (GLA_DK ** -0.5) * jnp.exp(bcum)).astype(bf16)
    k_inv = (gk_ref[...] * jnp.exp(-bcum)).astype(bf16)
    k_end = (gk_ref[...] * jnp.exp(blast - bcum)).astype(bf16)
    decay = jnp.exp(blast)
    ng = ng_ref[...]
    for h in range(GLA_HEADS):
        ks = slice(h * GLA_DK, (h + 1) * GLA_DK)
        vs = slice(h * GLA_DV, (h + 1) * GLA_DV)
        qd, ki, ke, vh = q_dec[:, ks], k_inv[:, ks], k_end[:, ks], gv_ref[:, vs]
        attn = jnp.where(causal, _dot(qd, ki, _NT), 0.0)
        o = _dot(attn.astype(bf16), vh)
        st = state_ref[h]
        inter = []
        for n in range(nch):
            rows = slice(n * GLA_CHUNK, (n + 1) * GLA_CHUNK)
            inter.append(_dot(qd[rows], st.astype(bf16), _NT))
            st = st * decay[n * GLA_CHUNK:n * GLA_CHUNK + 1, ks] + _dot(vh[rows], ke[rows], _TN)
        state_ref[h] = st
        o = o + jnp.concatenate(inter, axis=0)
        o = o * lax.rsqrt(jnp.mean(o * o, axis=-1, keepdims=True) + RMS_EPS) * ng
        g = gr_ref[:, vs]
        o_ref[:, vs] = (o * (g * jax.nn.sigmoid(g))).astype(o_ref.dtype)


def _gla(gq, gk, gv, la, gr, norm_g, ct):
    B, S, _ = gq.shape
    blk = lambda n: pl.BlockSpec((None, ct, n), lambda b, j: (b, j, 0))
    ng = norm_g.reshape(1, GLA_DV)
    return pl.pallas_call(
        _gla_kernel, grid=(B, S // ct), out_shape=jax.ShapeDtypeStruct((B, S, W_GV), bf16), out_specs=blk(W_GV),
        in_specs=[blk(W_GQK), blk(W_GQK), blk(W_GV), blk(W_GQK), blk(W_GV), pl.BlockSpec(ng.shape, lambda b, j: (0, 0))],
        scratch_shapes=[pltpu.VMEM((GLA_HEADS, GLA_DV, GLA_DK), f32)],
        compiler_params=_params("parallel", "arbitrary"), name="gla",
    )(gq, gk, gv, la, gr, ng)


def _mix_out_kernel(x_ref, ya_ref, yb_ref, wa_ref, wb_ref, g_ref, b_ref, o_ref, *, alpha):
    mix = _dot(ya_ref[...], wa_ref[...]) + _dot(yb_ref[...], wb_ref[...])
    o_ref[...] = _layer_norm(alpha * x_ref[...] + mix, g_ref[...], b_ref[...])


def _mix_out(x2, ya, yb, w_out, g, b, alpha, tm):
    T, D = x2.shape
    wa, wb = w_out[:W_DSA].astype(bf16), w_out[W_DSA:].astype(bf16)
    tok = lambda n: pl.BlockSpec((tm, n), lambda i: (i, 0))
    full = lambda a: pl.BlockSpec(a.shape, lambda i: (0, 0))
    g, b = g.reshape(1, D), b.reshape(1, D)
    return pl.pallas_call(
        functools.partial(_mix_out_kernel, alpha=alpha), grid=(T // tm,),
        out_shape=jax.ShapeDtypeStruct((T, D), f32), out_specs=tok(D),
        in_specs=[tok(D), tok(W_DSA), tok(W_GV), full(wa), full(wb), full(g), full(b)],
        compiler_params=_params("parallel"), name="mix_out",
    )(x2, ya, yb, wa, wb, g, b)


def _mem_kv_kernel(m_ref, wk_ref, wv_ref, k_ref, v_ref):
    m = m_ref[...].astype(bf16)
    k_ref[...] = _dot(m, wk_ref[...]).astype(bf16)
    v_ref[...] = _dot(m, wv_ref[...]).astype(bf16)


def _mem_kv(mem, w_k, w_v):
    B, M, D = mem.shape
    wk, wv = w_k.astype(bf16), w_v.astype(bf16)
    blk = pl.BlockSpec((None, M, D), lambda b: (b, 0, 0))
    full = pl.BlockSpec((D, D), lambda b: (0, 0))
    sd = jax.ShapeDtypeStruct((B, M, D), bf16)
    return pl.pallas_call(_mem_kv_kernel, grid=(B,), out_shape=[sd, sd], out_specs=[blk, blk],
                          in_specs=[blk, full, full], compiler_params=_params("parallel"), name="mem_kv")(mem, wk, wv)


def _xattn_kernel(h_ref, k_ref, v_ref, wq_ref, wo_ref, g_ref, b_ref, o_ref, *, alpha):
    h = h_ref[...]
    D = h.shape[1]
    hd = D // XATTN_HEADS
    q = (_dot(h.astype(bf16), wq_ref[...]) * (hd ** -0.5)).astype(bf16)
    outs = []
    for a in range(XATTN_HEADS):
        s = slice(a * hd, (a + 1) * hd)
        lg = _dot(q[:, s], k_ref[:, s], _NT)
        p = jnp.exp(lg - jnp.max(lg, axis=1, keepdims=True))
        l = jnp.sum(p, axis=1, keepdims=True)
        outs.append((_dot(p.astype(bf16), v_ref[:, s]) / l).astype(bf16))
    ca = _dot(jnp.concatenate(outs, axis=1), wo_ref[...])
    o_ref[...] = _layer_norm(alpha * h + ca, g_ref[...], b_ref[...])


def _xattn(h1, km, vm, w_q, w_o, g, b, alpha, tm):
    B, S, D = h1.shape
    M = km.shape[1]
    wq, wo = w_q.astype(bf16), w_o.astype(bf16)
    g, b = g.reshape(1, D), b.reshape(1, D)
    tok = pl.BlockSpec((None, tm, D), lambda bi, j: (bi, j, 0))
    per_b = pl.BlockSpec((None, M, D), lambda bi, j: (bi, 0, 0))
    full = lambda a: pl.BlockSpec(a.shape, lambda bi, j: (0, 0))
    return pl.pallas_call(
        functools.partial(_xattn_kernel, alpha=alpha), grid=(B, S // tm),
        out_shape=jax.ShapeDtypeStruct((B, S, D), f32), out_specs=tok,
        in_specs=[tok, per_b, per_b, full(wq), full(wo), full(g), full(b)],
        compiler_params=_params("parallel", "parallel"), name="xattn",
    )(h1, km, vm, wq, wo, g, b)


def _top_rows(s, n_top, ids=None, payload=None):
    if ids is None:
        ids = lax.broadcasted_iota(i32, s.shape, 0)
    vals, picks = [], []
    for _ in range(n_top):
        m = jnp.max(s, axis=0, keepdims=True)
        am = jnp.min(jnp.where(s == m, ids, jnp.int32(2 ** 30)), axis=0, keepdims=True)
        hit = ids == am
        vals.append(m)
        picks.append(am if payload is None else jnp.max(jnp.where(hit, payload, -1), axis=0, keepdims=True))
        s = jnp.where(hit, NEG_INF, s)
    return jnp.concatenate(vals, axis=0), jnp.concatenate(picks, axis=0)


def _pair_candidates(v1, i1, v2, i2):
    n = PEER_TOPK
    sub = lax.broadcasted_iota(i32, (SUBLANES,) + v1.shape[1:], 0)
    blocks = [(v1[0:1] + v2, lax.broadcasted_iota(i32, v2.shape, 0), i1[0:1] * PEER_N_KEYS + i2)]
    for a in range(1, 5):
        blocks.append((v1[a:a + 1] + v2[:SUBLANES], a * n + sub, i1[a:a + 1] * PEER_N_KEYS + i2[:SUBLANES]))
    pick = lambda x: jnp.where(sub < 2, x[5:6], jnp.where(sub < 4, x[6:7], x[7:8]))
    alt = lambda x: jnp.where((sub & 1) == 0, x[0:1], x[1:2])
    a_of = jnp.where(sub < 2, 5, jnp.where(sub < 4, 6, 7))
    blocks.append((jnp.where(sub < 6, pick(v1) + alt(v2), NEG_INF), a_of * n + (sub & 1),
                   pick(i1) * PEER_N_KEYS + alt(i2)))
    blocks.append((v1[SUBLANES:] + v2[0:1], (sub + SUBLANES) * n, i1[SUBLANES:] * PEER_N_KEYS + i2[0:1]))
    return tuple(jnp.concatenate(parts, axis=0) for parts in zip(*blocks))


def _route_kernel(h_ref, wq_ref, k1_ref, k2_ref, row_ref, ne_ref, g_ref):
    q = _dot(h_ref[...].astype(bf16), wq_ref[...])
    dk = PEER_D_KEY // 2
    k1, k2 = k1_ref[...], k2_ref[...]
    for a in range(PEER_HEADS):
        qa = q[:, a * PEER_D_KEY:a * PEER_D_KEY + dk].astype(bf16)
        qb = q[:, a * PEER_D_KEY + dk:(a + 1) * PEER_D_KEY].astype(bf16)
        v1, i1 = _top_rows(_dot(k1, qa, _NT), PEER_TOPK)
        v2, i2 = _top_rows(_dot(k2, qb, _NT), PEER_TOPK)
        cand, ids, cidx = _pair_candidates(v1, i1, v2, i2)
        top, experts = _top_rows(cand, PEER_TOPK, ids=ids, payload=cidx)
        p = jnp.exp(top - top[0:1])
        gates = p / jnp.sum(p, axis=0, keepdims=True)
        for half in range(PEER_TOPK // SUBLANES):
            grp = slice(half * SUBLANES, (half + 1) * SUBLANES)
            e, g = experts[grp], gates[grp]
            odd = e & 1
            n_even = SUBLANES - jnp.sum(odd, axis=0, keepdims=True)
            sub = lax.broadcasted_iota(i32, e.shape, 0)
            evens_before = jnp.zeros_like(n_even)
            e_sorted, g_sorted = jnp.zeros_like(e), jnp.zeros_like(g)
            for r in range(SUBLANES):
                odd_r = odd[r:r + 1]
                dest = jnp.where(odd_r == 1, n_even + (r - evens_before), evens_before)
                hit = sub == dest
                e_sorted = jnp.where(hit, e[r:r + 1], e_sorted)
                g_sorted = jnp.where(hit, g[r:r + 1], g_sorted)
                evens_before = evens_before + (1 - odd_r)
            rows = slice(a * PEER_TOPK + half * SUBLANES, a * PEER_TOPK + (half + 1) * SUBLANES)
            row_ref[rows, :] = (e_sorted >> 1) * SUBLANES
            g_ref[rows, :] = g_sorted
            n = a * (PEER_TOPK // SUBLANES) + half
            ne_ref[n:n + 1, :] = n_even


def _route(h2, w_query, k1, k2, tm):
    T, D = h2.shape
    wq = w_query.astype(bf16)
    k1, k2 = k1.astype(bf16), k2.astype(bf16)
    hk = PEER_HEADS * PEER_TOPK
    full = lambda a: pl.BlockSpec(a.shape, lambda i: (0, 0))
    out = pl.BlockSpec((hk, tm), lambda i: (0, i))
    n_groups = hk // SUBLANES
    sd = jax.ShapeDtypeStruct
    return pl.pallas_call(
        _route_kernel, grid=(T // tm,),
        out_shape=[sd((hk, T), i32), sd((n_groups, T), i32), sd((hk, T), f32)],
        out_specs=[out, pl.BlockSpec((n_groups, tm), lambda i: (0, i)), out],
        in_specs=[pl.BlockSpec((tm, D), lambda i: (i, 0)), full(wq), full(k1), full(k2)],
        compiler_params=_params("parallel"), name="route",
    )(h2, wq, k1, k2)


def _pack_table(tab):
    n, d = tab.shape
    assert d == SUBLANES * LANES
    u = lax.bitcast_convert_type(tab.astype(bf16), jnp.uint16).astype(jnp.uint32).reshape(n // 2, 2, SUBLANES, LANES)
    return lax.bitcast_convert_type((u[:, 0] << 16) | u[:, 1], i32).reshape(n // 2 * SUBLANES, LANES)


def _shift_patterns():
    n_even = np.arange(SUBLANES + 1)[:, None, None]
    p = np.arange(SUBLANES)[None, :, None]
    pat = np.where(p < n_even, 0, 16) + np.zeros((1, 1, LANES), np.int64)
    return jnp.asarray(pat.reshape(-1, LANES), i32)


def _unpack(w, sh_ref, n_even, p):
    shift = jnp.broadcast_to(sh_ref[pl.ds(n_even * SUBLANES + p, 1), :], w.shape)
    return pltpu.bitcast(jnp.left_shift(w, shift) & jnp.int32(-65536), f32)


def _row(tab_ref, row0, sh_ref, n_even, p):
    return _unpack(tab_ref[pl.ds(pl.multiple_of(row0, SUBLANES), SUBLANES), :], sh_ref, n_even, p)


TOKEN_BATCH = 16
GROUP_UNROLL = 4


_TREE_ORDER = (0, 4, 2, 6, 1, 5, 3, 7)


def _packed(w):
    return pltpu.bitcast(w, bf16)


def _merge(p, q, mask, shift):
    moved = pltpu.roll(jnp.where(mask, q, p), shift, 0)
    return pltpu.bitcast(_packed(jnp.where(mask, p, q)) + _packed(moved), i32)


def _sublane_sums(ps):
    sub = lax.broadcasted_iota(i32, ps[0].shape, 0)
    quad = lambda a, b, o: _merge(a, b, ((sub - o) & 7) < 4, 4)
    duo = lambda a, b, o: _merge(a, b, ((sub - o) & 3) < 2, 6)
    r1 = duo(quad(ps[0], ps[1], 0), quad(ps[2], ps[3], 2), 0)
    r2 = duo(quad(ps[4], ps[5], 1), quad(ps[6], ps[7], 3), 1)
    return _merge(r1, r2, (sub & 1) == 0, 7)


def _peer_down_kernel(*refs):
    row_refs, (ne_ref, x_ref, gt_ref, sh_ref, tab_ref, ct_ref, part_ref, actt_ref) = refs[:SUBLANES], refs[SUBLANES:]
    hk, tb = ct_ref.shape
    n_groups = hk // SUBLANES
    lane = lax.broadcasted_iota(i32, (hk, tb), 1)

    def batch(b, carry):
        t0 = b * TOKEN_BATCH

        def token(u, c1):
            t = t0 + u
            xh = pltpu.bitcast(x_ref[t].astype(bf16).astype(f32), i32)
            xw = _packed(xh | lax.shift_right_logical(xh, 16))

            def group(g, c2):
                k0 = pl.multiple_of(g * SUBLANES, SUBLANES)
                tg = t * n_groups + g
                ps = []
                for k in _TREE_ORDER:
                    w = tab_ref[pl.ds(pl.multiple_of(row_refs[k][tg], SUBLANES), SUBLANES), :]
                    ps.append(pltpu.bitcast(_packed(w) * xw, i32))
                sums = _sublane_sums(ps)
                shift = sh_ref[pl.ds(pl.multiple_of(ne_ref[tg] * SUBLANES, SUBLANES), SUBLANES), :]
                part_ref[u, pl.ds(k0, SUBLANES), :] = pltpu.bitcast(jnp.left_shift(sums, shift) & jnp.int32(-65536), f32)
                return c2

            lax.fori_loop(0, n_groups, group, 0, unroll=4 * GROUP_UNROLL)
            return c1

        lax.fori_loop(0, TOKEN_BATCH, token, 0)
        a = actt_ref[...]
        for u in range(TOKEN_BATCH):
            a = jnp.where(lane == t0 + u, jnp.sum(part_ref[u], axis=1, keepdims=True), a)
        actt_ref[...] = a
        return carry

    lax.fori_loop(0, tb // TOKEN_BATCH, batch, 0)
    a = actt_ref[...]
    gelu = 0.5 * a * (1.0 + lax.erf(a * (2.0 ** -0.5)))
    ct_ref[...] = gt_ref[...] * gelu


def _peer_specs(tb, hk):
    per_group = pl.BlockSpec((tb * hk // SUBLANES,), lambda i: (i,), memory_space=pltpu.SMEM)
    whole = pl.BlockSpec(memory_space=pltpu.VMEM)
    return per_group, whole


def _peer_down(rows8, n_even, x3, gates_t, tab, tb):
    hk, T = gates_t.shape
    per_group, whole = _peer_specs(tb, hk)
    per_k = pl.BlockSpec((hk, tb), lambda i: (0, i))
    return pl.pallas_call(
        _peer_down_kernel, grid=(T // tb,), out_shape=jax.ShapeDtypeStruct((hk, T), f32), out_specs=per_k,
        in_specs=[per_group] * (SUBLANES + 1) + [pl.BlockSpec((tb,) + x3.shape[1:], lambda i: (i, 0, 0)), per_k,
                                                 whole, whole],
        scratch_shapes=[pltpu.VMEM((TOKEN_BATCH, hk, LANES), f32), pltpu.VMEM((hk, tb), f32)],
        compiler_params=_params("arbitrary"), name="peer_down",
    )(*rows8, n_even, x3, gates_t, _shift_patterns(), tab)


def _peer_up_kernel(*refs):
    row_refs, (ne_ref, ct_ref, sh_ref, tab_ref, o_ref, cx_ref) = refs[:SUBLANES], refs[SUBLANES:]
    hk, tb = ct_ref.shape
    n_groups = hk // SUBLANES
    n_acc = 4
    vreg = o_ref.shape[1:]
    lane = lax.broadcasted_iota(i32, (hk, tb), 1)

    def batch(b, carry):
        t0 = b * TOKEN_BATCH
        ct = ct_ref[...]
        for u in range(TOKEN_BATCH):
            col = jnp.sum(jnp.where(lane == t0 + u, ct, 0.0), axis=1, keepdims=True)
            cx_ref[u] = jnp.broadcast_to(col, (hk, LANES))

        def token(u, c1):
            t = t0 + u

            def fetch(g):
                tg = t * n_groups + g
                return tuple(tab_ref[pl.ds(pl.multiple_of(r[tg], SUBLANES), SUBLANES), :] for r in row_refs)

            def group(g, accs):
                accs, packed = list(accs), fetch(g)
                k0 = pl.multiple_of(g * SUBLANES, SUBLANES)
                n_even = ne_ref[t * n_groups + g]
                for p in range(SUBLANES):
                    coef = jnp.broadcast_to(cx_ref[u, pl.ds(k0 + p, 1), :], vreg)
                    accs[p % n_acc] = accs[p % n_acc] + _unpack(packed[p], sh_ref, n_even, p) * coef
                return tuple(accs)

            accs = lax.fori_loop(0, n_groups, group, (jnp.zeros(vreg, f32),) * n_acc, unroll=4 * GROUP_UNROLL)
            o_ref[t] = (accs[0] + accs[1]) + (accs[2] + accs[3])
            return c1

        lax.fori_loop(0, TOKEN_BATCH, token, 0)
        return carry

    lax.fori_loop(0, tb // TOKEN_BATCH, batch, 0)


def _peer_up(rows8, n_even, coef_t, tab, tb):
    hk, T = coef_t.shape
    per_group, whole = _peer_specs(tb, hk)
    return pl.pallas_call(
        _peer_up_kernel, grid=(T // tb,), out_shape=jax.ShapeDtypeStruct((T, SUBLANES, LANES), f32),
        out_specs=pl.BlockSpec((tb, SUBLANES, LANES), lambda i: (i, 0, 0)),
        in_specs=[per_group] * (SUBLANES + 1) + [pl.BlockSpec((hk, tb), lambda i: (0, i)), whole, whole],
        scratch_shapes=[pltpu.VMEM((TOKEN_BATCH, hk, LANES), f32)],
        compiler_params=_params("arbitrary"), name="peer_up",
    )(*rows8, n_even, coef_t, _shift_patterns(), tab)


def _ffn_out_kernel(h_ref, f_ref, g_ref, b_ref, o_ref, *, alpha):
    o_ref[...] = _layer_norm(alpha * h_ref[...] + f_ref[...], g_ref[...], b_ref[...])


def _ffn_out(h2, ff, g, b, alpha, tm):
    T, D = h2.shape
    g, b = g.reshape(1, D), b.reshape(1, D)
    tok = pl.BlockSpec((tm, D), lambda i: (i, 0))
    full = pl.BlockSpec((1, D), lambda i: (0, 0))
    return pl.pallas_call(
        functools.partial(_ffn_out_kernel, alpha=alpha), grid=(T // tm,),
        out_shape=jax.ShapeDtypeStruct((T, D), f32), out_specs=tok, in_specs=[tok, tok, full, full],
        compiler_params=_params("parallel"), name="ffn_out",
    )(h2, ff, g, b)


def _tile(n, want):
    t = min(n, want)
    assert n % t == 0, (n, t)
    return t


def kernel(x, positions, mem, w_in, gla_gate_up, gla_gate_bias, gla_norm_g, w_out, ln_mix_g, ln_mix_b, xattn_w_q, xattn_w_k, xattn_w_v, xattn_w_o, ln_mem_g, ln_mem_b, peer_w_query, peer_sub_keys_1, peer_sub_keys_2, peer_expert_down, peer_expert_up, ln_ffn_g, ln_ffn_b):
    B, S, D = x.shape
    T = B * S
    depth = w_in.shape[0]
    alpha = (2.0 * depth) ** 0.25
    tm = _tile(S, 512)
    h = x
    for l in range(depth):
        q, kt, v, qi, kit, misc, gq, gk, gv, la, gr = _proj(h, positions, w_in[l], gla_gate_up[l], gla_gate_bias[l], tm)
        y_dsa = _dsa(q, kt, v, qi, kit, misc, _tile(S, 128), _tile(S, 512))
        y_gla = _gla(gq, gk, gv, la, gr, gla_norm_g[l], tm)
        h1 = _mix_out(h.reshape(T, D), y_dsa.reshape(T, W_DSA), y_gla.reshape(T, W_GV), w_out[l],
                      ln_mix_g[l], ln_mix_b[l], alpha, tm)
        km, vm = _mem_kv(mem, xattn_w_k[l], xattn_w_v[l])
        h2 = _xattn(h1.reshape(B, S, D), km, vm, xattn_w_q[l], xattn_w_o[l], ln_mem_g[l], ln_mem_b[l], alpha, tm)
        h2 = h2.reshape(T, D)
        rows_t, ne_t, gates_t = _route(h2, peer_w_query[l], peer_sub_keys_1[l], peer_sub_keys_2[l], _tile(T, 256))
        flat = lambda a: a.T.reshape(-1)
        n_even = flat(ne_t)
        rows8 = [flat(rows_t[p::SUBLANES]) for p in range(SUBLANES)]
        tb = _tile(T, LANES)
        coef_t = _peer_down(rows8, n_even, h2.reshape(T, SUBLANES, LANES), gates_t, _pack_table(peer_expert_down[l]), tb)
        ff = _peer_up(rows8, n_even, coef_t, _pack_table(peer_expert_up[l]), tb)
        h = _ffn_out(h2, ff.reshape(T, D), ln_ffn_g[l], ln_ffn_b[l], alpha, tm).reshape(B, S, D)
    return h
```

```python
import functools

import jax
import jax.numpy as jnp
import numpy as np
from jax import lax
from jax.experimental import pallas as pl
from jax.experimental.pallas import tpu as pltpu

f32 = jnp.float32
bf16 = jnp.bfloat16
i32 = jnp.int32

DSA_HEADS = 8
DSA_HEAD_DIM = 64
IDX_HEADS = 8
IDX_DIM = 32
TOPK_MAX = 256
GLA_HEADS = 4
GLA_DK = 64
GLA_DV = 128
GLA_GATE_RANK = 16
GLA_GATE_TEMP = 16.0
GLA_CHUNK = 64
ROPE_THETA = 500000.0
ROPE_FRACTION = 4
XATTN_HEADS = 4
PEER_N_KEYS = 128
PEER_HEADS = 8
PEER_D_KEY = 256
PEER_TOPK = 16
LN_EPS = 1e-5
RMS_EPS = 1e-6

LANES = 128
SUBLANES = 8
VMEM_LIMIT = 56 * 1024 * 1024

INT_MIN = -(2 ** 31)
NEG_INF = float("-inf")

W_DSA = DSA_HEADS * DSA_HEAD_DIM
W_IDX = IDX_HEADS * IDX_DIM
W_GQK = GLA_HEADS * GLA_DK
W_GV = GLA_HEADS * GLA_DV
MISC_KI = 0
MISC_WI = IDX_DIM
MISC_LR = IDX_DIM + IDX_HEADS


def _dot(a, b, dims=(((1,), (0,)), ((), ())), precision=None):
    return lax.dot_general(a, b, dims, precision=precision, preferred_element_type=f32)


_NN = (((1,), (0,)), ((), ()))
_NT = (((1,), (1,)), ((), ()))
_TN = (((0,), (0,)), ((), ()))


def _params(*sem):
    return pltpu.CompilerParams(dimension_semantics=sem, vmem_limit_bytes=VMEM_LIMIT)


def _layer_norm(y, g, b):
    mu = jnp.mean(y, axis=-1, keepdims=True)
    yc = y - mu
    var = jnp.mean(yc * yc, axis=-1, keepdims=True)
    return yc * lax.rsqrt(var + LN_EPS) * g + b


def _rot(xb, c, sa, sb, half):
    return xb * c + pltpu.roll(xb, LANES - half, 1) * sa + pltpu.roll(xb, half, 1) * sb


def _proj_kernel(x_ref, w_ref, gup_ref, gb_ref, cq_ref, saq_ref, sbq_ref, ci_ref, sai_ref, sbi_ref,
                 q_ref, kt_ref, v_ref, qi_ref, kit_ref, misc_ref, gq_ref, gk_ref, gv_ref, la_ref, gr_ref):
    x = x_ref[...].astype(bf16)
    tm = x.shape[0]
    cq, saq, sbq = cq_ref[...], saq_ref[...], sbq_ref[...]
    ci, sai, sbi = ci_ref[...], sai_ref[...], sbi_ref[...]
    hq = DSA_HEAD_DIM // ROPE_FRACTION // 2
    hi = IDX_DIM // ROPE_FRACTION // 2
    o = 0
    scale = DSA_HEAD_DIM ** -0.5
    for j in range(W_DSA // LANES):
        a = _dot(x, w_ref[:, o + j * LANES:o + (j + 1) * LANES])
        q_ref[:, j * LANES:(j + 1) * LANES] = (_rot(a, cq, saq, sbq, hq) * scale).astype(bf16)
    o += W_DSA
    for j in range(W_DSA // LANES):
        a = _dot(x, w_ref[:, o + j * LANES:o + (j + 1) * LANES])
        kt_ref[j * LANES:(j + 1) * LANES, :] = _rot(a, cq, saq, sbq, hq).T.astype(bf16)
    o += W_DSA
    v_ref[...] = _dot(x, w_ref[:, o:o + W_DSA]).astype(bf16)
    o += W_DSA
    for j in range(W_IDX // LANES):
        a = _dot(x, w_ref[:, o + j * LANES:o + (j + 1) * LANES])
        qi_ref[:, j * LANES:(j + 1) * LANES] = _rot(a, ci, sai, sbi, hi).astype(bf16)
    o += W_IDX
    m = _dot(x, w_ref[:, o:o + LANES])
    lane = lax.broadcasted_iota(i32, (tm, LANES), 1)
    is_ki = lane < IDX_DIM
    m = _rot(m, jnp.where(is_ki, ci, 1.0), jnp.where(is_ki, sai, 0.0), jnp.where(is_ki, sbi, 0.0), hi)
    misc_ref[...] = m
    kit_ref[...] = m.T[:IDX_DIM, :].astype(bf16)
    z = _dot(m.astype(bf16), gup_ref[...]) + gb_ref[...]
    la_ref[...] = (jnp.minimum(z, 0.0) - jnp.log1p(jnp.exp(-jnp.abs(z)))) / GLA_GATE_TEMP
    o += LANES
    gq_ref[...] = _dot(x, w_ref[:, o:o + W_GQK])
    o += W_GQK
    gk_ref[...] = _dot(x, w_ref[:, o:o + W_GQK])
    o += W_GQK
    gv_ref[...] = _dot(x, w_ref[:, o:o + W_GV]).astype(bf16)
    o += W_GV
    gr_ref[...] = _dot(x, w_ref[:, o:o + W_GV])


def _rot_tables(positions, head_dim):
    r = head_dim // ROPE_FRACTION
    half = r // 2
    inv_freq = ROPE_THETA ** (-jnp.arange(half, dtype=f32) / half)
    ang = positions.astype(f32)[..., None] * inv_freq
    cos, sin = jnp.cos(ang), jnp.sin(ang)
    lane = np.arange(LANES) % head_dim
    src = np.where(lane < half, lane, np.clip(lane - half, 0, half - 1))
    cos_l, sin_l = cos[..., src], sin[..., src]
    c = jnp.where(lane < r, cos_l, 1.0)
    sa = jnp.where(lane < half, -sin_l, 0.0)
    sb = jnp.where((lane >= half) & (lane < r), sin_l, 0.0)
    return c, sa, sb


def _proj(x, positions, w_in, gate_up, gate_bias, tm):
    B, S, D = x.shape
    splits = np.cumsum([W_DSA, W_DSA, W_DSA, W_IDX, IDX_DIM, IDX_HEADS, W_GQK, W_GQK, W_GV, GLA_GATE_RANK])
    (wq, wk, wv, wqi, wki, wwi, wgq, wgk, wgv, wlr, wgr) = jnp.split(w_in, splits.tolist(), axis=1)
    pad = jnp.zeros((D, LANES - IDX_DIM - IDX_HEADS - GLA_GATE_RANK), w_in.dtype)
    w_a = jnp.concatenate([wq, wk, wv, wqi, wki, wwi, wlr, pad, wgq, wgk, wgv, wgr], axis=1).astype(bf16)
    gup = jnp.zeros((LANES, W_GQK), f32).at[MISC_LR:MISC_LR + GLA_GATE_RANK].set(gate_up).astype(bf16)
    tabs = _rot_tables(positions, DSA_HEAD_DIM) + _rot_tables(positions, IDX_DIM)
    W = w_a.shape[1]
    tok = lambda n: pl.BlockSpec((None, tm, n), lambda b, j: (b, j, 0))
    full = lambda a: pl.BlockSpec(a.shape, lambda b, j: (0,) * a.ndim)
    tr = lambda n: pl.BlockSpec((None, n, tm), lambda b, j: (b, 0, j))
    sd = jax.ShapeDtypeStruct
    out_shape = [sd((B, S, W_DSA), bf16), sd((B, W_DSA, S), bf16), sd((B, S, W_DSA), bf16), sd((B, S, W_IDX), bf16),
                 sd((B, IDX_DIM, S), bf16), sd((B, S, LANES), f32), sd((B, S, W_GQK), f32), sd((B, S, W_GQK), f32),
                 sd((B, S, W_GV), bf16), sd((B, S, W_GQK), f32), sd((B, S, W_GV), f32)]
    out_specs = [tok(W_DSA), tr(W_DSA), tok(W_DSA), tok(W_IDX), tr(IDX_DIM), tok(LANES), tok(W_GQK), tok(W_GQK),
                 tok(W_GV), tok(W_GQK), tok(W_GV)]
    gb = gate_bias.reshape(1, W_GQK)
    return pl.pallas_call(
        _proj_kernel, grid=(B, S // tm), out_shape=out_shape, out_specs=out_specs,
        in_specs=[tok(D), full(w_a), full(gup), full(gb)] + [tok(LANES)] * 6,
        compiler_params=_params("parallel", "parallel"), name="proj",
    )(x, w_a, gup, gb, *tabs)


def _dsa_kernel(q_ref, kt_ref, v_ref, qi_ref, kit_ref, misc_ref, o_ref, key_ref, keyt_ref, bias_ref, lg_ref, mx_ref,
                ls_ref, acc_ref, *, n_sel, idx_bits, kc):
    qb, S = q_ref.shape[0], kt_ref.shape[1]
    i = pl.program_id(1)
    nk = lax.div((i + 1) * qb + (kc - 1), kc)
    assert qb == LANES
    qi = qi_ref[...]
    wi = misc_ref[:, MISC_WI:MISC_WI + IDX_HEADS] * (IDX_HEADS ** -0.5) * (IDX_DIM ** -0.5)
    qpos = lax.broadcasted_iota(i32, (qb, 1), 0) + i * qb
    lane = lax.broadcasted_iota(i32, (qb, kc), 1)

    def chunk(c):
        return pl.ds(pl.multiple_of(c * kc, kc), kc)

    def score_chunk(c, carry):
        kit = kit_ref[:, chunk(c)]
        score = jnp.zeros((qb, kc), f32)
        for h in range(IDX_HEADS):
            d = _dot(qi[:, h * IDX_DIM:(h + 1) * IDX_DIM], kit)
            score = score + wi[:, h:h + 1] * jnp.maximum(d, 0.0)
        score = jnp.where(score == 0.0, 0.0, score)
        bits = pltpu.bitcast(score, i32)
        key = bits ^ ((bits >> 31) & jnp.int32(0x7FFFFFFF))
        key = jnp.where(lane + c * kc <= qpos, key, INT_MIN)
        key_ref[:, chunk(c)] = key
        for j in range(kc // LANES):
            keyt_ref[pl.ds(pl.multiple_of(c * kc + j * LANES, LANES), LANES), :] = key[:, j * LANES:(j + 1) * LANES].T
        return carry

    lax.fori_loop(0, nk, score_chunk, 0)

    qpos_row = lax.broadcasted_iota(i32, (1, qb), 1) + i * qb
    target = jnp.minimum(n_sel, qpos_row + 1).astype(f32)
    key_row = lax.broadcasted_iota(i32, (kc, qb), 0)
    n_part = 4

    def count(pred):
        def body(c, accs):
            hit = jnp.where(pred(keyt_ref[chunk(c), :], key_row + c * kc), 1.0, 0.0)
            accs = list(accs)
            for j in range(kc // SUBLANES):
                accs[j % n_part] = accs[j % n_part] + hit[j * SUBLANES:(j + 1) * SUBLANES]
            return tuple(accs)

        accs = lax.fori_loop(0, nk, body, (jnp.zeros((SUBLANES, qb), f32),) * n_part)
        return jnp.sum((accs[0] + accs[1]) + (accs[2] + accs[3]), axis=0, keepdims=True)

    base = jnp.where(count(lambda k, r: k >= 0) >= target, jnp.int32(0), jnp.int32(INT_MIN))

    def tau_bit(t, base):
        cand = base | jnp.left_shift(jnp.int32(1), 30 - t)
        return jnp.where(count(lambda k, r: k >= cand) >= target, cand, base)

    tau_row = lax.fori_loop(0, 31, tau_bit, base)
    excess = jnp.max(count(lambda k, r: k >= tau_row) - target)

    def tie_limit():
        need = target - count(lambda k, r: k > tau_row)

        def idx_bit(t, m):
            cand = m | jnp.left_shift(jnp.int32(1), idx_bits - 1 - t)
            return jnp.where(count(lambda k, r: (k == tau_row) & (r < cand)) < need, cand, m)

        return lax.fori_loop(0, idx_bits, idx_bit, jnp.zeros((1, qb), i32))

    m_row = lax.cond(excess > 0.0, tie_limit, lambda: jnp.full((1, qb), S, i32))
    to_col = lambda row: jnp.broadcast_to(row, (qb, qb)).T[:, 0:1]
    tau, m_idx = to_col(tau_row), to_col(m_row)

    def bias_chunk(c, carry):
        k = key_ref[:, chunk(c)]
        sel = (k > tau) | ((k == tau) & (lane + c * kc <= m_idx))
        bias_ref[:, chunk(c)] = jnp.where(sel, 0.0, NEG_INF)
        return carry

    lax.fori_loop(0, nk, bias_chunk, 0)

    q = q_ref[...]
    pair = LANES // DSA_HEAD_DIM

    def fold(x, op, acc):
        for j in range(kc // LANES):
            acc = op(acc, x[:, j * LANES:(j + 1) * LANES])
        return acc

    heads = range(DSA_HEADS)
    head_rows = [slice(h * DSA_HEAD_DIM, (h + 1) * DSA_HEAD_DIM) for h in heads]
    mx_ref[...] = jnp.full(mx_ref.shape, NEG_INF, f32)
    ls_ref[...] = jnp.zeros_like(ls_ref)
    acc_ref[...] = jnp.zeros_like(acc_ref)

    def logits(c, carry):
        bias = bias_ref[:, chunk(c)]
        for h in heads:
            lg = _dot(q[:, head_rows[h]], kt_ref[head_rows[h], chunk(c)]) + bias
            lg_ref[h, :, chunk(c)] = lg
            mx_ref[h] = fold(lg, jnp.maximum, mx_ref[h])
        return carry

    lax.fori_loop(0, nk, logits, 0)
    ms = [jnp.max(mx_ref[h], axis=1, keepdims=True) for h in heads]

    def weigh(c, carry):
        for h in heads:
            p = jnp.exp(lg_ref[h, :, chunk(c)] - ms[h])
            ls_ref[h] = fold(p, jnp.add, ls_ref[h])
            slab = slice((h // pair) * LANES, (h // pair + 1) * LANES)
            acc_ref[h] = acc_ref[h] + _dot(p.astype(bf16), v_ref[chunk(c), slab])
        return carry

    lax.fori_loop(0, nk, weigh, 0)
    for h in heads:
        l = jnp.sum(ls_ref[h], axis=1, keepdims=True)
        off = (h % pair) * DSA_HEAD_DIM
        o_ref[:, head_rows[h]] = (acc_ref[h][:, off:off + DSA_HEAD_DIM] / l).astype(o_ref.dtype)


def _dsa(q, kt, v, qi, kit, misc, qb, kc):
    B, S, _ = q.shape
    n_sel = min(TOPK_MAX, S // 4)
    blk = lambda n: pl.BlockSpec((None, qb, n), lambda b, i: (b, i, 0))
    per_b = lambda r, c: pl.BlockSpec((None, r, c), lambda b, i: (b, 0, 0))
    kern = functools.partial(_dsa_kernel, n_sel=n_sel, idx_bits=max(1, (S - 1).bit_length()), kc=kc)
    return pl.pallas_call(
        kern, grid=(B, S // qb), out_shape=jax.ShapeDtypeStruct((B, S, W_DSA), bf16), out_specs=blk(W_DSA),
        in_specs=[blk(W_DSA), per_b(W_DSA, S), per_b(S, W_DSA), blk(W_IDX), per_b(IDX_DIM, S), blk(LANES)],
        scratch_shapes=[pltpu.VMEM((qb, S), i32), pltpu.VMEM((S, qb), i32), pltpu.VMEM((qb, S), f32),
                        pltpu.VMEM((DSA_HEADS, qb, S), f32)]
        + [pltpu.VMEM((DSA_HEADS, qb, LANES), f32)] * 3,
        compiler_params=_params("parallel", "arbitrary"), name="dsa",
    )(q, kt, v, qi, kit, misc)


def _gla_kernel(gq_ref, gk_ref, gv_ref, la_ref, gr_ref, ng_ref, o_ref, state_ref):
    ct = gq_ref.shape[0]
    nch = ct // GLA_CHUNK

    @pl.when(pl.program_id(1) == 0)
    def _():
        state_ref[...] = jnp.zeros_like(state_ref)

    la = la_ref[...]
    r = lax.broadcasted_iota(i32, (ct, ct), 0)
    c = lax.broadcasted_iota(i32, (ct, ct), 1)
    same = (r // GLA_CHUNK) == (c // GLA_CHUNK)
    causal = same & (c <= r)
    hp = lax.Precision.HIGHEST
    bcum = _dot(jnp.where(causal, 1.0, 0.0), la, precision=hp)
    blast = _dot(jnp.where(same, 1.0, 0.0), la, precision=hp)
    q_dec = (gq_ref[...] * (GLA_DK ** -0.5) * jnp.exp(bcum)).astype(bf16)
    k_inv = (gk_ref[...] * jnp.exp(-bcum)).astype(bf16)
    k_end = (gk_ref[...] * jnp.exp(blast - bcum)).astype(bf16)
    decay = jnp.exp(blast)
    ng = ng_ref[...]
    for h in range(GLA_HEADS):
        ks = slice(h * GLA_DK, (h + 1) * GLA_DK)
        vs = slice(h * GLA_DV, (h + 1) * GLA_DV)
        qd, ki, ke, vh = q_dec[:, ks], k_inv[:, ks], k_end[:, ks], gv_ref[:, vs]
        attn = jnp.where(causal, _dot(qd, ki, _NT), 0.0)
        o = _dot(attn.astype(bf16), vh)
        st = state_ref[h]
        inter = []
        for n in range(nch):
            rows = slice(n * GLA_CHUNK, (n + 1) * GLA_CHUNK)
            inter.append(_dot(qd[rows], st.astype(bf16), _NT))
            st = st * decay[n * GLA_CHUNK:n * GLA_CHUNK + 1, ks] + _dot(vh[rows], ke[rows], _TN)
        state_ref[h] = st
        o = o + jnp.concatenate(inter, axis=0)
        o = o * lax.rsqrt(jnp.mean(o * o, axis=-1, keepdims=True) + RMS_EPS) * ng
        g = gr_ref[:, vs]
        o_ref[:, vs] = (o * (g * jax.nn.sigmoid(g))).astype(o_ref.dtype)


def _gla(gq, gk, gv, la, gr, norm_g, ct):
    B, S, _ = gq.shape
    blk = lambda n: pl.BlockSpec((None, ct, n), lambda b, j: (b, j, 0))
    ng = norm_g.reshape(1, GLA_DV)
    return pl.pallas_call(
        _gla_kernel, grid=(B, S // ct), out_shape=jax.ShapeDtypeStruct((B, S, W_GV), bf16), out_specs=blk(W_GV),
        in_specs=[blk(W_GQK), blk(W_GQK), blk(W_GV), blk(W_GQK), blk(W_GV), pl.BlockSpec(ng.shape, lambda b, j: (0, 0))],
        scratch_shapes=[pltpu.VMEM((GLA_HEADS, GLA_DV, GLA_DK), f32)],
        compiler_params=_params("parallel", "arbitrary"), name="gla",
    )(gq, gk, gv, la, gr, ng)


def _mix_out_kernel(x_ref, ya_ref, yb_ref, wa_ref, wb_ref, g_ref, b_ref, o_ref, *, alpha):
    mix = _dot(ya_ref[...], wa_ref[...]) + _dot(yb_ref[...], wb_ref[...])
    o_ref[...] = _layer_norm(alpha * x_ref[...] + mix, g_ref[...], b_ref[...])


def _mix_out(x2, ya, yb, w_out, g, b, alpha, tm):
    T, D = x2.shape
    wa, wb = w_out[:W_DSA].astype(bf16), w_out[W_DSA:].astype(bf16)
    tok = lambda n: pl.BlockSpec((tm, n), lambda i: (i, 0))
    full = lambda a: pl.BlockSpec(a.shape, lambda i: (0, 0))
    g, b = g.reshape(1, D), b.reshape(1, D)
    return pl.pallas_call(
        functools.partial(_mix_out_kernel, alpha=alpha), grid=(T // tm,),
        out_shape=jax.ShapeDtypeStruct((T, D), f32), out_specs=tok(D),
        in_specs=[tok(D), tok(W_DSA), tok(W_GV), full(wa), full(wb), full(g), full(b)],
        compiler_params=_params("parallel"), name="mix_out",
    )(x2, ya, yb, wa, wb, g, b)


def _mem_kv_kernel(m_ref, wk_ref, wv_ref, k_ref, v_ref):
    m = m_ref[...].astype(bf16)
    k_ref[...] = _dot(m, wk_ref[...]).astype(bf16)
    v_ref[...] = _dot(m, wv_ref[...]).astype(bf16)


def _mem_kv(mem, w_k, w_v):
    B, M, D = mem.shape
    wk, wv = w_k.astype(bf16), w_v.astype(bf16)
    blk = pl.BlockSpec((None, M, D), lambda b: (b, 0, 0))
    full = pl.BlockSpec((D, D), lambda b: (0, 0))
    sd = jax.ShapeDtypeStruct((B, M, D), bf16)
    return pl.pallas_call(_mem_kv_kernel, grid=(B,), out_shape=[sd, sd], out_specs=[blk, blk],
                          in_specs=[blk, full, full], compiler_params=_params("parallel"), name="mem_kv")(mem, wk, wv)


def _xattn_kernel(h_ref, k_ref, v_ref, wq_ref, wo_ref, g_ref, b_ref, o_ref, *, alpha):
    h = h_ref[...]
    D = h.shape[1]
    hd = D // XATTN_HEADS
    q = (_dot(h.astype(bf16), wq_ref[...]) * (hd ** -0.5)).astype(bf16)
    outs = []
    for a in range(XATTN_HEADS):
        s = slice(a * hd, (a + 1) * hd)
        lg = _dot(q[:, s], k_ref[:, s], _NT)
        p = jnp.exp(lg - jnp.max(lg, axis=1, keepdims=True))
        l = jnp.sum(p, axis=1, keepdims=True)
        outs.append((_dot(p.astype(bf16), v_ref[:, s]) / l).astype(bf16))
    ca = _dot(jnp.concatenate(outs, axis=1), wo_ref[...])
    o_ref[...] = _layer_norm(alpha * h + ca, g_ref[...], b_ref[...])


def _xattn(h1, km, vm, w_q, w_o, g, b, alpha, tm):
    B, S, D = h1.shape
    M = km.shape[1]
    wq, wo = w_q.astype(bf16), w_o.astype(bf16)
    g, b = g.reshape(1, D), b.reshape(1, D)
    tok = pl.BlockSpec((None, tm, D), lambda bi, j: (bi, j, 0))
    per_b = pl.BlockSpec((None, M, D), lambda bi, j: (bi, 0, 0))
    full = lambda a: pl.BlockSpec(a.shape, lambda bi, j: (0, 0))
    return pl.pallas_call(
        functools.partial(_xattn_kernel, alpha=alpha), grid=(B, S // tm),
        out_shape=jax.ShapeDtypeStruct((B, S, D), f32), out_specs=tok,
        in_specs=[tok, per_b, per_b, full(wq), full(wo), full(g), full(b)],
        compiler_params=_params("parallel", "parallel"), name="xattn",
    )(h1, km, vm, wq, wo, g, b)


def _top_rows(s, n_top, ids=None, payload=None):
    if ids is None:
        ids = lax.broadcasted_iota(i32, s.shape, 0)
    vals, picks = [], []
    for _ in range(n_top):
        m = jnp.max(s, axis=0, keepdims=True)
        am = jnp.min(jnp.where(s == m, ids, jnp.int32(2 ** 30)), axis=0, keepdims=True)
        hit = ids == am
        vals.append(m)
        picks.append(am if payload is None else jnp.max(jnp.where(hit, payload, -1), axis=0, keepdims=True))
        s = jnp.where(hit, NEG_INF, s)
    return jnp.concatenate(vals, axis=0), jnp.concatenate(picks, axis=0)


def _pair_candidates(v1, i1, v2, i2):
    n = PEER_TOPK
    sub = lax.broadcasted_iota(i32, (SUBLANES,) + v1.shape[1:], 0)
    blocks = [(v1[0:1] + v2, lax.broadcasted_iota(i32, v2.shape, 0), i1[0:1] * PEER_N_KEYS + i2)]
    for a in range(1, 5):
        blocks.append((v1[a:a + 1] + v2[:SUBLANES], a * n + sub, i1[a:a + 1] * PEER_N_KEYS + i2[:SUBLANES]))
    pick = lambda x: jnp.where(sub < 2, x[5:6], jnp.where(sub < 4, x[6:7], x[7:8]))
    alt = lambda x: jnp.where((sub & 1) == 0, x[0:1], x[1:2])
    a_of = jnp.where(sub < 2, 5, jnp.where(sub < 4, 6, 7))
    blocks.append((jnp.where(sub < 6, pick(v1) + alt(v2), NEG_INF), a_of * n + (sub & 1),
                   pick(i1) * PEER_N_KEYS + alt(i2)))
    blocks.append((v1[SUBLANES:] + v2[0:1], (sub + SUBLANES) * n, i1[SUBLANES:] * PEER_N_KEYS + i2[0:1]))
    return tuple(jnp.concatenate(parts, axis=0) for parts in zip(*blocks))


def _route_kernel(h_ref, wq_ref, k1_ref, k2_ref, row_ref, ne_ref, g_ref):
    q = _dot(h_ref[...].astype(bf16), wq_ref[...])
    dk = PEER_D_KEY // 2
    k1, k2 = k1_ref[...], k2_ref[...]
    for a in range(PEER_HEADS):
        qa = q[:, a * PEER_D_KEY:a * PEER_D_KEY + dk].astype(bf16)
        qb = q[:, a * PEER_D_KEY + dk:(a + 1) * PEER_D_KEY].astype(bf16)
        v1, i1 = _top_rows(_dot(k1, qa, _NT), PEER_TOPK)
        v2, i2 = _top_rows(_dot(k2, qb, _NT), PEER_TOPK)
        cand, ids, cidx = _pair_candidates(v1, i1, v2, i2)
        top, experts = _top_rows(cand, PEER_TOPK, ids=ids, payload=cidx)
        p = jnp.exp(top - top[0:1])
        gates = p / jnp.sum(p, axis=0, keepdims=True)
        for half in range(PEER_TOPK // SUBLANES):
            grp = slice(half * SUBLANES, (half + 1) * SUBLANES)
            e, g = experts[grp], gates[grp]
            odd = e & 1
            n_even = SUBLANES - jnp.sum(odd, axis=0, keepdims=True)
            sub = lax.broadcasted_iota(i32, e.shape, 0)
            evens_before = jnp.zeros_like(n_even)
            e_sorted, g_sorted = jnp.zeros_like(e), jnp.zeros_like(g)
            for r in range(SUBLANES):
                odd_r = odd[r:r + 1]
                dest = jnp.where(odd_r == 1, n_even + (r - evens_before), evens_before)
                hit = sub == dest
                e_sorted = jnp.where(hit, e[r:r + 1], e_sorted)
                g_sorted = jnp.where(hit, g[r:r + 1], g_sorted)
                evens_before = evens_before + (1 - odd_r)
            rows = slice(a * PEER_TOPK + half * SUBLANES, a * PEER_TOPK + (half + 1) * SUBLANES)
            row_ref[rows, :] = (e_sorted >> 1) * SUBLANES
            g_ref[rows, :] = g_sorted
            n = a * (PEER_TOPK // SUBLANES) + half
            ne_ref[n:n + 1, :] = n_even


def _route(h2, w_query, k1, k2, tm):
    T, D = h2.shape
    wq = w_query.astype(bf16)
    k1, k2 = k1.astype(bf16), k2.astype(bf16)
    hk = PEER_HEADS * PEER_TOPK
    full = lambda a: pl.BlockSpec(a.shape, lambda i: (0, 0))
    out = pl.BlockSpec((hk, tm), lambda i: (0, i))
    n_groups = hk // SUBLANES
    sd = jax.ShapeDtypeStruct
    return pl.pallas_call(
        _route_kernel, grid=(T // tm,),
        out_shape=[sd((hk, T), i32), sd((n_groups, T), i32), sd((hk, T), f32)],
        out_specs=[out, pl.BlockSpec((n_groups, tm), lambda i: (0, i)), out],
        in_specs=[pl.BlockSpec((tm, D), lambda i: (i, 0)), full(wq), full(k1), full(k2)],
        compiler_params=_params("parallel"), name="route",
    )(h2, wq, k1, k2)


def _pack_table(tab):
    n, d = tab.shape
    assert d == SUBLANES * LANES
    u = lax.bitcast_convert_type(tab.astype(bf16), jnp.uint16).astype(jnp.uint32).reshape(n // 2, 2, SUBLANES, LANES)
    return lax.bitcast_convert_type((u[:, 0] << 16) | u[:, 1], i32).reshape(n // 2 * SUBLANES, LANES)


def _shift_patterns():
    n_even = np.arange(SUBLANES + 1)[:, None, None]
    p = np.arange(SUBLANES)[None, :, None]
    pat = np.where(p < n_even, 0, 16) + np.zeros((1, 1, LANES), np.int64)
    return jnp.asarray(pat.reshape(-1, LANES), i32)


def _unpack(w, sh_ref, n_even, p):
    shift = jnp.broadcast_to(sh_ref[pl.ds(n_even * SUBLANES + p, 1), :], w.shape)
    return pltpu.bitcast(jnp.left_shift(w, shift) & jnp.int32(-65536), f32)


def _row(tab_ref, row0, sh_ref, n_even, p):
    return _unpack(tab_ref[pl.ds(pl.multiple_of(row0, SUBLANES), SUBLANES), :], sh_ref, n_even, p)


TOKEN_BATCH = 16
GROUP_UNROLL = 4


_TREE_ORDER = (0, 4, 2, 6, 1, 5, 3, 7)


def _packed(w):
    return pltpu.bitcast(w, bf16)


def _merge(p, q, mask, shift):
    moved = pltpu.roll(jnp.where(mask, q, p), shift, 0)
    return pltpu.bitcast(_packed(jnp.where(mask, p, q)) + _packed(moved), i32)


def _sublane_sums(ps):
    sub = lax.broadcasted_iota(i32, ps[0].shape, 0)
    quad = lambda a, b, o: _merge(a, b, ((sub - o) & 7) < 4, 4)
    duo = lambda a, b, o: _merge(a, b, ((sub - o) & 3) < 2, 6)
    r1 = duo(quad(ps[0], ps[1], 0), quad(ps[2], ps[3], 2), 0)
    r2 = duo(quad(ps[4], ps[5], 1), quad(ps[6], ps[7], 3), 1)
    return _merge(r1, r2, (sub & 1) == 0, 7)


def _peer_down_kernel(*refs):
    row_refs, (ne_ref, x_ref, gt_ref, sh_ref, tab_ref, ct_ref, part_ref, actt_ref) = refs[:SUBLANES], refs[SUBLANES:]
    hk, tb = ct_ref.shape
    n_groups = hk // SUBLANES
    lane = lax.broadcasted_iota(i32, (hk, tb), 1)

    def batch(b, carry):
        t0 = b * TOKEN_BATCH

        def token(u, c1):
            t = t0 + u
            xh = pltpu.bitcast(x_ref[t].astype(bf16).astype(f32), i32)
            xw = _packed(xh | lax.shift_right_logical(xh, 16))

            def group(g, c2):
                k0 = pl.multiple_of(g * SUBLANES, SUBLANES)
                tg = t * n_groups + g
                ps = []
                for k in _TREE_ORDER:
                    w = tab_ref[pl.ds(pl.multiple_of(row_refs[k][tg], SUBLANES), SUBLANES), :]
                    ps.append(pltpu.bitcast(_packed(w) * xw, i32))
                sums = _sublane_sums(ps)
                shift = sh_ref[pl.ds(pl.multiple_of(ne_ref[tg] * SUBLANES, SUBLANES), SUBLANES), :]
                part_ref[u, pl.ds(k0, SUBLANES), :] = pltpu.bitcast(jnp.left_shift(sums, shift) & jnp.int32(-65536), f32)
                return c2

            lax.fori_loop(0, n_groups, group, 0, unroll=4 * GROUP_UNROLL)
            return c1

        lax.fori_loop(0, TOKEN_BATCH, token, 0)
        a = actt_ref[...]
        for u in range(TOKEN_BATCH):
            a = jnp.where(lane == t0 + u, jnp.sum(part_ref[u], axis=1, keepdims=True), a)
        actt_ref[...] = a
        return carry

    lax.fori_loop(0, tb // TOKEN_BATCH, batch, 0)
    a = actt_ref[...]
    gelu = 0.5 * a * (1.0 + lax.erf(a * (2.0 ** -0.5)))
    ct_ref[...] = gt_ref[...] * gelu


def _peer_specs(tb, hk):
    per_group = pl.BlockSpec((tb * hk // SUBLANES,), lambda i: (i,), memory_space=pltpu.SMEM)
    whole = pl.BlockSpec(memory_space=pltpu.VMEM)
    return per_group, whole


def _peer_down(rows8, n_even, x3, gates_t, tab, tb):
    hk, T = gates_t.shape
    per_group, whole = _peer_specs(tb, hk)
    per_k = pl.BlockSpec((hk, tb), lambda i: (0, i))
    return pl.pallas_call(
        _peer_down_kernel, grid=(T // tb,), out_shape=jax.ShapeDtypeStruct((hk, T), f32), out_specs=per_k,
        in_specs=[per_group] * (SUBLANES + 1) + [pl.BlockSpec((tb,) + x3.shape[1:], lambda i: (i, 0, 0)), per_k,
                                                 whole, whole],
        scratch_shapes=[pltpu.VMEM((TOKEN_BATCH, hk, LANES), f32), pltpu.VMEM((hk, tb), f32)],
        compiler_params=_params("arbitrary"), name="peer_down",
    )(*rows8, n_even, x3, gates_t, _shift_patterns(), tab)


def _peer_up_kernel(*refs):
    row_refs, (ne_ref, ct_ref, sh_ref, tab_ref, o_ref, cx_ref) = refs[:SUBLANES], refs[SUBLANES:]
    hk, tb = ct_ref.shape
    n_groups = hk // SUBLANES
    n_acc = 4
    vreg = o_ref.shape[1:]
    lane = lax.broadcasted_iota(i32, (hk, tb), 1)

    def batch(b, carry):
        t0 = b * TOKEN_BATCH
        ct = ct_ref[...]
        for u in range(TOKEN_BATCH):
            col = jnp.sum(jnp.where(lane == t0 + u, ct, 0.0), axis=1, keepdims=True)
            cx_ref[u] = jnp.broadcast_to(col, (hk, LANES))

        def token(u, c1):
            t = t0 + u

            def fetch(g):
                tg = t * n_groups + g
                return tuple(tab_ref[pl.ds(pl.multiple_of(r[tg], SUBLANES), SUBLANES), :] for r in row_refs)

            def group(g, accs):
                accs, packed = list(accs), fetch(g)
                k0 = pl.multiple_of(g * SUBLANES, SUBLANES)
                n_even = ne_ref[t * n_groups + g]
                for p in range(SUBLANES):
                    coef = jnp.broadcast_to(cx_ref[u, pl.ds(k0 + p, 1), :], vreg)
                    accs[p % n_acc] = accs[p % n_acc] + _unpack(packed[p], sh_ref, n_even, p) * coef
                return tuple(accs)

            accs = lax.fori_loop(0, n_groups, group, (jnp.zeros(vreg, f32),) * n_acc, unroll=4 * GROUP_UNROLL)
            o_ref[t] = (accs[0] + accs[1]) + (accs[2] + accs[3])
            return c1

        lax.fori_loop(0, TOKEN_BATCH, token, 0)
        return carry

    lax.fori_loop(0, tb // TOKEN_BATCH, batch, 0)


def _peer_up(rows8, n_even, coef_t, tab, tb):
    hk, T = coef_t.shape
    per_group, whole = _peer_specs(tb, hk)
    return pl.pallas_call(
        _peer_up_kernel, grid=(T // tb,), out_shape=jax.ShapeDtypeStruct((T, SUBLANES, LANES), f32),
        out_specs=pl.BlockSpec((tb, SUBLANES, LANES), lambda i: (i, 0, 0)),
        in_specs=[per_group] * (SUBLANES + 1) + [pl.BlockSpec((hk, tb), lambda i: (0, i)), whole, whole],
        scratch_shapes=[pltpu.VMEM((TOKEN_BATCH, hk, LANES), f32)],
        compiler_params=_params("arbitrary"), name="peer_up",
    )(*rows8, n_even, coef_t, _shift_patterns(), tab)


def _ffn_out_kernel(h_ref, f_ref, g_ref, b_ref, o_ref, *, alpha):
    o_ref[...] = _layer_norm(alpha * h_ref[...] + f_ref[...], g_ref[...], b_ref[...])


def _ffn_out(h2, ff, g, b, alpha, tm):
    T, D = h2.shape
    g, b = g.reshape(1, D), b.reshape(1, D)
    tok = pl.BlockSpec((tm, D), lambda i: (i, 0))
    full = pl.BlockSpec((1, D), lambda i: (0, 0))
    return pl.pallas_call(
        functools.partial(_ffn_out_kernel, alpha=alpha), grid=(T // tm,),
        out_shape=jax.ShapeDtypeStruct((T, D), f32), out_specs=tok, in_specs=[tok, tok, full, full],
        compiler_params=_params("parallel"), name="ffn_out",
    )(h2, ff, g, b)


def _tile(n, want):
    t = min(n, want)
    assert n % t == 0, (n, t)
    return t


def kernel(x, positions, mem, w_in, gla_gate_up, gla_gate_bias, gla_norm_g, w_out, ln_mix_g, ln_mix_b, xattn_w_q, xattn_w_k, xattn_w_v, xattn_w_o, ln_mem_g, ln_mem_b, peer_w_query, peer_sub_keys_1, peer_sub_keys_2, peer_expert_down, peer_expert_up, ln_ffn_g, ln_ffn_b):
    B, S, D = x.shape
    T = B * S
    depth = w_in.shape[0]
    alpha = (2.0 * depth) ** 0.25
    tm = _tile(S, 512)
    h = x
    for l in range(depth):
        q, kt, v, qi, kit, misc, gq, gk, gv, la, gr = _proj(h, positions, w_in[l], gla_gate_up[l], gla_gate_bias[l], tm)
        y_dsa = _dsa(q, kt, v, qi, kit, misc, _tile(S, 128), _tile(S, 512))
        y_gla = _gla(gq, gk, gv, la, gr, gla_norm_g[l], tm)
        h1 = _mix_out(h.reshape(T, D), y_dsa.reshape(T, W_DSA), y_gla.reshape(T, W_GV), w_out[l],
                      ln_mix_g[l], ln_mix_b[l], alpha, tm)
        km, vm = _mem_kv(mem, xattn_w_k[l], xattn_w_v[l])
        h2 = _xattn(h1.reshape(B, S, D), km, vm, xattn_w_q[l], xattn_w_o[l], ln_mem_g[l], ln_mem_b[l], alpha, tm)
        h2 = h2.reshape(T, D)
        rows_t, ne_t, gates_t = _route(h2, peer_w_query[l], peer_sub_keys_1[l], peer_sub_keys_2[l], _tile(T, 256))
        flat = lambda a: a.T.reshape(-1)
        n_even = flat(ne_t)
        by_pos = rows_t.reshape(-1, SUBLANES, T).transpose(1, 2, 0).reshape(SUBLANES, -1)
        rows8 = [by_pos[p] for p in range(SUBLANES)]
        tb = _tile(T, LANES)
        coef_t = _peer_down(rows8, n_even, h2.reshape(T, SUBLANES, LANES), gates_t, _pack_table(peer_expert_down[l]), tb)
        ff = _peer_up(rows8, n_even, coef_t, _pack_table(peer_expert_up[l]), tb)
        h = _ffn_out(h2, ff.reshape(T, D), ln_ffn_g[l], ln_ffn_b[l], alpha, tm).reshape(B, S, D)
    return h
```

```python
import functools

import jax
import jax.numpy as jnp
import numpy as np
from jax import lax
from jax.experimental import pallas as pl
from jax.experimental.pallas import tpu as pltpu

f32 = jnp.float32
bf16 = jnp.bfloat16
i32 = jnp.int32

DSA_HEADS = 8
DSA_HEAD_DIM = 64
IDX_HEADS = 8
IDX_DIM = 32
TOPK_MAX = 256
GLA_HEADS = 4
GLA_DK = 64
GLA_DV = 128
GLA_GATE_RANK = 16
GLA_GATE_TEMP = 16.0
GLA_CHUNK = 64
ROPE_THETA = 500000.0
ROPE_FRACTION = 4
XATTN_HEADS = 4
PEER_N_KEYS = 128
PEER_HEADS = 8
PEER_D_KEY = 256
PEER_TOPK = 16
LN_EPS = 1e-5
RMS_EPS = 1e-6

LANES = 128
SUBLANES = 8
VMEM_LIMIT = 56 * 1024 * 1024

INT_MIN = -(2 ** 31)
NEG_INF = float("-inf")

W_DSA = DSA_HEADS * DSA_HEAD_DIM
W_IDX = IDX_HEADS * IDX_DIM
W_GQK = GLA_HEADS * GLA_DK
W_GV = GLA_HEADS * GLA_DV
MISC_KI = 0
MISC_WI = IDX_DIM
MISC_LR = IDX_DIM + IDX_HEADS


def _dot(a, b, dims=(((1,), (0,)), ((), ())), precision=None):
    return lax.dot_general(a, b, dims, precision=precision, preferred_element_type=f32)


_NN = (((1,), (0,)), ((), ()))
_NT = (((1,), (1,)), ((), ()))
_TN = (((0,), (0,)), ((), ()))


def _params(*sem):
    return pltpu.CompilerParams(dimension_semantics=sem, vmem_limit_bytes=VMEM_LIMIT)


def _layer_norm(y, g, b):
    mu = jnp.mean(y, axis=-1, keepdims=True)
    yc = y - mu
    var = jnp.mean(yc * yc, axis=-1, keepdims=True)
    return yc * lax.rsqrt(var + LN_EPS) * g + b


def _rot(xb, c, sa, sb, half):
    return xb * c + pltpu.roll(xb, LANES - half, 1) * sa + pltpu.roll(xb, half, 1) * sb


def _proj_kernel(x_ref, w_ref, gup_ref, gb_ref, cq_ref, saq_ref, sbq_ref, ci_ref, sai_ref, sbi_ref,
                 q_ref, kt_ref, v_ref, qi_ref, kit_ref, misc_ref, gq_ref, gk_ref, gv_ref, la_ref, gr_ref):
    x = x_ref[...].astype(bf16)
    tm = x.shape[0]
    cq, saq, sbq = cq_ref[...], saq_ref[...], sbq_ref[...]
    ci, sai, sbi = ci_ref[...], sai_ref[...], sbi_ref[...]
    hq = DSA_HEAD_DIM // ROPE_FRACTION // 2
    hi = IDX_DIM // ROPE_FRACTION // 2
    o = 0
    scale = DSA_HEAD_DIM ** -0.5
    for j in range(W_DSA // LANES):
        a = _dot(x, w_ref[:, o + j * LANES:o + (j + 1) * LANES])
        q_ref[:, j * LANES:(j + 1) * LANES] = (_rot(a, cq, saq, sbq, hq) * scale).astype(bf16)
    o += W_DSA
    for j in range(W_DSA // LANES):
        a = _dot(x, w_ref[:, o + j * LANES:o + (j + 1) * LANES])
        kt_ref[j * LANES:(j + 1) * LANES, :] = _rot(a, cq, saq, sbq, hq).T.astype(bf16)
    o += W_DSA
    v_ref[...] = _dot(x, w_ref[:, o:o + W_DSA]).astype(bf16)
    o += W_DSA
    for j in range(W_IDX // LANES):
        a = _dot(x, w_ref[:, o + j * LANES:o + (j + 1) * LANES])
        qi_ref[:, j * LANES:(j + 1) * LANES] = _rot(a, ci, sai, sbi, hi).astype(bf16)
    o += W_IDX
    m = _dot(x, w_ref[:, o:o + LANES])
    lane = lax.broadcasted_iota(i32, (tm, LANES), 1)
    is_ki = lane < IDX_DIM
    m = _rot(m, jnp.where(is_ki, ci, 1.0), jnp.where(is_ki, sai, 0.0), jnp.where(is_ki, sbi, 0.0), hi)
    misc_ref[...] = m
    kit_ref[...] = m.T[:IDX_DIM, :].astype(bf16)
    z = _dot(m.astype(bf16), gup_ref[...]) + gb_ref[...]
    la_ref[...] = (jnp.minimum(z, 0.0) - jnp.log1p(jnp.exp(-jnp.abs(z)))) / GLA_GATE_TEMP
    o += LANES
    gq_ref[...] = _dot(x, w_ref[:, o:o + W_GQK])
    o += W_GQK
    gk_ref[...] = _dot(x, w_ref[:, o:o + W_GQK])
    o += W_GQK
    gv_ref[...] = _dot(x, w_ref[:, o:o + W_GV]).astype(bf16)
    o += W_GV
    gr_ref[...] = _dot(x, w_ref[:, o:o + W_GV])


def _rot_tables(positions, head_dim):
    r = head_dim // ROPE_FRACTION
    half = r // 2
    inv_freq = ROPE_THETA ** (-jnp.arange(half, dtype=f32) / half)
    ang = positions.astype(f32)[..., None] * inv_freq
    cos, sin = jnp.cos(ang), jnp.sin(ang)
    lane = np.arange(LANES) % head_dim
    src = np.where(lane < half, lane, np.clip(lane - half, 0, half - 1))
    cos_l, sin_l = cos[..., src], sin[..., src]
    c = jnp.where(lane < r, cos_l, 1.0)
    sa = jnp.where(lane < half, -sin_l, 0.0)
    sb = jnp.where((lane >= half) & (lane < r), sin_l, 0.0)
    return c, sa, sb


def _proj(x, positions, w_in, gate_up, gate_bias, tm):
    B, S, D = x.shape
    splits = np.cumsum([W_DSA, W_DSA, W_DSA, W_IDX, IDX_DIM, IDX_HEADS, W_GQK, W_GQK, W_GV, GLA_GATE_RANK])
    (wq, wk, wv, wqi, wki, wwi, wgq, wgk, wgv, wlr, wgr) = jnp.split(w_in, splits.tolist(), axis=1)
    pad = jnp.zeros((D, LANES - IDX_DIM - IDX_HEADS - GLA_GATE_RANK), w_in.dtype)
    w_a = jnp.concatenate([wq, wk, wv, wqi, wki, wwi, wlr, pad, wgq, wgk, wgv, wgr], axis=1).astype(bf16)
    gup = jnp.zeros((LANES, W_GQK), f32).at[MISC_LR:MISC_LR + GLA_GATE_RANK].set(gate_up).astype(bf16)
    tabs = _rot_tables(positions, DSA_HEAD_DIM) + _rot_tables(positions, IDX_DIM)
    W = w_a.shape[1]
    tok = lambda n: pl.BlockSpec((None, tm, n), lambda b, j: (b, j, 0))
    full = lambda a: pl.BlockSpec(a.shape, lambda b, j: (0,) * a.ndim)
    tr = lambda n: pl.BlockSpec((None, n, tm), lambda b, j: (b, 0, j))
    sd = jax.ShapeDtypeStruct
    out_shape = [sd((B, S, W_DSA), bf16), sd((B, W_DSA, S), bf16), sd((B, S, W_DSA), bf16), sd((B, S, W_IDX), bf16),
                 sd((B, IDX_DIM, S), bf16), sd((B, S, LANES), f32), sd((B, S, W_GQK), f32), sd((B, S, W_GQK), f32),
                 sd((B, S, W_GV), bf16), sd((B, S, W_GQK), f32), sd((B, S, W_GV), f32)]
    out_specs = [tok(W_DSA), tr(W_DSA), tok(W_DSA), tok(W_IDX), tr(IDX_DIM), tok(LANES), tok(W_GQK), tok(W_GQK),
                 tok(W_GV), tok(W_GQK), tok(W_GV)]
    gb = gate_bias.reshape(1, W_GQK)
    return pl.pallas_call(
        _proj_kernel, grid=(B, S // tm), out_shape=out_shape, out_specs=out_specs,
        in_specs=[tok(D), full(w_a), full(gup), full(gb)] + [tok(LANES)] * 6,
        compiler_params=_params("parallel", "parallel"), name="proj",
    )(x, w_a, gup, gb, *tabs)


def _dsa_kernel(q_ref, kt_ref, v_ref, qi_ref, kit_ref, misc_ref, o_ref, key_ref, keyt_ref, bias_ref, lg_ref, mx_ref,
                ls_ref, acc_ref, *, n_sel, idx_bits, kc):
    qb, S = q_ref.shape[0], kt_ref.shape[1]
    i = pl.program_id(1)
    nk = lax.div((i + 1) * qb + (kc - 1), kc)
    assert qb == LANES
    qi = qi_ref[...]
    wi = misc_ref[:, MISC_WI:MISC_WI + IDX_HEADS] * (IDX_HEADS ** -0.5) * (IDX_DIM ** -0.5)
    qpos = lax.broadcasted_iota(i32, (qb, 1), 0) + i * qb
    lane = lax.broadcasted_iota(i32, (qb, kc), 1)

    def chunk(c):
        return pl.ds(pl.multiple_of(c * kc, kc), kc)

    def score_chunk(c, carry):
        kit = kit_ref[:, chunk(c)]
        score = jnp.zeros((qb, kc), f32)
        for h in range(IDX_HEADS):
            d = _dot(qi[:, h * IDX_DIM:(h + 1) * IDX_DIM], kit)
            score = score + wi[:, h:h + 1] * jnp.maximum(d, 0.0)
        score = jnp.where(score == 0.0, 0.0, score)
        bits = pltpu.bitcast(score, i32)
        key = bits ^ ((bits >> 31) & jnp.int32(0x7FFFFFFF))
        key = jnp.where(lane + c * kc <= qpos, key, INT_MIN)
        key_ref[:, chunk(c)] = key
        for j in range(kc // LANES):
            keyt_ref[pl.ds(pl.multiple_of(c * kc + j * LANES, LANES), LANES), :] = key[:, j * LANES:(j + 1) * LANES].T
        return carry

    lax.fori_loop(0, nk, score_chunk, 0)

    qpos_row = lax.broadcasted_iota(i32, (1, qb), 1) + i * qb
    target = jnp.minimum(n_sel, qpos_row + 1).astype(f32)
    key_row = lax.broadcasted_iota(i32, (kc, qb), 0)
    n_part = 4

    def count(pred):
        def body(c, accs):
            hit = jnp.where(pred(keyt_ref[chunk(c), :], key_row + c * kc), 1.0, 0.0)
            accs = list(accs)
            for j in range(kc // SUBLANES):
                accs[j % n_part] = accs[j % n_part] + hit[j * SUBLANES:(j + 1) * SUBLANES]
            return tuple(accs)

        accs = lax.fori_loop(0, nk, body, (jnp.zeros((SUBLANES, qb), f32),) * n_part)
        return jnp.sum((accs[0] + accs[1]) + (accs[2] + accs[3]), axis=0, keepdims=True)

    base = jnp.where(count(lambda k, r: k >= 0) >= target, jnp.int32(0), jnp.int32(INT_MIN))

    def tau_bit(t, base):
        cand = base | jnp.left_shift(jnp.int32(1), 30 - t)
        return jnp.where(count(lambda k, r: k >= cand) >= target, cand, base)

    tau_row = lax.fori_loop(0, 31, tau_bit, base)
    excess = jnp.max(count(lambda k, r: k >= tau_row) - target)

    def tie_limit():
        need = target - count(lambda k, r: k > tau_row)

        def idx_bit(t, m):
            cand = m | jnp.left_shift(jnp.int32(1), idx_bits - 1 - t)
            return jnp.where(count(lambda k, r: (k == tau_row) & (r < cand)) < need, cand, m)

        return lax.fori_loop(0, idx_bits, idx_bit, jnp.zeros((1, qb), i32))

    m_row = lax.cond(excess > 0.0, tie_limit, lambda: jnp.full((1, qb), S, i32))
    to_col = lambda row: jnp.broadcast_to(row, (qb, qb)).T[:, 0:1]
    tau, m_idx = to_col(tau_row), to_col(m_row)

    def bias_chunk(c, carry):
        k = key_ref[:, chunk(c)]
        sel = (k > tau) | ((k == tau) & (lane + c * kc <= m_idx))
        bias_ref[:, chunk(c)] = jnp.where(sel, 0.0, NEG_INF)
        return carry

    lax.fori_loop(0, nk, bias_chunk, 0)

    q = q_ref[...]
    pair = LANES // DSA_HEAD_DIM

    def fold(x, op, acc):
        for j in range(kc // LANES):
            acc = op(acc, x[:, j * LANES:(j + 1) * LANES])
        return acc

    heads = range(DSA_HEADS)
    head_rows = [slice(h * DSA_HEAD_DIM, (h + 1) * DSA_HEAD_DIM) for h in heads]
    mx_ref[...] = jnp.full(mx_ref.shape, NEG_INF, f32)
    ls_ref[...] = jnp.zeros_like(ls_ref)
    acc_ref[...] = jnp.zeros_like(acc_ref)

    def logits(c, carry):
        bias = bias_ref[:, chunk(c)]
        for h in heads:
            lg = _dot(q[:, head_rows[h]], kt_ref[head_rows[h], chunk(c)]) + bias
            lg_ref[h, :, chunk(c)] = lg
            mx_ref[h] = fold(lg, jnp.maximum, mx_ref[h])
        return carry

    lax.fori_loop(0, nk, logits, 0)
    ms = [jnp.max(mx_ref[h], axis=1, keepdims=True) for h in heads]

    def weigh(c, carry):
        for h in heads:
            p = jnp.exp(lg_ref[h, :, chunk(c)] - ms[h])
            ls_ref[h] = fold(p, jnp.add, ls_ref[h])
            slab = slice((h // pair) * LANES, (h // pair + 1) * LANES)
            acc_ref[h] = acc_ref[h] + _dot(p.astype(bf16), v_ref[chunk(c), slab])
        return carry

    lax.fori_loop(0, nk, weigh, 0)
    for h in heads:
        l = jnp.sum(ls_ref[h], axis=1, keepdims=True)
        off = (h % pair) * DSA_HEAD_DIM
        o_ref[:, head_rows[h]] = (acc_ref[h][:, off:off + DSA_HEAD_DIM] / l).astype(o_ref.dtype)


def _dsa(q, kt, v, qi, kit, misc, qb, kc):
    B, S, _ = q.shape
    n_sel = min(TOPK_MAX, S // 4)
    blk = lambda n: pl.BlockSpec((None, qb, n), lambda b, i: (b, i, 0))
    per_b = lambda r, c: pl.BlockSpec((None, r, c), lambda b, i: (b, 0, 0))
    kern = functools.partial(_dsa_kernel, n_sel=n_sel, idx_bits=max(1, (S - 1).bit_length()), kc=kc)
    return pl.pallas_call(
        kern, grid=(B, S // qb), out_shape=jax.ShapeDtypeStruct((B, S, W_DSA), bf16), out_specs=blk(W_DSA),
        in_specs=[blk(W_DSA), per_b(W_DSA, S), per_b(S, W_DSA), blk(W_IDX), per_b(IDX_DIM, S), blk(LANES)],
        scratch_shapes=[pltpu.VMEM((qb, S), i32), pltpu.VMEM((S, qb), i32), pltpu.VMEM((qb, S), f32),
                        pltpu.VMEM((DSA_HEADS, qb, S), f32)]
        + [pltpu.VMEM((DSA_HEADS, qb, LANES), f32)] * 3,
        compiler_params=_params("parallel", "arbitrary"), name="dsa",
    )(q, kt, v, qi, kit, misc)


def _gla_kernel(gq_ref, gk_ref, gv_ref, la_ref, gr_ref, ng_ref, o_ref, state_ref):
    ct = gq_ref.shape[0]
    nch = ct // GLA_CHUNK

    @pl.when(pl.program_id(1) == 0)
    def _():
        state_ref[...] = jnp.zeros_like(state_ref)

    la = la_ref[...]
    r = lax.broadcasted_iota(i32, (ct, ct), 0)
    c = lax.broadcasted_iota(i32, (ct, ct), 1)
    same = (r // GLA_CHUNK) == (c // GLA_CHUNK)
    causal = same & (c <= r)
    hp = lax.Precision.HIGHEST
    bcum = _dot(jnp.where(causal, 1.0, 0.0), la, precision=hp)
    blast = _dot(jnp.where(same, 1.0, 0.0), la, precision=hp)
    q_dec = (gq_ref[...] * (GLA_DK ** -0.5) * jnp.exp(bcum)).astype(bf16)
    k_inv = (gk_ref[...] * jnp.exp(-bcum)).astype(bf16)
    k_end = (gk_ref[...] * jnp.exp(blast - bcum)).astype(bf16)
    decay = jnp.exp(blast)
    ng = ng_ref[...]
    for h in range(GLA_HEADS):
        ks = slice(h * GLA_DK, (h + 1) * GLA_DK)
        vs = slice(h * GLA_DV, (h + 1) * GLA_DV)
        qd, ki, ke, vh = q_dec[:, ks], k_inv[:, ks], k_end[:, ks], gv_ref[:, vs]
        attn = jnp.where(causal, _dot(qd, ki, _NT), 0.0)
        o = _dot(attn.astype(bf16), vh)
        st = state_ref[h]
        inter = []
        for n in range(nch):
            rows = slice(n * GLA_CHUNK, (n + 1) * GLA_CHUNK)
            inter.append(_dot(qd[rows], st.astype(bf16), _NT))
            st = st * decay[n * GLA_CHUNK:n * GLA_CHUNK + 1, ks] + _dot(vh[rows], ke[rows], _TN)
        state_ref[h] = st
        o = o + jnp.concatenate(inter, axis=0)
        o = o * lax.rsqrt(jnp.mean(o * o, axis=-1, keepdims=True) + RMS_EPS) * ng
        g = gr_ref[:, vs]
        o_ref[:, vs] = (o * (g * jax.nn.sigmoid(g))).astype(o_ref.dtype)


def _gla(gq, gk, gv, la, gr, norm_g, ct):
    B, S, _ = gq.shape
    blk = lambda n: pl.BlockSpec((None, ct, n), lambda b, j: (b, j, 0))
    ng = norm_g.reshape(1, GLA_DV)
    return pl.pallas_call(
        _gla_kernel, grid=(B, S // ct), out_shape=jax.ShapeDtypeStruct((B, S, W_GV), bf16), out_specs=blk(W_GV),
        in_specs=[blk(W_GQK), blk(W_GQK), blk(W_GV), blk(W_GQK), blk(W_GV), pl.BlockSpec(ng.shape, lambda b, j: (0, 0))],
        scratch_shapes=[pltpu.VMEM((GLA_HEADS, GLA_DV, GLA_DK), f32)],
        compiler_params=_params("parallel", "arbitrary"), name="gla",
    )(gq, gk, gv, la, gr, ng)


def _mix_out_kernel(x_ref, ya_ref, yb_ref, wa_ref, wb_ref, g_ref, b_ref, o_ref, *, alpha):
    mix = _dot(ya_ref[...], wa_ref[...]) + _dot(yb_ref[...], wb_ref[...])
    o_ref[...] = _layer_norm(alpha * x_ref[...] + mix, g_ref[...], b_ref[...])


def _mix_out(x2, ya, yb, w_out, g, b, alpha, tm):
    T, D = x2.shape
    wa, wb = w_out[:W_DSA].astype(bf16), w_out[W_DSA:].astype(bf16)
    tok = lambda n: pl.BlockSpec((tm, n), lambda i: (i, 0))
    full = lambda a: pl.BlockSpec(a.shape, lambda i: (0, 0))
    g, b = g.reshape(1, D), b.reshape(1, D)
    return pl.pallas_call(
        functools.partial(_mix_out_kernel, alpha=alpha), grid=(T // tm,),
        out_shape=jax.ShapeDtypeStruct((T, D), f32), out_specs=tok(D),
        in_specs=[tok(D), tok(W_DSA), tok(W_GV), full(wa), full(wb), full(g), full(b)],
        compiler_params=_params("parallel"), name="mix_out",
    )(x2, ya, yb, wa, wb, g, b)


def _mem_kv_kernel(m_ref, wk_ref, wv_ref, k_ref, v_ref):
    m = m_ref[...].astype(bf16)
    k_ref[...] = _dot(m, wk_ref[...]).astype(bf16)
    v_ref[...] = _dot(m, wv_ref[...]).astype(bf16)


def _mem_kv(mem, w_k, w_v):
    B, M, D = mem.shape
    wk, wv = w_k.astype(bf16), w_v.astype(bf16)
    blk = pl.BlockSpec((None, M, D), lambda b: (b, 0, 0))
    full = pl.BlockSpec((D, D), lambda b: (0, 0))
    sd = jax.ShapeDtypeStruct((B, M, D), bf16)
    return pl.pallas_call(_mem_kv_kernel, grid=(B,), out_shape=[sd, sd], out_specs=[blk, blk],
                          in_specs=[blk, full, full], compiler_params=_params("parallel"), name="mem_kv")(mem, wk, wv)


def _xattn_kernel(h_ref, k_ref, v_ref, wq_ref, wo_ref, g_ref, b_ref, o_ref, *, alpha):
    h = h_ref[...]
    D = h.shape[1]
    hd = D // XATTN_HEADS
    q = (_dot(h.astype(bf16), wq_ref[...]) * (hd ** -0.5)).astype(bf16)
    outs = []
    for a in range(XATTN_HEADS):
        s = slice(a * hd, (a + 1) * hd)
        lg = _dot(q[:, s], k_ref[:, s], _NT)
        p = jnp.exp(lg - jnp.max(lg, axis=1, keepdims=True))
        l = jnp.sum(p, axis=1, keepdims=True)
        outs.append((_dot(p.astype(bf16), v_ref[:, s]) / l).astype(bf16))
    ca = _dot(jnp.concatenate(outs, axis=1), wo_ref[...])
    o_ref[...] = _layer_norm(alpha * h + ca, g_ref[...], b_ref[...])


def _xattn(h1, km, vm, w_q, w_o, g, b, alpha, tm):
    B, S, D = h1.shape
    M = km.shape[1]
    wq, wo = w_q.astype(bf16), w_o.astype(bf16)
    g, b = g.reshape(1, D), b.reshape(1, D)
    tok = pl.BlockSpec((None, tm, D), lambda bi, j: (bi, j, 0))
    per_b = pl.BlockSpec((None, M, D), lambda bi, j: (bi, 0, 0))
    full = lambda a: pl.BlockSpec(a.shape, lambda bi, j: (0, 0))
    return pl.pallas_call(
        functools.partial(_xattn_kernel, alpha=alpha), grid=(B, S // tm),
        out_shape=jax.ShapeDtypeStruct((B, S, D), f32), out_specs=tok,
        in_specs=[tok, per_b, per_b, full(wq), full(wo), full(g), full(b)],
        compiler_params=_params("parallel", "parallel"), name="xattn",
    )(h1, km, vm, wq, wo, g, b)


def _top_rows(s, n_top, ids=None, payload=None):
    if ids is None:
        ids = lax.broadcasted_iota(i32, s.shape, 0)
    vals, picks = [], []
    for _ in range(n_top):
        m = jnp.max(s, axis=0, keepdims=True)
        am = jnp.min(jnp.where(s == m, ids, jnp.int32(2 ** 30)), axis=0, keepdims=True)
        hit = ids == am
        vals.append(m)
        picks.append(am if payload is None else jnp.max(jnp.where(hit, payload, -1), axis=0, keepdims=True))
        s = jnp.where(hit, NEG_INF, s)
    return jnp.concatenate(vals, axis=0), jnp.concatenate(picks, axis=0)


def _top_keys(s, n_top):
    half = s.shape[0] // 2
    ids = lax.broadcasted_iota(i32, (half,) + s.shape[1:], 0)
    parts = [_top_rows(s[a:a + half], n_top, ids=ids + a) for a in (0, half)]
    vals, idx = (jnp.concatenate(x, axis=0) for x in zip(*parts))
    return _top_rows(vals, n_top, ids=idx)


def _pair_candidates(v1, i1, v2, i2):
    n = PEER_TOPK
    sub = lax.broadcasted_iota(i32, (SUBLANES,) + v1.shape[1:], 0)
    blocks = [(v1[0:1] + v2, lax.broadcasted_iota(i32, v2.shape, 0), i1[0:1] * PEER_N_KEYS + i2)]
    for a in range(1, 5):
        blocks.append((v1[a:a + 1] + v2[:SUBLANES], a * n + sub, i1[a:a + 1] * PEER_N_KEYS + i2[:SUBLANES]))
    pick = lambda x: jnp.where(sub < 2, x[5:6], jnp.where(sub < 4, x[6:7], x[7:8]))
    alt = lambda x: jnp.where((sub & 1) == 0, x[0:1], x[1:2])
    a_of = jnp.where(sub < 2, 5, jnp.where(sub < 4, 6, 7))
    blocks.append((jnp.where(sub < 6, pick(v1) + alt(v2), NEG_INF), a_of * n + (sub & 1),
                   pick(i1) * PEER_N_KEYS + alt(i2)))
    blocks.append((v1[SUBLANES:] + v2[0:1], (sub + SUBLANES) * n, i1[SUBLANES:] * PEER_N_KEYS + i2[0:1]))
    return tuple(jnp.concatenate(parts, axis=0) for parts in zip(*blocks))


def _route_kernel(h_ref, wq_ref, k1_ref, k2_ref, row_ref, ne_ref, g_ref):
    q = _dot(h_ref[...].astype(bf16), wq_ref[...])
    dk = PEER_D_KEY // 2
    k1, k2 = k1_ref[...], k2_ref[...]
    for a in range(PEER_HEADS):
        qa = q[:, a * PEER_D_KEY:a * PEER_D_KEY + dk].astype(bf16)
        qb = q[:, a * PEER_D_KEY + dk:(a + 1) * PEER_D_KEY].astype(bf16)
        v1, i1 = _top_keys(_dot(k1, qa, _NT), PEER_TOPK)
        v2, i2 = _top_keys(_dot(k2, qb, _NT), PEER_TOPK)
        cand, ids, cidx = _pair_candidates(v1, i1, v2, i2)
        top, experts = _top_rows(cand, PEER_TOPK, ids=ids, payload=cidx)
        p = jnp.exp(top - top[0:1])
        gates = p / jnp.sum(p, axis=0, keepdims=True)
        for half in range(PEER_TOPK // SUBLANES):
            grp = slice(half * SUBLANES, (half + 1) * SUBLANES)
            e, g = experts[grp], gates[grp]
            odd = e & 1
            n_even = SUBLANES - jnp.sum(odd, axis=0, keepdims=True)
            sub = lax.broadcasted_iota(i32, e.shape, 0)
            evens_before = jnp.zeros_like(n_even)
            e_sorted, g_sorted = jnp.zeros_like(e), jnp.zeros_like(g)
            for r in range(SUBLANES):
                odd_r = odd[r:r + 1]
                dest = jnp.where(odd_r == 1, n_even + (r - evens_before), evens_before)
                hit = sub == dest
                e_sorted = jnp.where(hit, e[r:r + 1], e_sorted)
                g_sorted = jnp.where(hit, g[r:r + 1], g_sorted)
                evens_before = evens_before + (1 - odd_r)
            rows = slice(a * PEER_TOPK + half * SUBLANES, a * PEER_TOPK + (half + 1) * SUBLANES)
            row_ref[rows, :] = (e_sorted >> 1) * SUBLANES
            g_ref[rows, :] = g_sorted
            n = a * (PEER_TOPK // SUBLANES) + half
            ne_ref[n:n + 1, :] = n_even


def _route(h2, w_query, k1, k2, tm):
    T, D = h2.shape
    wq = w_query.astype(bf16)
    k1, k2 = k1.astype(bf16), k2.astype(bf16)
    hk = PEER_HEADS * PEER_TOPK
    full = lambda a: pl.BlockSpec(a.shape, lambda i: (0, 0))
    out = pl.BlockSpec((hk, tm), lambda i: (0, i))
    n_groups = hk // SUBLANES
    sd = jax.ShapeDtypeStruct
    return pl.pallas_call(
        _route_kernel, grid=(T // tm,),
        out_shape=[sd((hk, T), i32), sd((n_groups, T), i32), sd((hk, T), f32)],
        out_specs=[out, pl.BlockSpec((n_groups, tm), lambda i: (0, i)), out],
        in_specs=[pl.BlockSpec((tm, D), lambda i: (i, 0)), full(wq), full(k1), full(k2)],
        compiler_params=_params("parallel"), name="route",
    )(h2, wq, k1, k2)


def _pack_table(tab):
    n, d = tab.shape
    assert d == SUBLANES * LANES
    u = lax.bitcast_convert_type(tab.astype(bf16), jnp.uint16).astype(jnp.uint32).reshape(n // 2, 2, SUBLANES, LANES)
    return lax.bitcast_convert_type((u[:, 0] << 16) | u[:, 1], i32).reshape(n // 2 * SUBLANES, LANES)


def _shift_patterns():
    n_even = np.arange(SUBLANES + 1)[:, None, None]
    p = np.arange(SUBLANES)[None, :, None]
    pat = np.where(p < n_even, 0, 16) + np.zeros((1, 1, LANES), np.int64)
    return jnp.asarray(pat.reshape(-1, LANES), i32)


def _unpack(w, sh_ref, n_even, p):
    shift = jnp.broadcast_to(sh_ref[pl.ds(n_even * SUBLANES + p, 1), :], w.shape)
    return pltpu.bitcast(jnp.left_shift(w, shift) & jnp.int32(-65536), f32)


def _row(tab_ref, row0, sh_ref, n_even, p):
    return _unpack(tab_ref[pl.ds(pl.multiple_of(row0, SUBLANES), SUBLANES), :], sh_ref, n_even, p)


TOKEN_BATCH = 16
GROUP_UNROLL = 4


_TREE_ORDER = (0, 4, 2, 6, 1, 5, 3, 7)


def _packed(w):
    return pltpu.bitcast(w, bf16)


def _merge(p, q, mask, shift):
    moved = pltpu.roll(jnp.where(mask, q, p), shift, 0)
    return pltpu.bitcast(_packed(jnp.where(mask, p, q)) + _packed(moved), i32)


def _sublane_sums(ps):
    sub = lax.broadcasted_iota(i32, ps[0].shape, 0)
    quad = lambda a, b, o: _merge(a, b, ((sub - o) & 7) < 4, 4)
    duo = lambda a, b, o: _merge(a, b, ((sub - o) & 3) < 2, 6)
    r1 = duo(quad(ps[0], ps[1], 0), quad(ps[2], ps[3], 2), 0)
    r2 = duo(quad(ps[4], ps[5], 1), quad(ps[6], ps[7], 3), 1)
    return _merge(r1, r2, (sub & 1) == 0, 7)


def _peer_down_kernel(*refs):
    row_refs, (ne_ref, x_ref, gt_ref, sh_ref, tab_ref, ct_ref, part_ref, actt_ref) = refs[:SUBLANES], refs[SUBLANES:]
    hk, tb = ct_ref.shape
    n_groups = hk // SUBLANES
    lane = lax.broadcasted_iota(i32, (hk, tb), 1)

    def batch(b, carry):
        t0 = b * TOKEN_BATCH

        def token(u, c1):
            t = t0 + u
            xh = pltpu.bitcast(x_ref[t].astype(bf16).astype(f32), i32)
            xw = _packed(xh | lax.shift_right_logical(xh, 16))

            def group(g, c2):
                k0 = pl.multiple_of(g * SUBLANES, SUBLANES)
                tg = t * n_groups + g
                ps = []
                for k in _TREE_ORDER:
                    w = tab_ref[pl.ds(pl.multiple_of(row_refs[k][tg], SUBLANES), SUBLANES), :]
                    ps.append(pltpu.bitcast(_packed(w) * xw, i32))
                sums = _sublane_sums(ps)
                shift = sh_ref[pl.ds(pl.multiple_of(ne_ref[tg] * SUBLANES, SUBLANES), SUBLANES), :]
                part_ref[u, pl.ds(k0, SUBLANES), :] = pltpu.bitcast(jnp.left_shift(sums, shift) & jnp.int32(-65536), f32)
                return c2

            lax.fori_loop(0, n_groups, group, 0, unroll=4 * GROUP_UNROLL)
            return c1

        lax.fori_loop(0, TOKEN_BATCH, token, 0)
        a = actt_ref[...]
        for u in range(TOKEN_BATCH):
            a = jnp.where(lane == t0 + u, jnp.sum(part_ref[u], axis=1, keepdims=True), a)
        actt_ref[...] = a
        return carry

    lax.fori_loop(0, tb // TOKEN_BATCH, batch, 0)
    a = actt_ref[...]
    gelu = 0.5 * a * (1.0 + lax.erf(a * (2.0 ** -0.5)))
    ct_ref[...] = gt_ref[...] * gelu


def _peer_specs(tb, hk):
    per_group = pl.BlockSpec((tb * hk // SUBLANES,), lambda i: (i,), memory_space=pltpu.SMEM)
    whole = pl.BlockSpec(memory_space=pltpu.VMEM)
    return per_group, whole


def _peer_down(rows8, n_even, x3, gates_t, tab, tb):
    hk, T = gates_t.shape
    per_group, whole = _peer_specs(tb, hk)
    per_k = pl.BlockSpec((hk, tb), lambda i: (0, i))
    return pl.pallas_call(
        _peer_down_kernel, grid=(T // tb,), out_shape=jax.ShapeDtypeStruct((hk, T), f32), out_specs=per_k,
        in_specs=[per_group] * (SUBLANES + 1) + [pl.BlockSpec((tb,) + x3.shape[1:], lambda i: (i, 0, 0)), per_k,
                                                 whole, whole],
        scratch_shapes=[pltpu.VMEM((TOKEN_BATCH, hk, LANES), f32), pltpu.VMEM((hk, tb), f32)],
        compiler_params=_params("arbitrary"), name="peer_down",
    )(*rows8, n_even, x3, gates_t, _shift_patterns(), tab)


def _peer_up_kernel(*refs):
    row_refs, (ne_ref, ct_ref, sh_ref, tab_ref, o_ref, cx_ref) = refs[:SUBLANES], refs[SUBLANES:]
    hk, tb = ct_ref.shape
    n_groups = hk // SUBLANES
    n_acc = 4
    vreg = o_ref.shape[1:]
    lane = lax.broadcasted_iota(i32, (hk, tb), 1)

    def batch(b, carry):
        t0 = b * TOKEN_BATCH
        ct = ct_ref[...]
        for u in range(TOKEN_BATCH):
            col = jnp.sum(jnp.where(lane == t0 + u, ct, 0.0), axis=1, keepdims=True)
            cx_ref[u] = jnp.broadcast_to(col, (hk, LANES))

        def token(u, c1):
            t = t0 + u

            def fetch(g):
                tg = t * n_groups + g
                return tuple(tab_ref[pl.ds(pl.multiple_of(r[tg], SUBLANES), SUBLANES), :] for r in row_refs)

            def group(g, accs):
                accs, packed = list(accs), fetch(g)
                k0 = pl.multiple_of(g * SUBLANES, SUBLANES)
                n_even = ne_ref[t * n_groups + g]
                for p in range(SUBLANES):
                    coef = jnp.broadcast_to(cx_ref[u, pl.ds(k0 + p, 1), :], vreg)
                    accs[p % n_acc] = accs[p % n_acc] + _unpack(packed[p], sh_ref, n_even, p) * coef
                return tuple(accs)

            accs = lax.fori_loop(0, n_groups, group, (jnp.zeros(vreg, f32),) * n_acc, unroll=4 * GROUP_UNROLL)
            o_ref[t] = (accs[0] + accs[1]) + (accs[2] + accs[3])
            return c1

        lax.fori_loop(0, TOKEN_BATCH, token, 0)
        return carry

    lax.fori_loop(0, tb // TOKEN_BATCH, batch, 0)


def _peer_up(rows8, n_even, coef_t, tab, tb):
    hk, T = coef_t.shape
    per_group, whole = _peer_specs(tb, hk)
    return pl.pallas_call(
        _peer_up_kernel, grid=(T // tb,), out_shape=jax.ShapeDtypeStruct((T, SUBLANES, LANES), f32),
        out_specs=pl.BlockSpec((tb, SUBLANES, LANES), lambda i: (i, 0, 0)),
        in_specs=[per_group] * (SUBLANES + 1) + [pl.BlockSpec((hk, tb), lambda i: (0, i)), whole, whole],
        scratch_shapes=[pltpu.VMEM((TOKEN_BATCH, hk, LANES), f32)],
        compiler_params=_params("arbitrary"), name="peer_up",
    )(*rows8, n_even, coef_t, _shift_patterns(), tab)


def _ffn_out_kernel(h_ref, f_ref, g_ref, b_ref, o_ref, *, alpha):
    o_ref[...] = _layer_norm(alpha * h_ref[...] + f_ref[...], g_ref[...], b_ref[...])


def _ffn_out(h2, ff, g, b, alpha, tm):
    T, D = h2.shape
    g, b = g.reshape(1, D), b.reshape(1, D)
    tok = pl.BlockSpec((tm, D), lambda i: (i, 0))
    full = pl.BlockSpec((1, D), lambda i: (0, 0))
    return pl.pallas_call(
        functools.partial(_ffn_out_kernel, alpha=alpha), grid=(T // tm,),
        out_shape=jax.ShapeDtypeStruct((T, D), f32), out_specs=tok, in_specs=[tok, tok, full, full],
        compiler_params=_params("parallel"), name="ffn_out",
    )(h2, ff, g, b)


def _tile(n, want):
    t = min(n, want)
    assert n % t == 0, (n, t)
    return t


def kernel(x, positions, mem, w_in, gla_gate_up, gla_gate_bias, gla_norm_g, w_out, ln_mix_g, ln_mix_b, xattn_w_q, xattn_w_k, xattn_w_v, xattn_w_o, ln_mem_g, ln_mem_b, peer_w_query, peer_sub_keys_1, peer_sub_keys_2, peer_expert_down, peer_expert_up, ln_ffn_g, ln_ffn_b):
    B, S, D = x.shape
    T = B * S
    depth = w_in.shape[0]
    alpha = (2.0 * depth) ** 0.25
    tm = _tile(S, 512)
    h = x
    for l in range(depth):
        q, kt, v, qi, kit, misc, gq, gk, gv, la, gr = _proj(h, positions, w_in[l], gla_gate_up[l], gla_gate_bias[l], tm)
        y_dsa = _dsa(q, kt, v, qi, kit, misc, _tile(S, 128), _tile(S, 512))
        y_gla = _gla(gq, gk, gv, la, gr, gla_norm_g[l], tm)
        h1 = _mix_out(h.reshape(T, D), y_dsa.reshape(T, W_DSA), y_gla.reshape(T, W_GV), w_out[l],
                      ln_mix_g[l], ln_mix_b[l], alpha, tm)
        km, vm = _mem_kv(mem, xattn_w_k[l], xattn_w_v[l])
        h2 = _xattn(h1.reshape(B, S, D), km, vm, xattn_w_q[l], xattn_w_o[l], ln_mem_g[l], ln_mem_b[l], alpha, tm)
        h2 = h2.reshape(T, D)
        rows_t, ne_t, gates_t = _route(h2, peer_w_query[l], peer_sub_keys_1[l], peer_sub_keys_2[l], _tile(T, 256))
        flat = lambda a: a.T.reshape(-1)
        n_even = flat(ne_t)
        by_pos = rows_t.reshape(-1, SUBLANES, T).transpose(1, 2, 0).reshape(SUBLANES, -1)
        rows8 = [by_pos[p] for p in range(SUBLANES)]
        tb = _tile(T, LANES)
        coef_t = _peer_down(rows8, n_even, h2.reshape(T, SUBLANES, LANES), gates_t, _pack_table(peer_expert_down[l]), tb)
        ff = _peer_up(rows8, n_even, coef_t, _pack_table(peer_expert_up[l]), tb)
        h = _ffn_out(h2, ff.reshape(T, D), ln_ffn_g[l], ln_ffn_b[l], alpha, tm).reshape(B, S, D)
    return h
```

```python
import functools

import jax
import jax.numpy as jnp
import numpy as np
from jax import lax
from jax.experimental import pallas as pl
from jax.experimental.pallas import tpu as pltpu

f32 = jnp.float32
bf16 = jnp.bfloat16
i32 = jnp.int32

DSA_HEADS = 8
DSA_HEAD_DIM = 64
IDX_HEADS = 8
IDX_DIM = 32
TOPK_MAX = 256
GLA_HEADS = 4
GLA_DK = 64
GLA_DV = 128
GLA_GATE_RANK = 16
GLA_GATE_TEMP = 16.0
GLA_CHUNK = 64
ROPE_THETA = 500000.0
ROPE_FRACTION = 4
XATTN_HEADS = 4
PEER_N_KEYS = 128
PEER_HEADS = 8
PEER_D_KEY = 256
PEER_TOPK = 16
LN_EPS = 1e-5
RMS_EPS = 1e-6

LANES = 128
SUBLANES = 8
VMEM_LIMIT = 56 * 1024 * 1024

INT_MIN = -(2 ** 31)
NEG_INF = float("-inf")

W_DSA = DSA_HEADS * DSA_HEAD_DIM
W_IDX = IDX_HEADS * IDX_DIM
W_GQK = GLA_HEADS * GLA_DK
W_GV = GLA_HEADS * GLA_DV
MISC_KI = 0
MISC_WI = IDX_DIM
MISC_LR = IDX_DIM + IDX_HEADS


def _dot(a, b, dims=(((1,), (0,)), ((), ())), precision=None):
    return lax.dot_general(a, b, dims, precision=precision, preferred_element_type=f32)


_NN = (((1,), (0,)), ((), ()))
_NT = (((1,), (1,)), ((), ()))
_TN = (((0,), (0,)), ((), ()))


def _params(*sem):
    return pltpu.CompilerParams(dimension_semantics=sem, vmem_limit_bytes=VMEM_LIMIT)


def _layer_norm(y, g, b):
    mu = jnp.mean(y, axis=-1, keepdims=True)
    yc = y - mu
    var = jnp.mean(yc * yc, axis=-1, keepdims=True)
    return yc * lax.rsqrt(var + LN_EPS) * g + b


def _rot(xb, c, sa, sb, half):
    return xb * c + pltpu.roll(xb, LANES - half, 1) * sa + pltpu.roll(xb, half, 1) * sb


def _proj_kernel(x_ref, w_ref, gup_ref, gb_ref, cq_ref, saq_ref, sbq_ref, ci_ref, sai_ref, sbi_ref,
                 q_ref, kt_ref, v_ref, qi_ref, kit_ref, misc_ref, gq_ref, gk_ref, gv_ref, la_ref, gr_ref):
    x = x_ref[...].astype(bf16)
    tm = x.shape[0]
    cq, saq, sbq = cq_ref[...], saq_ref[...], sbq_ref[...]
    ci, sai, sbi = ci_ref[...], sai_ref[...], sbi_ref[...]
    hq = DSA_HEAD_DIM // ROPE_FRACTION // 2
    hi = IDX_DIM // ROPE_FRACTION // 2
    o = 0
    scale = DSA_HEAD_DIM ** -0.5
    for j in range(W_DSA // LANES):
        a = _dot(x, w_ref[:, o + j * LANES:o + (j + 1) * LANES])
        q_ref[:, j * LANES:(j + 1) * LANES] = (_rot(a, cq, saq, sbq, hq) * scale).astype(bf16)
    o += W_DSA
    for j in range(W_DSA // LANES):
        a = _dot(x, w_ref[:, o + j * LANES:o + (j + 1) * LANES])
        kt_ref[j * LANES:(j + 1) * LANES, :] = _rot(a, cq, saq, sbq, hq).T.astype(bf16)
    o += W_DSA
    v_ref[...] = _dot(x, w_ref[:, o:o + W_DSA]).astype(bf16)
    o += W_DSA
    for j in range(W_IDX // LANES):
        a = _dot(x, w_ref[:, o + j * LANES:o + (j + 1) * LANES])
        qi_ref[:, j * LANES:(j + 1) * LANES] = _rot(a, ci, sai, sbi, hi).astype(bf16)
    o += W_IDX
    m = _dot(x, w_ref[:, o:o + LANES])
    lane = lax.broadcasted_iota(i32, (tm, LANES), 1)
    is_ki = lane < IDX_DIM
    m = _rot(m, jnp.where(is_ki, ci, 1.0), jnp.where(is_ki, sai, 0.0), jnp.where(is_ki, sbi, 0.0), hi)
    misc_ref[...] = m
    kit_ref[...] = m.T[:IDX_DIM, :].astype(bf16)
    z = _dot(m.astype(bf16), gup_ref[...]) + gb_ref[...]
    la_ref[...] = (jnp.minimum(z, 0.0) - jnp.log1p(jnp.exp(-jnp.abs(z)))) / GLA_GATE_TEMP
    o += LANES
    gq_ref[...] = _dot(x, w_ref[:, o:o + W_GQK])
    o += W_GQK
    gk_ref[...] = _dot(x, w_ref[:, o:o + W_GQK])
    o += W_GQK
    gv_ref[...] = _dot(x, w_ref[:, o:o + W_GV]).astype(bf16)
    o += W_GV
    gr_ref[...] = _dot(x, w_ref[:, o:o + W_GV])


def _rot_tables(positions, head_dim):
    r = head_dim // ROPE_FRACTION
    half = r // 2
    inv_freq = ROPE_THETA ** (-jnp.arange(half, dtype=f32) / half)
    ang = positions.astype(f32)[..., None] * inv_freq
    cos, sin = jnp.cos(ang), jnp.sin(ang)
    lane = np.arange(LANES) % head_dim
    src = np.where(lane < half, lane, np.clip(lane - half, 0, half - 1))
    cos_l, sin_l = cos[..., src], sin[..., src]
    c = jnp.where(lane < r, cos_l, 1.0)
    sa = jnp.where(lane < half, -sin_l, 0.0)
    sb = jnp.where((lane >= half) & (lane < r), sin_l, 0.0)
    return c, sa, sb


def _proj(x, positions, w_in, gate_up, gate_bias, tm):
    B, S, D = x.shape
    splits = np.cumsum([W_DSA, W_DSA, W_DSA, W_IDX, IDX_DIM, IDX_HEADS, W_GQK, W_GQK, W_GV, GLA_GATE_RANK])
    (wq, wk, wv, wqi, wki, wwi, wgq, wgk, wgv, wlr, wgr) = jnp.split(w_in, splits.tolist(), axis=1)
    pad = jnp.zeros((D, LANES - IDX_DIM - IDX_HEADS - GLA_GATE_RANK), w_in.dtype)
    w_a = jnp.concatenate([wq, wk, wv, wqi, wki, wwi, wlr, pad, wgq, wgk, wgv, wgr], axis=1).astype(bf16)
    gup = jnp.zeros((LANES, W_GQK), f32).at[MISC_LR:MISC_LR + GLA_GATE_RANK].set(gate_up).astype(bf16)
    tabs = _rot_tables(positions, DSA_HEAD_DIM) + _rot_tables(positions, IDX_DIM)
    W = w_a.shape[1]
    tok = lambda n: pl.BlockSpec((None, tm, n), lambda b, j: (b, j, 0))
    full = lambda a: pl.BlockSpec(a.shape, lambda b, j: (0,) * a.ndim)
    tr = lambda n: pl.BlockSpec((None, n, tm), lambda b, j: (b, 0, j))
    sd = jax.ShapeDtypeStruct
    out_shape = [sd((B, S, W_DSA), bf16), sd((B, W_DSA, S), bf16), sd((B, S, W_DSA), bf16), sd((B, S, W_IDX), bf16),
                 sd((B, IDX_DIM, S), bf16), sd((B, S, LANES), f32), sd((B, S, W_GQK), f32), sd((B, S, W_GQK), f32),
                 sd((B, S, W_GV), bf16), sd((B, S, W_GQK), f32), sd((B, S, W_GV), f32)]
    out_specs = [tok(W_DSA), tr(W_DSA), tok(W_DSA), tok(W_IDX), tr(IDX_DIM), tok(LANES), tok(W_GQK), tok(W_GQK),
                 tok(W_GV), tok(W_GQK), tok(W_GV)]
    gb = gate_bias.reshape(1, W_GQK)
    return pl.pallas_call(
        _proj_kernel, grid=(B, S // tm), out_shape=out_shape, out_specs=out_specs,
        in_specs=[tok(D), full(w_a), full(gup), full(gb)] + [tok(LANES)] * 6,
        compiler_params=_params("parallel", "parallel"), name="proj",
    )(x, w_a, gup, gb, *tabs)


def _dsa_kernel(q_ref, kt_ref, v_ref, qi_ref, kit_ref, misc_ref, o_ref, key_ref, keyt_ref, bias_ref, lg_ref, mx_ref,
                ls_ref, acc_ref, *, n_sel, idx_bits, kc):
    qb, S = q_ref.shape[0], kt_ref.shape[1]
    i = pl.program_id(1)
    nk = lax.div((i + 1) * qb + (kc - 1), kc)
    assert qb == LANES
    qi = qi_ref[...]
    wi = misc_ref[:, MISC_WI:MISC_WI + IDX_HEADS] * (IDX_HEADS ** -0.5) * (IDX_DIM ** -0.5)
    qpos = lax.broadcasted_iota(i32, (qb, 1), 0) + i * qb
    lane = lax.broadcasted_iota(i32, (qb, kc), 1)

    def chunk(c):
        return pl.ds(pl.multiple_of(c * kc, kc), kc)

    def score_chunk(c, carry):
        kit = kit_ref[:, chunk(c)]
        score = jnp.zeros((qb, kc), f32)
        for h in range(IDX_HEADS):
            d = _dot(qi[:, h * IDX_DIM:(h + 1) * IDX_DIM], kit)
            score = score + wi[:, h:h + 1] * jnp.maximum(d, 0.0)
        score = jnp.where(score == 0.0, 0.0, score)
        bits = pltpu.bitcast(score, i32)
        key = bits ^ ((bits >> 31) & jnp.int32(0x7FFFFFFF))
        key = jnp.where(lane + c * kc <= qpos, key, INT_MIN)
        key_ref[:, chunk(c)] = key
        for j in range(kc // LANES):
            keyt_ref[pl.ds(pl.multiple_of(c * kc + j * LANES, LANES), LANES), :] = key[:, j * LANES:(j + 1) * LANES].T
        return carry

    lax.fori_loop(0, nk, score_chunk, 0)

    qpos_row = lax.broadcasted_iota(i32, (1, qb), 1) + i * qb
    target = jnp.minimum(n_sel, qpos_row + 1).astype(f32)
    key_row = lax.broadcasted_iota(i32, (kc, qb), 0)
    n_part = 4

    def count(pred):
        def body(c, accs):
            hit = jnp.where(pred(keyt_ref[chunk(c), :], key_row + c * kc), 1.0, 0.0)
            accs = list(accs)
            for j in range(kc // SUBLANES):
                accs[j % n_part] = accs[j % n_part] + hit[j * SUBLANES:(j + 1) * SUBLANES]
            return tuple(accs)

        accs = lax.fori_loop(0, nk, body, (jnp.zeros((SUBLANES, qb), f32),) * n_part)
        return jnp.sum((accs[0] + accs[1]) + (accs[2] + accs[3]), axis=0, keepdims=True)

    base = jnp.where(count(lambda k, r: k >= 0) >= target, jnp.int32(0), jnp.int32(INT_MIN))

    def tau_bit(t, base):
        cand = base | jnp.left_shift(jnp.int32(1), 30 - t)
        return jnp.where(count(lambda k, r: k >= cand) >= target, cand, base)

    tau_row = lax.fori_loop(0, 31, tau_bit, base)
    excess = jnp.max(count(lambda k, r: k >= tau_row) - target)

    def tie_limit():
        need = target - count(lambda k, r: k > tau_row)

        def idx_bit(t, m):
            cand = m | jnp.left_shift(jnp.int32(1), idx_bits - 1 - t)
            return jnp.where(count(lambda k, r: (k == tau_row) & (r < cand)) < need, cand, m)

        return lax.fori_loop(0, idx_bits, idx_bit, jnp.zeros((1, qb), i32))

    m_row = lax.cond(excess > 0.0, tie_limit, lambda: jnp.full((1, qb), S, i32))
    to_col = lambda row: jnp.broadcast_to(row, (qb, qb)).T[:, 0:1]
    tau, m_idx = to_col(tau_row), to_col(m_row)

    def bias_chunk(c, carry):
        k = key_ref[:, chunk(c)]
        sel = (k > tau) | ((k == tau) & (lane + c * kc <= m_idx))
        bias_ref[:, chunk(c)] = jnp.where(sel, 0.0, NEG_INF)
        return carry

    lax.fori_loop(0, nk, bias_chunk, 0)

    q = q_ref[...]
    pair = LANES // DSA_HEAD_DIM

    def fold(x, op, acc):
        for j in range(kc // LANES):
            acc = op(acc, x[:, j * LANES:(j + 1) * LANES])
        return acc

    heads = range(DSA_HEADS)
    head_rows = [slice(h * DSA_HEAD_DIM, (h + 1) * DSA_HEAD_DIM) for h in heads]
    mx_ref[...] = jnp.full(mx_ref.shape, NEG_INF, f32)
    ls_ref[...] = jnp.zeros_like(ls_ref)
    acc_ref[...] = jnp.zeros_like(acc_ref)

    def logits(c, carry):
        bias = bias_ref[:, chunk(c)]
        for h in heads:
            lg = _dot(q[:, head_rows[h]], kt_ref[head_rows[h], chunk(c)]) + bias
            lg_ref[h, :, chunk(c)] = lg
            mx_ref[h] = fold(lg, jnp.maximum, mx_ref[h])
        return carry

    lax.fori_loop(0, nk, logits, 0)
    ms = [jnp.max(mx_ref[h], axis=1, keepdims=True) for h in heads]

    def weigh(c, carry):
        for h in heads:
            p = jnp.exp(lg_ref[h, :, chunk(c)] - ms[h])
            ls_ref[h] = fold(p, jnp.add, ls_ref[h])
            slab = slice((h // pair) * LANES, (h // pair + 1) * LANES)
            acc_ref[h] = acc_ref[h] + _dot(p.astype(bf16), v_ref[chunk(c), slab])
        return carry

    lax.fori_loop(0, nk, weigh, 0)
    for h in heads:
        l = jnp.sum(ls_ref[h], axis=1, keepdims=True)
        off = (h % pair) * DSA_HEAD_DIM
        o_ref[:, head_rows[h]] = (acc_ref[h][:, off:off + DSA_HEAD_DIM] / l).astype(o_ref.dtype)


def _dsa(q, kt, v, qi, kit, misc, qb, kc):
    B, S, _ = q.shape
    n_sel = min(TOPK_MAX, S // 4)
    blk = lambda n: pl.BlockSpec((None, qb, n), lambda b, i: (b, i, 0))
    per_b = lambda r, c: pl.BlockSpec((None, r, c), lambda b, i: (b, 0, 0))
    kern = functools.partial(_dsa_kernel, n_sel=n_sel, idx_bits=max(1, (S - 1).bit_length()), kc=kc)
    return pl.pallas_call(
        kern, grid=(B, S // qb), out_shape=jax.ShapeDtypeStruct((B, S, W_DSA), bf16), out_specs=blk(W_DSA),
        in_specs=[blk(W_DSA), per_b(W_DSA, S), per_b(S, W_DSA), blk(W_IDX), per_b(IDX_DIM, S), blk(LANES)],
        scratch_shapes=[pltpu.VMEM((qb, S), i32), pltpu.VMEM((S, qb), i32), pltpu.VMEM((qb, S), f32),
                        pltpu.VMEM((DSA_HEADS, qb, S), f32)]
        + [pltpu.VMEM((DSA_HEADS, qb, LANES), f32)] * 3,
        compiler_params=_params("parallel", "arbitrary"), name="dsa",
    )(q, kt, v, qi, kit, misc)


def _gla_kernel(gq_ref, gk_ref, gv_ref, la_ref, gr_ref, ng_ref, o_ref, state_ref):
    ct = gq_ref.shape[0]
    nch = ct // GLA_CHUNK

    @pl.when(pl.program_id(1) == 0)
    def _():
        state_ref[...] = jnp.zeros_like(state_ref)

    la = la_ref[...]
    r = lax.broadcasted_iota(i32, (ct, ct), 0)
    c = lax.broadcasted_iota(i32, (ct, ct), 1)
    same = (r // GLA_CHUNK) == (c // GLA_CHUNK)
    causal = same & (c <= r)
    hp = lax.Precision.HIGHEST
    bcum = _dot(jnp.where(causal, 1.0, 0.0), la, precision=hp)
    blast = _dot(jnp.where(same, 1.0, 0.0), la, precision=hp)
    q_dec = (gq_ref[...] * (GLA_DK ** -0.5) * jnp.exp(bcum)).astype(bf16)
    k_inv = (gk_ref[...] * jnp.exp(-bcum)).astype(bf16)
    k_end = (gk_ref[...] * jnp.exp(blast - bcum)).astype(bf16)
    decay = jnp.exp(blast)
    ng = ng_ref[...]
    for h in range(GLA_HEADS):
        ks = slice(h * GLA_DK, (h + 1) * GLA_DK)
        vs = slice(h * GLA_DV, (h + 1) * GLA_DV)
        qd, ki, ke, vh = q_dec[:, ks], k_inv[:, ks], k_end[:, ks], gv_ref[:, vs]
        attn = jnp.where(causal, _dot(qd, ki, _NT), 0.0)
        o = _dot(attn.astype(bf16), vh)
        st = state_ref[h]
        inter = []
        for n in range(nch):
            rows = slice(n * GLA_CHUNK, (n + 1) * GLA_CHUNK)
            inter.append(_dot(qd[rows], st.astype(bf16), _NT))
            st = st * decay[n * GLA_CHUNK:n * GLA_CHUNK + 1, ks] + _dot(vh[rows], ke[rows], _TN)
        state_ref[h] = st
        o = o + jnp.concatenate(inter, axis=0)
        o = o * lax.rsqrt(jnp.mean(o * o, axis=-1, keepdims=True) + RMS_EPS) * ng
        g = gr_ref[:, vs]
        o_ref[:, vs] = (o * (g * jax.nn.sigmoid(g))).astype(o_ref.dtype)


def _gla(gq, gk, gv, la, gr, norm_g, ct):
    B, S, _ = gq.shape
    blk = lambda n: pl.BlockSpec((None, ct, n), lambda b, j: (b, j, 0))
    ng = norm_g.reshape(1, GLA_DV)
    return pl.pallas_call(
        _gla_kernel, grid=(B, S // ct), out_shape=jax.ShapeDtypeStruct((B, S, W_GV), bf16), out_specs=blk(W_GV),
        in_specs=[blk(W_GQK), blk(W_GQK), blk(W_GV), blk(W_GQK), blk(W_GV), pl.BlockSpec(ng.shape, lambda b, j: (0, 0))],
        scratch_shapes=[pltpu.VMEM((GLA_HEADS, GLA_DV, GLA_DK), f32)],
        compiler_params=_params("parallel", "arbitrary"), name="gla",
    )(gq, gk, gv, la, gr, ng)


def _mix_out_kernel(x_ref, ya_ref, yb_ref, wa_ref, wb_ref, g_ref, b_ref, o_ref, *, alpha):
    mix = _dot(ya_ref[...], wa_ref[...]) + _dot(yb_ref[...], wb_ref[...])
    o_ref[...] = _layer_norm(alpha * x_ref[...] + mix, g_ref[...], b_ref[...])


def _mix_out(x2, ya, yb, w_out, g, b, alpha, tm):
    T, D = x2.shape
    wa, wb = w_out[:W_DSA].astype(bf16), w_out[W_DSA:].astype(bf16)
    tok = lambda n: pl.BlockSpec((tm, n), lambda i: (i, 0))
    full = lambda a: pl.BlockSpec(a.shape, lambda i: (0, 0))
    g, b = g.reshape(1, D), b.reshape(1, D)
    return pl.pallas_call(
        functools.partial(_mix_out_kernel, alpha=alpha), grid=(T // tm,),
        out_shape=jax.ShapeDtypeStruct((T, D), f32), out_specs=tok(D),
        in_specs=[tok(D), tok(W_DSA), tok(W_GV), full(wa), full(wb), full(g), full(b)],
        compiler_params=_params("parallel"), name="mix_out",
    )(x2, ya, yb, wa, wb, g, b)


def _mem_kv_kernel(m_ref, wk_ref, wv_ref, k_ref, v_ref):
    m = m_ref[...].astype(bf16)
    k_ref[...] = _dot(m, wk_ref[...]).astype(bf16)
    v_ref[...] = _dot(m, wv_ref[...]).astype(bf16)


def _mem_kv(mem, w_k, w_v):
    B, M, D = mem.shape
    wk, wv = w_k.astype(bf16), w_v.astype(bf16)
    blk = pl.BlockSpec((None, M, D), lambda b: (b, 0, 0))
    full = pl.BlockSpec((D, D), lambda b: (0, 0))
    sd = jax.ShapeDtypeStruct((B, M, D), bf16)
    return pl.pallas_call(_mem_kv_kernel, grid=(B,), out_shape=[sd, sd], out_specs=[blk, blk],
                          in_specs=[blk, full, full], compiler_params=_params("parallel"), name="mem_kv")(mem, wk, wv)


def _xattn_kernel(h_ref, k_ref, v_ref, wq_ref, wo_ref, g_ref, b_ref, o_ref, *, alpha):
    h = h_ref[...]
    D = h.shape[1]
    hd = D // XATTN_HEADS
    q = (_dot(h.astype(bf16), wq_ref[...]) * (hd ** -0.5)).astype(bf16)
    outs = []
    for a in range(XATTN_HEADS):
        s = slice(a * hd, (a + 1) * hd)
        lg = _dot(q[:, s], k_ref[:, s], _NT)
        p = jnp.exp(lg - jnp.max(lg, axis=1, keepdims=True))
        l = jnp.sum(p, axis=1, keepdims=True)
        outs.append((_dot(p.astype(bf16), v_ref[:, s]) / l).astype(bf16))
    ca = _dot(jnp.concatenate(outs, axis=1), wo_ref[...])
    o_ref[...] = _layer_norm(alpha * h + ca, g_ref[...], b_ref[...])


def _xattn(h1, km, vm, w_q, w_o, g, b, alpha, tm):
    B, S, D = h1.shape
    M = km.shape[1]
    wq, wo = w_q.astype(bf16), w_o.astype(bf16)
    g, b = g.reshape(1, D), b.reshape(1, D)
    tok = pl.BlockSpec((None, tm, D), lambda bi, j: (bi, j, 0))
    per_b = pl.BlockSpec((None, M, D), lambda bi, j: (bi, 0, 0))
    full = lambda a: pl.BlockSpec(a.shape, lambda bi, j: (0, 0))
    return pl.pallas_call(
        functools.partial(_xattn_kernel, alpha=alpha), grid=(B, S // tm),
        out_shape=jax.ShapeDtypeStruct((B, S, D), f32), out_specs=tok,
        in_specs=[tok, per_b, per_b, full(wq), full(wo), full(g), full(b)],
        compiler_params=_params("parallel", "parallel"), name="xattn",
    )(h1, km, vm, wq, wo, g, b)


def _top_rows(s, n_top, ids=None, payload=None):
    if ids is None:
        ids = lax.broadcasted_iota(i32, s.shape, 0).astype(f32)
    vals, picks = [], []
    for _ in range(n_top):
        m = jnp.max(s, axis=0, keepdims=True)
        am = jnp.min(jnp.where(s == m, ids, 2.0 ** 30), axis=0, keepdims=True)
        hit = ids == am
        vals.append(m)
        picks.append(am if payload is None else jnp.max(jnp.where(hit, payload, -1.0), axis=0, keepdims=True))
        s = jnp.where(hit, NEG_INF, s)
    return jnp.concatenate(vals, axis=0), jnp.concatenate(picks, axis=0)


def _pair_candidates(v1, i1, v2, i2):
    n = PEER_TOPK
    sub = lax.broadcasted_iota(i32, (SUBLANES,) + v1.shape[1:], 0)
    as_f = lambda x: x.astype(f32)
    blocks = [(v1[0:1] + v2, as_f(lax.broadcasted_iota(i32, v2.shape, 0)), i1[0:1] * PEER_N_KEYS + i2)]
    for a in range(1, 5):
        blocks.append((v1[a:a + 1] + v2[:SUBLANES], as_f(a * n + sub), i1[a:a + 1] * PEER_N_KEYS + i2[:SUBLANES]))
    pick = lambda x: jnp.where(sub < 2, x[5:6], jnp.where(sub < 4, x[6:7], x[7:8]))
    alt = lambda x: jnp.where((sub & 1) == 0, x[0:1], x[1:2])
    a_of = jnp.where(sub < 2, 5, jnp.where(sub < 4, 6, 7))
    blocks.append((jnp.where(sub < 6, pick(v1) + alt(v2), NEG_INF), as_f(a_of * n + (sub & 1)),
                   pick(i1) * PEER_N_KEYS + alt(i2)))
    blocks.append((v1[SUBLANES:] + v2[0:1], as_f((sub + SUBLANES) * n), i1[SUBLANES:] * PEER_N_KEYS + i2[0:1]))
    return tuple(jnp.concatenate(parts, axis=0) for parts in zip(*blocks))


def _route_kernel(h_ref, wq_ref, k1_ref, k2_ref, row_ref, ne_ref, g_ref):
    q = _dot(h_ref[...].astype(bf16), wq_ref[...])
    dk = PEER_D_KEY // 2
    k1, k2 = k1_ref[...], k2_ref[...]
    for a in range(PEER_HEADS):
        qa = q[:, a * PEER_D_KEY:a * PEER_D_KEY + dk].astype(bf16)
        qb = q[:, a * PEER_D_KEY + dk:(a + 1) * PEER_D_KEY].astype(bf16)
        v1, i1 = _top_rows(_dot(k1, qa, _NT), PEER_TOPK)
        v2, i2 = _top_rows(_dot(k2, qb, _NT), PEER_TOPK)
        cand, ids, cidx = _pair_candidates(v1, i1, v2, i2)
        top, experts = _top_rows(cand, PEER_TOPK, ids=ids, payload=cidx)
        experts = experts.astype(i32)
        p = jnp.exp(top - top[0:1])
        gates = p / jnp.sum(p, axis=0, keepdims=True)
        for half in range(PEER_TOPK // SUBLANES):
            grp = slice(half * SUBLANES, (half + 1) * SUBLANES)
            e, g = experts[grp], gates[grp]
            odd = e & 1
            n_even = SUBLANES - jnp.sum(odd, axis=0, keepdims=True)
            sub = lax.broadcasted_iota(i32, e.shape, 0)
            evens_before = jnp.zeros_like(n_even)
            e_sorted, g_sorted = jnp.zeros_like(e), jnp.zeros_like(g)
            for r in range(SUBLANES):
                odd_r = odd[r:r + 1]
                dest = jnp.where(odd_r == 1, n_even + (r - evens_before), evens_before)
                hit = sub == dest
                e_sorted = jnp.where(hit, e[r:r + 1], e_sorted)
                g_sorted = jnp.where(hit, g[r:r + 1], g_sorted)
                evens_before = evens_before + (1 - odd_r)
            rows = slice(a * PEER_TOPK + half * SUBLANES, a * PEER_TOPK + (half + 1) * SUBLANES)
            row_ref[rows, :] = (e_sorted >> 1) * SUBLANES
            g_ref[rows, :] = g_sorted
            n = a * (PEER_TOPK // SUBLANES) + half
            ne_ref[n:n + 1, :] = n_even


def _route(h2, w_query, k1, k2, tm):
    T, D = h2.shape
    wq = w_query.astype(bf16)
    k1, k2 = k1.astype(bf16), k2.astype(bf16)
    hk = PEER_HEADS * PEER_TOPK
    full = lambda a: pl.BlockSpec(a.shape, lambda i: (0, 0))
    out = pl.BlockSpec((hk, tm), lambda i: (0, i))
    n_groups = hk // SUBLANES
    sd = jax.ShapeDtypeStruct
    return pl.pallas_call(
        _route_kernel, grid=(T // tm,),
        out_shape=[sd((hk, T), i32), sd((n_groups, T), i32), sd((hk, T), f32)],
        out_specs=[out, pl.BlockSpec((n_groups, tm), lambda i: (0, i)), out],
        in_specs=[pl.BlockSpec((tm, D), lambda i: (i, 0)), full(wq), full(k1), full(k2)],
        compiler_params=_params("parallel"), name="route",
    )(h2, wq, k1, k2)


def _pack_table(tab):
    n, d = tab.shape
    assert d == SUBLANES * LANES
    u = lax.bitcast_convert_type(tab.astype(bf16), jnp.uint16).astype(jnp.uint32).reshape(n // 2, 2, SUBLANES, LANES)
    return lax.bitcast_convert_type((u[:, 0] << 16) | u[:, 1], i32).reshape(n // 2 * SUBLANES, LANES)


def _shift_patterns():
    n_even = np.arange(SUBLANES + 1)[:, None, None]
    p = np.arange(SUBLANES)[None, :, None]
    pat = np.where(p < n_even, 0, 16) + np.zeros((1, 1, LANES), np.int64)
    return jnp.asarray(pat.reshape(-1, LANES), i32)


def _unpack(w, sh_ref, n_even, p):
    shift = jnp.broadcast_to(sh_ref[pl.ds(n_even * SUBLANES + p, 1), :], w.shape)
    return pltpu.bitcast(jnp.left_shift(w, shift) & jnp.int32(-65536), f32)


def _row(tab_ref, row0, sh_ref, n_even, p):
    return _unpack(tab_ref[pl.ds(pl.multiple_of(row0, SUBLANES), SUBLANES), :], sh_ref, n_even, p)


TOKEN_BATCH = 16
GROUP_UNROLL = 4


_TREE_ORDER = (0, 4, 2, 6, 1, 5, 3, 7)


def _packed(w):
    return pltpu.bitcast(w, bf16)


def _merge(p, q, mask, shift):
    moved = pltpu.roll(jnp.where(mask, q, p), shift, 0)
    return pltpu.bitcast(_packed(jnp.where(mask, p, q)) + _packed(moved), i32)


def _sublane_sums(ps):
    sub = lax.broadcasted_iota(i32, ps[0].shape, 0)
    quad = lambda a, b, o: _merge(a, b, ((sub - o) & 7) < 4, 4)
    duo = lambda a, b, o: _merge(a, b, ((sub - o) & 3) < 2, 6)
    r1 = duo(quad(ps[0], ps[1], 0), quad(ps[2], ps[3], 2), 0)
    r2 = duo(quad(ps[4], ps[5], 1), quad(ps[6], ps[7], 3), 1)
    return _merge(r1, r2, (sub & 1) == 0, 7)


def _peer_down_kernel(*refs):
    row_refs, (ne_ref, x_ref, gt_ref, sh_ref, tab_ref, ct_ref, part_ref, actt_ref) = refs[:SUBLANES], refs[SUBLANES:]
    hk, tb = ct_ref.shape
    n_groups = hk // SUBLANES
    lane = lax.broadcasted_iota(i32, (hk, tb), 1)

    def batch(b, carry):
        t0 = b * TOKEN_BATCH

        def token(u, c1):
            t = t0 + u
            xh = pltpu.bitcast(x_ref[t].astype(bf16).astype(f32), i32)
            xw = _packed(xh | lax.shift_right_logical(xh, 16))

            def group(g, c2):
                k0 = pl.multiple_of(g * SUBLANES, SUBLANES)
                tg = t * n_groups + g
                ps = []
                for k in _TREE_ORDER:
                    w = tab_ref[pl.ds(pl.multiple_of(row_refs[k][tg], SUBLANES), SUBLANES), :]
                    ps.append(pltpu.bitcast(_packed(w) * xw, i32))
                sums = _sublane_sums(ps)
                shift = sh_ref[pl.ds(pl.multiple_of(ne_ref[tg] * SUBLANES, SUBLANES), SUBLANES), :]
                part_ref[u, pl.ds(k0, SUBLANES), :] = pltpu.bitcast(jnp.left_shift(sums, shift) & jnp.int32(-65536), f32)
                return c2

            lax.fori_loop(0, n_groups, group, 0, unroll=4 * GROUP_UNROLL)
            return c1

        lax.fori_loop(0, TOKEN_BATCH, token, 0)
        a = actt_ref[...]
        for u in range(TOKEN_BATCH):
            a = jnp.where(lane == t0 + u, jnp.sum(part_ref[u], axis=1, keepdims=True), a)
        actt_ref[...] = a
        return carry

    lax.fori_loop(0, tb // TOKEN_BATCH, batch, 0)
    a = actt_ref[...]
    gelu = 0.5 * a * (1.0 + lax.erf(a * (2.0 ** -0.5)))
    ct_ref[...] = gt_ref[...] * gelu


def _peer_specs(tb, hk):
    per_group = pl.BlockSpec((tb * hk // SUBLANES,), lambda i: (i,), memory_space=pltpu.SMEM)
    whole = pl.BlockSpec(memory_space=pltpu.VMEM)
    return per_group, whole


def _peer_down(rows8, n_even, x3, gates_t, tab, tb):
    hk, T = gates_t.shape
    per_group, whole = _peer_specs(tb, hk)
    per_k = pl.BlockSpec((hk, tb), lambda i: (0, i))
    return pl.pallas_call(
        _peer_down_kernel, grid=(T // tb,), out_shape=jax.ShapeDtypeStruct((hk, T), f32), out_specs=per_k,
        in_specs=[per_group] * (SUBLANES + 1) + [pl.BlockSpec((tb,) + x3.shape[1:], lambda i: (i, 0, 0)), per_k,
                                                 whole, whole],
        scratch_shapes=[pltpu.VMEM((TOKEN_BATCH, hk, LANES), f32), pltpu.VMEM((hk, tb), f32)],
        compiler_params=_params("arbitrary"), name="peer_down",
    )(*rows8, n_even, x3, gates_t, _shift_patterns(), tab)


def _peer_up_kernel(*refs):
    row_refs, (ne_ref, ct_ref, sh_ref, tab_ref, o_ref, cx_ref) = refs[:SUBLANES], refs[SUBLANES:]
    hk, tb = ct_ref.shape
    n_groups = hk // SUBLANES
    n_acc = 4
    vreg = o_ref.shape[1:]
    lane = lax.broadcasted_iota(i32, (hk, tb), 1)

    def batch(b, carry):
        t0 = b * TOKEN_BATCH
        ct = ct_ref[...]
        for u in range(TOKEN_BATCH):
            col = jnp.sum(jnp.where(lane == t0 + u, ct, 0.0), axis=1, keepdims=True)
            cx_ref[u] = jnp.broadcast_to(col, (hk, LANES))

        def token(u, c1):
            t = t0 + u

            def fetch(g):
                tg = t * n_groups + g
                return tuple(tab_ref[pl.ds(pl.multiple_of(r[tg], SUBLANES), SUBLANES), :] for r in row_refs)

            def group(g, accs):
                accs, packed = list(accs), fetch(g)
                k0 = pl.multiple_of(g * SUBLANES, SUBLANES)
                n_even = ne_ref[t * n_groups + g]
                for p in range(SUBLANES):
                    coef = jnp.broadcast_to(cx_ref[u, pl.ds(k0 + p, 1), :], vreg)
                    accs[p % n_acc] = accs[p % n_acc] + _unpack(packed[p], sh_ref, n_even, p) * coef
                return tuple(accs)

            accs = lax.fori_loop(0, n_groups, group, (jnp.zeros(vreg, f32),) * n_acc, unroll=4 * GROUP_UNROLL)
            o_ref[t] = (accs[0] + accs[1]) + (accs[2] + accs[3])
            return c1

        lax.fori_loop(0, TOKEN_BATCH, token, 0)
        return carry

    lax.fori_loop(0, tb // TOKEN_BATCH, batch, 0)


def _peer_up(rows8, n_even, coef_t, tab, tb):
    hk, T = coef_t.shape
    per_group, whole = _peer_specs(tb, hk)
    return pl.pallas_call(
        _peer_up_kernel, grid=(T // tb,), out_shape=jax.ShapeDtypeStruct((T, SUBLANES, LANES), f32),
        out_specs=pl.BlockSpec((tb, SUBLANES, LANES), lambda i: (i, 0, 0)),
        in_specs=[per_group] * (SUBLANES + 1) + [pl.BlockSpec((hk, tb), lambda i: (0, i)), whole, whole],
        scratch_shapes=[pltpu.VMEM((TOKEN_BATCH, hk, LANES), f32)],
        compiler_params=_params("arbitrary"), name="peer_up",
    )(*rows8, n_even, coef_t, _shift_patterns(), tab)


def _ffn_out_kernel(h_ref, f_ref, g_ref, b_ref, o_ref, *, alpha):
    o_ref[...] = _layer_norm(alpha * h_ref[...] + f_ref[...], g_ref[...], b_ref[...])


def _ffn_out(h2, ff, g, b, alpha, tm):
    T, D = h2.shape
    g, b = g.reshape(1, D), b.reshape(1, D)
    tok = pl.BlockSpec((tm, D), lambda i: (i, 0))
    full = pl.BlockSpec((1, D), lambda i: (0, 0))
    return pl.pallas_call(
        functools.partial(_ffn_out_kernel, alpha=alpha), grid=(T // tm,),
        out_shape=jax.ShapeDtypeStruct((T, D), f32), out_specs=tok, in_specs=[tok, tok, full, full],
        compiler_params=_params("parallel"), name="ffn_out",
    )(h2, ff, g, b)


def _tile(n, want):
    t = min(n, want)
    assert n % t == 0, (n, t)
    return t


def kernel(x, positions, mem, w_in, gla_gate_up, gla_gate_bias, gla_norm_g, w_out, ln_mix_g, ln_mix_b, xattn_w_q, xattn_w_k, xattn_w_v, xattn_w_o, ln_mem_g, ln_mem_b, peer_w_query, peer_sub_keys_1, peer_sub_keys_2, peer_expert_down, peer_expert_up, ln_ffn_g, ln_ffn_b):
    B, S, D = x.shape
    T = B * S
    depth = w_in.shape[0]
    alpha = (2.0 * depth) ** 0.25
    tm = _tile(S, 512)
    h = x
    for l in range(depth):
        q, kt, v, qi, kit, misc, gq, gk, gv, la, gr = _proj(h, positions, w_in[l], gla_gate_up[l], gla_gate_bias[l], tm)
        y_dsa = _dsa(q, kt, v, qi, kit, misc, _tile(S, 128), _tile(S, 512))
        y_gla = _gla(gq, gk, gv, la, gr, gla_norm_g[l], tm)
        h1 = _mix_out(h.reshape(T, D), y_dsa.reshape(T, W_DSA), y_gla.reshape(T, W_GV), w_out[l],
                      ln_mix_g[l], ln_mix_b[l], alpha, tm)
        km, vm = _mem_kv(mem, xattn_w_k[l], xattn_w_v[l])
        h2 = _xattn(h1.reshape(B, S, D), km, vm, xattn_w_q[l], xattn_w_o[l], ln_mem_g[l], ln_mem_b[l], alpha, tm)
        h2 = h2.reshape(T, D)
        rows_t, ne_t, gates_t = _route(h2, peer_w_query[l], peer_sub_keys_1[l], peer_sub_keys_2[l], _tile(T, 256))
        flat = lambda a: a.T.reshape(-1)
        n_even = flat(ne_t)
        by_pos = rows_t.reshape(-1, SUBLANES, T).transpose(1, 2, 0).reshape(SUBLANES, -1)
        rows8 = [by_pos[p] for p in range(SUBLANES)]
        tb = _tile(T, LANES)
        coef_t = _peer_down(rows8, n_even, h2.reshape(T, SUBLANES, LANES), gates_t, _pack_table(peer_expert_down[l]), tb)
        ff = _peer_up(rows8, n_even, coef_t, _pack_table(peer_expert_up[l]), tb)
        h = _ffn_out(h2, ff.reshape(T, D), ln_ffn_g[l], ln_ffn_b[l], alpha, tm).reshape(B, S, D)
    return h
```

```python
import functools

import jax
import jax.numpy as jnp
import numpy as np
from jax import lax
from jax.experimental import pallas as pl
from jax.experimental.pallas import tpu as pltpu

f32 = jnp.float32
bf16 = jnp.bfloat16
i32 = jnp.int32

DSA_HEADS = 8
DSA_HEAD_DIM = 64
IDX_HEADS = 8
IDX_DIM = 32
TOPK_MAX = 256
GLA_HEADS = 4
GLA_DK = 64
GLA_DV = 128
GLA_GATE_RANK = 16
GLA_GATE_TEMP = 16.0
GLA_CHUNK = 64
ROPE_THETA = 500000.0
ROPE_FRACTION = 4
XATTN_HEADS = 4
PEER_N_KEYS = 128
PEER_HEADS = 8
PEER_D_KEY = 256
PEER_TOPK = 16
LN_EPS = 1e-5
RMS_EPS = 1e-6

LANES = 128
SUBLANES = 8
VMEM_LIMIT = 56 * 1024 * 1024

INT_MIN = -(2 ** 31)
NEG_INF = float("-inf")

W_DSA = DSA_HEADS * DSA_HEAD_DIM
W_IDX = IDX_HEADS * IDX_DIM
W_GQK = GLA_HEADS * GLA_DK
W_GV = GLA_HEADS * GLA_DV
MISC_KI = 0
MISC_WI = IDX_DIM
MISC_LR = IDX_DIM + IDX_HEADS


def _dot(a, b, dims=(((1,), (0,)), ((), ())), precision=None):
    return lax.dot_general(a, b, dims, precision=precision, preferred_element_type=f32)


_NN = (((1,), (0,)), ((), ()))
_NT = (((1,), (1,)), ((), ()))
_TN = (((0,), (0,)), ((), ()))


def _params(*sem):
    return pltpu.CompilerParams(dimension_semantics=sem, vmem_limit_bytes=VMEM_LIMIT)


def _layer_norm(y, g, b):
    mu = jnp.mean(y, axis=-1, keepdims=True)
    yc = y - mu
    var = jnp.mean(yc * yc, axis=-1, keepdims=True)
    return yc * lax.rsqrt(var + LN_EPS) * g + b


def _rot(xb, c, sa, sb, half):
    return xb * c + pltpu.roll(xb, LANES - half, 1) * sa + pltpu.roll(xb, half, 1) * sb


def _proj_kernel(x_ref, w_ref, gup_ref, gb_ref, cq_ref, saq_ref, sbq_ref, ci_ref, sai_ref, sbi_ref,
                 q_ref, kt_ref, v_ref, qi_ref, kit_ref, misc_ref, gq_ref, gk_ref, gv_ref, la_ref, gr_ref):
    x = x_ref[...].astype(bf16)
    tm = x.shape[0]
    cq, saq, sbq = cq_ref[...], saq_ref[...], sbq_ref[...]
    ci, sai, sbi = ci_ref[...], sai_ref[...], sbi_ref[...]
    hq = DSA_HEAD_DIM // ROPE_FRACTION // 2
    hi = IDX_DIM // ROPE_FRACTION // 2
    o = 0
    scale = DSA_HEAD_DIM ** -0.5
    for j in range(W_DSA // LANES):
        a = _dot(x, w_ref[:, o + j * LANES:o + (j + 1) * LANES])
        q_ref[:, j * LANES:(j + 1) * LANES] = (_rot(a, cq, saq, sbq, hq) * scale).astype(bf16)
    o += W_DSA
    for j in range(W_DSA // LANES):
        a = _dot(x, w_ref[:, o + j * LANES:o + (j + 1) * LANES])
        kt_ref[j * LANES:(j + 1) * LANES, :] = _rot(a, cq, saq, sbq, hq).T.astype(bf16)
    o += W_DSA
    v_ref[...] = _dot(x, w_ref[:, o:o + W_DSA]).astype(bf16)
    o += W_DSA
    for j in range(W_IDX // LANES):
        a = _dot(x, w_ref[:, o + j * LANES:o + (j + 1) * LANES])
        qi_ref[:, j * LANES:(j + 1) * LANES] = _rot(a, ci, sai, sbi, hi).astype(bf16)
    o += W_IDX
    m = _dot(x, w_ref[:, o:o + LANES])
    lane = lax.broadcasted_iota(i32, (tm, LANES), 1)
    is_ki = lane < IDX_DIM
    m = _rot(m, jnp.where(is_ki, ci, 1.0), jnp.where(is_ki, sai, 0.0), jnp.where(is_ki, sbi, 0.0), hi)
    misc_ref[...] = m
    kit_ref[...] = m.T[:IDX_DIM, :].astype(bf16)
    z = _dot(m.astype(bf16), gup_ref[...]) + gb_ref[...]
    la_ref[...] = (jnp.minimum(z, 0.0) - jnp.log1p(jnp.exp(-jnp.abs(z)))) / GLA_GATE_TEMP
    o += LANES
    gq_ref[...] = _dot(x, w_ref[:, o:o + W_GQK])
    o += W_GQK
    gk_ref[...] = _dot(x, w_ref[:, o:o + W_GQK])
    o += W_GQK
    gv_ref[...] = _dot(x, w_ref[:, o:o + W_GV]).astype(bf16)
    o += W_GV
    gr_ref[...] = _dot(x, w_ref[:, o:o + W_GV])


def _rot_tables(positions, head_dim):
    r = head_dim // ROPE_FRACTION
    half = r // 2
    inv_freq = ROPE_THETA ** (-jnp.arange(half, dtype=f32) / half)
    ang = positions.astype(f32)[..., None] * inv_freq
    cos, sin = jnp.cos(ang), jnp.sin(ang)
    lane = np.arange(LANES) % head_dim
    src = np.where(lane < half, lane, np.clip(lane - half, 0, half - 1))
    cos_l, sin_l = cos[..., src], sin[..., src]
    c = jnp.where(lane < r, cos_l, 1.0)
    sa = jnp.where(lane < half, -sin_l, 0.0)
    sb = jnp.where((lane >= half) & (lane < r), sin_l, 0.0)
    return c, sa, sb


def _proj(x, positions, w_in, gate_up, gate_bias, tm):
    B, S, D = x.shape
    splits = np.cumsum([W_DSA, W_DSA, W_DSA, W_IDX, IDX_DIM, IDX_HEADS, W_GQK, W_GQK, W_GV, GLA_GATE_RANK])
    (wq, wk, wv, wqi, wki, wwi, wgq, wgk, wgv, wlr, wgr) = jnp.split(w_in, splits.tolist(), axis=1)
    pad = jnp.zeros((D, LANES - IDX_DIM - IDX_HEADS - GLA_GATE_RANK), w_in.dtype)
    w_a = jnp.concatenate([wq, wk, wv, wqi, wki, wwi, wlr, pad, wgq, wgk, wgv, wgr], axis=1).astype(bf16)
    gup = jnp.zeros((LANES, W_GQK), f32).at[MISC_LR:MISC_LR + GLA_GATE_RANK].set(gate_up).astype(bf16)
    tabs = _rot_tables(positions, DSA_HEAD_DIM) + _rot_tables(positions, IDX_DIM)
    W = w_a.shape[1]
    tok = lambda n: pl.BlockSpec((None, tm, n), lambda b, j: (b, j, 0))
    full = lambda a: pl.BlockSpec(a.shape, lambda b, j: (0,) * a.ndim)
    tr = lambda n: pl.BlockSpec((None, n, tm), lambda b, j: (b, 0, j))
    sd = jax.ShapeDtypeStruct
    out_shape = [sd((B, S, W_DSA), bf16), sd((B, W_DSA, S), bf16), sd((B, S, W_DSA), bf16), sd((B, S, W_IDX), bf16),
                 sd((B, IDX_DIM, S), bf16), sd((B, S, LANES), f32), sd((B, S, W_GQK), f32), sd((B, S, W_GQK), f32),
                 sd((B, S, W_GV), bf16), sd((B, S, W_GQK), f32), sd((B, S, W_GV), f32)]
    out_specs = [tok(W_DSA), tr(W_DSA), tok(W_DSA), tok(W_IDX), tr(IDX_DIM), tok(LANES), tok(W_GQK), tok(W_GQK),
                 tok(W_GV), tok(W_GQK), tok(W_GV)]
    gb = gate_bias.reshape(1, W_GQK)
    return pl.pallas_call(
        _proj_kernel, grid=(B, S // tm), out_shape=out_shape, out_specs=out_specs,
        in_specs=[tok(D), full(w_a), full(gup), full(gb)] + [tok(LANES)] * 6,
        compiler_params=_params("parallel", "parallel"), name="proj",
    )(x, w_a, gup, gb, *tabs)


def _dsa_kernel(q_ref, kt_ref, v_ref, qi_ref, kit_ref, misc_ref, o_ref, key_ref, keyt_ref, bias_ref, lg_ref, mx_ref,
                ls_ref, acc_ref, *, n_sel, idx_bits, kc):
    qb, S = q_ref.shape[0], kt_ref.shape[1]
    i = pl.program_id(1)
    nk = lax.div((i + 1) * qb + (kc - 1), kc)
    assert qb == LANES
    qi = qi_ref[...]
    wi = misc_ref[:, MISC_WI:MISC_WI + IDX_HEADS] * (IDX_HEADS ** -0.5) * (IDX_DIM ** -0.5)
    qpos = lax.broadcasted_iota(i32, (qb, 1), 0) + i * qb
    lane = lax.broadcasted_iota(i32, (qb, kc), 1)

    def chunk(c):
        return pl.ds(pl.multiple_of(c * kc, kc), kc)

    def score_chunk(c, carry):
        kit = kit_ref[:, chunk(c)]
        score = jnp.zeros((qb, kc), f32)
        for h in range(IDX_HEADS):
            d = _dot(qi[:, h * IDX_DIM:(h + 1) * IDX_DIM], kit)
            score = score + wi[:, h:h + 1] * jnp.maximum(d, 0.0)
        score = jnp.where(score == 0.0, 0.0, score)
        bits = pltpu.bitcast(score, i32)
        key = bits ^ ((bits >> 31) & jnp.int32(0x7FFFFFFF))
        key = jnp.where(lane + c * kc <= qpos, key, INT_MIN)
        key_ref[:, chunk(c)] = key
        for j in range(kc // LANES):
            keyt_ref[pl.ds(pl.multiple_of(c * kc + j * LANES, LANES), LANES), :] = key[:, j * LANES:(j + 1) * LANES].T
        return carry

    lax.fori_loop(0, nk, score_chunk, 0)

    qpos_row = lax.broadcasted_iota(i32, (1, qb), 1) + i * qb
    target = jnp.minimum(n_sel, qpos_row + 1).astype(f32)
    key_row = lax.broadcasted_iota(i32, (kc, qb), 0)
    n_part = 4

    def count(pred):
        def body(c, accs):
            hit = jnp.where(pred(keyt_ref[chunk(c), :], key_row + c * kc), 1.0, 0.0)
            accs = list(accs)
            for j in range(kc // SUBLANES):
                accs[j % n_part] = accs[j % n_part] + hit[j * SUBLANES:(j + 1) * SUBLANES]
            return tuple(accs)

        accs = lax.fori_loop(0, nk, body, (jnp.zeros((SUBLANES, qb), f32),) * n_part)
        return jnp.sum((accs[0] + accs[1]) + (accs[2] + accs[3]), axis=0, keepdims=True)

    base = jnp.where(count(lambda k, r: k >= 0) >= target, jnp.int32(0), jnp.int32(INT_MIN))

    def tau_bit(t, base):
        cand = base | jnp.left_shift(jnp.int32(1), 30 - t)
        return jnp.where(count(lambda k, r: k >= cand) >= target, cand, base)

    tau_row = lax.fori_loop(0, 31, tau_bit, base)
    excess = jnp.max(count(lambda k, r: k >= tau_row) - target)

    def tie_limit():
        need = target - count(lambda k, r: k > tau_row)

        def idx_bit(t, m):
            cand = m | jnp.left_shift(jnp.int32(1), idx_bits - 1 - t)
            return jnp.where(count(lambda k, r: (k == tau_row) & (r < cand)) < need, cand, m)

        return lax.fori_loop(0, idx_bits, idx_bit, jnp.zeros((1, qb), i32))

    m_row = lax.cond(excess > 0.0, tie_limit, lambda: jnp.full((1, qb), S, i32))
    to_col = lambda row: jnp.broadcast_to(row, (qb, qb)).T[:, 0:1]
    tau, m_idx = to_col(tau_row), to_col(m_row)

    def bias_chunk(c, carry):
        k = key_ref[:, chunk(c)]
        sel = (k > tau) | ((k == tau) & (lane + c * kc <= m_idx))
        bias_ref[:, chunk(c)] = jnp.where(sel, 0.0, NEG_INF)
        return carry

    lax.fori_loop(0, nk, bias_chunk, 0)

    q = q_ref[...]
    pair = LANES // DSA_HEAD_DIM

    def fold(x, op, acc):
        for j in range(kc // LANES):
            acc = op(acc, x[:, j * LANES:(j + 1) * LANES])
        return acc

    heads = range(DSA_HEADS)
    head_rows = [slice(h * DSA_HEAD_DIM, (h + 1) * DSA_HEAD_DIM) for h in heads]
    mx_ref[...] = jnp.full(mx_ref.shape, NEG_INF, f32)
    ls_ref[...] = jnp.zeros_like(ls_ref)
    acc_ref[...] = jnp.zeros_like(acc_ref)

    def logits(c, carry):
        bias = bias_ref[:, chunk(c)]
        for h in heads:
            lg = _dot(q[:, head_rows[h]], kt_ref[head_rows[h], chunk(c)]) + bias
            lg_ref[h, :, chunk(c)] = lg
            mx_ref[h] = fold(lg, jnp.maximum, mx_ref[h])
        return carry

    lax.fori_loop(0, nk, logits, 0)
    ms = [jnp.max(mx_ref[h], axis=1, keepdims=True) for h in heads]

    def weigh(c, carry):
        for h in heads:
            p = jnp.exp(lg_ref[h, :, chunk(c)] - ms[h])
            ls_ref[h] = fold(p, jnp.add, ls_ref[h])
            slab = slice((h // pair) * LANES, (h // pair + 1) * LANES)
            acc_ref[h] = acc_ref[h] + _dot(p.astype(bf16), v_ref[chunk(c), slab])
        return carry

    lax.fori_loop(0, nk, weigh, 0)
    for h in heads:
        l = jnp.sum(ls_ref[h], axis=1, keepdims=True)
        off = (h % pair) * DSA_HEAD_DIM
        o_ref[:, head_rows[h]] = (acc_ref[h][:, off:off + DSA_HEAD_DIM] / l).astype(o_ref.dtype)


def _dsa(q, kt, v, qi, kit, misc, qb, kc):
    B, S, _ = q.shape
    n_sel = min(TOPK_MAX, S // 4)
    blk = lambda n: pl.BlockSpec((None, qb, n), lambda b, i: (b, i, 0))
    per_b = lambda r, c: pl.BlockSpec((None, r, c), lambda b, i: (b, 0, 0))
    kern = functools.partial(_dsa_kernel, n_sel=n_sel, idx_bits=max(1, (S - 1).bit_length()), kc=kc)
    return pl.pallas_call(
        kern, grid=(B, S // qb), out_shape=jax.ShapeDtypeStruct((B, S, W_DSA), bf16), out_specs=blk(W_DSA),
        in_specs=[blk(W_DSA), per_b(W_DSA, S), per_b(S, W_DSA), blk(W_IDX), per_b(IDX_DIM, S), blk(LANES)],
        scratch_shapes=[pltpu.VMEM((qb, S), i32), pltpu.VMEM((S, qb), i32), pltpu.VMEM((qb, S), f32),
                        pltpu.VMEM((DSA_HEADS, qb, S), f32)]
        + [pltpu.VMEM((DSA_HEADS, qb, LANES), f32)] * 3,
        compiler_params=_params("parallel", "arbitrary"), name="dsa",
    )(q, kt, v, qi, kit, misc)


def _gla_kernel(gq_ref, gk_ref, gv_ref, la_ref, gr_ref, ng_ref, o_ref, state_ref):
    ct = gq_ref.shape[0]
    nch = ct // GLA_CHUNK

    @pl.when(pl.program_id(1) == 0)
    def _():
        state_ref[...] = jnp.zeros_like(state_ref)

    la = la_ref[...]
    r = lax.broadcasted_iota(i32, (ct, ct), 0)
    c = lax.broadcasted_iota(i32, (ct, ct), 1)
    same = (r // GLA_CHUNK) == (c // GLA_CHUNK)
    causal = same & (c <= r)
    tri = jnp.where(causal, 1.0, 0.0).astype(bf16)
    la_hi = la.astype(bf16)
    rest = la - la_hi.astype(f32)
    la_mid = rest.astype(bf16)
    la_lo = (rest - la_mid.astype(f32)).astype(bf16)
    bcum = (_dot(tri, la_hi) + _dot(tri, la_mid)) + _dot(tri, la_lo)
    blast = jnp.concatenate([jnp.broadcast_to(bcum[(n + 1) * GLA_CHUNK - 1:(n + 1) * GLA_CHUNK], (GLA_CHUNK, la.shape[1]))
                             for n in range(nch)], axis=0)
    q_dec = (gq_ref[...] * (GLA_DK ** -0.5) * jnp.exp(bcum)).astype(bf16)
    k_inv = (gk_ref[...] * jnp.exp(-bcum)).astype(bf16)
    k_end = (gk_ref[...] * jnp.exp(blast - bcum)).astype(bf16)
    decay = jnp.exp(blast)
    ng = ng_ref[...]
    for h in range(GLA_HEADS):
        ks = slice(h * GLA_DK, (h + 1) * GLA_DK)
        vs = slice(h * GLA_DV, (h + 1) * GLA_DV)
        qd, ki, ke, vh = q_dec[:, ks], k_inv[:, ks], k_end[:, ks], gv_ref[:, vs]
        attn = jnp.where(causal, _dot(qd, ki, _NT), 0.0)
        o = _dot(attn.astype(bf16), vh)
        st = state_ref[h]
        inter = []
        for n in range(nch):
            rows = slice(n * GLA_CHUNK, (n + 1) * GLA_CHUNK)
            inter.append(_dot(qd[rows], st.astype(bf16), _NT))
            st = st * decay[n * GLA_CHUNK:n * GLA_CHUNK + 1, ks] + _dot(vh[rows], ke[rows], _TN)
        state_ref[h] = st
        o = o + jnp.concatenate(inter, axis=0)
        o = o * lax.rsqrt(jnp.mean(o * o, axis=-1, keepdims=True) + RMS_EPS) * ng
        g = gr_ref[:, vs]
        o_ref[:, vs] = (o * (g * jax.nn.sigmoid(g))).astype(o_ref.dtype)


def _gla(gq, gk, gv, la, gr, norm_g, ct):
    B, S, _ = gq.shape
    blk = lambda n: pl.BlockSpec((None, ct, n), lambda b, j: (b, j, 0))
    ng = norm_g.reshape(1, GLA_DV)
    return pl.pallas_call(
        _gla_kernel, grid=(B, S // ct), out_shape=jax.ShapeDtypeStruct((B, S, W_GV), bf16), out_specs=blk(W_GV),
        in_specs=[blk(W_GQK), blk(W_GQK), blk(W_GV), blk(W_GQK), blk(W_GV), pl.BlockSpec(ng.shape, lambda b, j: (0, 0))],
        scratch_shapes=[pltpu.VMEM((GLA_HEADS, GLA_DV, GLA_DK), f32)],
        compiler_params=_params("parallel", "arbitrary"), name="gla",
    )(gq, gk, gv, la, gr, ng)


def _mix_out_kernel(x_ref, ya_ref, yb_ref, wa_ref, wb_ref, g_ref, b_ref, o_ref, *, alpha):
    mix = _dot(ya_ref[...], wa_ref[...]) + _dot(yb_ref[...], wb_ref[...])
    o_ref[...] = _layer_norm(alpha * x_ref[...] + mix, g_ref[...], b_ref[...])


def _mix_out(x2, ya, yb, w_out, g, b, alpha, tm):
    T, D = x2.shape
    wa, wb = w_out[:W_DSA].astype(bf16), w_out[W_DSA:].astype(bf16)
    tok = lambda n: pl.BlockSpec((tm, n), lambda i: (i, 0))
    full = lambda a: pl.BlockSpec(a.shape, lambda i: (0, 0))
    g, b = g.reshape(1, D), b.reshape(1, D)
    return pl.pallas_call(
        functools.partial(_mix_out_kernel, alpha=alpha), grid=(T // tm,),
        out_shape=jax.ShapeDtypeStruct((T, D), f32), out_specs=tok(D),
        in_specs=[tok(D), tok(W_DSA), tok(W_GV), full(wa), full(wb), full(g), full(b)],
        compiler_params=_params("parallel"), name="mix_out",
    )(x2, ya, yb, wa, wb, g, b)


def _mem_kv_kernel(m_ref, wk_ref, wv_ref, k_ref, v_ref):
    m = m_ref[...].astype(bf16)
    k_ref[...] = _dot(m, wk_ref[...]).astype(bf16)
    v_ref[...] = _dot(m, wv_ref[...]).astype(bf16)


def _mem_kv(mem, w_k, w_v):
    B, M, D = mem.shape
    wk, wv = w_k.astype(bf16), w_v.astype(bf16)
    blk = pl.BlockSpec((None, M, D), lambda b: (b, 0, 0))
    full = pl.BlockSpec((D, D), lambda b: (0, 0))
    sd = jax.ShapeDtypeStruct((B, M, D), bf16)
    return pl.pallas_call(_mem_kv_kernel, grid=(B,), out_shape=[sd, sd], out_specs=[blk, blk],
                          in_specs=[blk, full, full], compiler_params=_params("parallel"), name="mem_kv")(mem, wk, wv)


def _xattn_kernel(h_ref, k_ref, v_ref, wq_ref, wo_ref, g_ref, b_ref, o_ref, *, alpha):
    h = h_ref[...]
    D = h.shape[1]
    hd = D // XATTN_HEADS
    q = (_dot(h.astype(bf16), wq_ref[...]) * (hd ** -0.5)).astype(bf16)
    outs = []
    for a in range(XATTN_HEADS):
        s = slice(a * hd, (a + 1) * hd)
        lg = _dot(q[:, s], k_ref[:, s], _NT)
        p = jnp.exp(lg - jnp.max(lg, axis=1, keepdims=True))
        l = jnp.sum(p, axis=1, keepdims=True)
        outs.append((_dot(p.astype(bf16), v_ref[:, s]) / l).astype(bf16))
    ca = _dot(jnp.concatenate(outs, axis=1), wo_ref[...])
    o_ref[...] = _layer_norm(alpha * h + ca, g_ref[...], b_ref[...])


def _xattn(h1, km, vm, w_q, w_o, g, b, alpha, tm):
    B, S, D = h1.shape
    M = km.shape[1]
    wq, wo = w_q.astype(bf16), w_o.astype(bf16)
    g, b = g.reshape(1, D), b.reshape(1, D)
    tok = pl.BlockSpec((None, tm, D), lambda bi, j: (bi, j, 0))
    per_b = pl.BlockSpec((None, M, D), lambda bi, j: (bi, 0, 0))
    full = lambda a: pl.BlockSpec(a.shape, lambda bi, j: (0, 0))
    return pl.pallas_call(
        functools.partial(_xattn_kernel, alpha=alpha), grid=(B, S // tm),
        out_shape=jax.ShapeDtypeStruct((B, S, D), f32), out_specs=tok,
        in_specs=[tok, per_b, per_b, full(wq), full(wo), full(g), full(b)],
        compiler_params=_params("parallel", "parallel"), name="xattn",
    )(h1, km, vm, wq, wo, g, b)


def _top_rows(s, n_top, ids=None, payload=None):
    if ids is None:
        ids = lax.broadcasted_iota(i32, s.shape, 0).astype(f32)
    vals, picks = [], []
    for _ in range(n_top):
        m = jnp.max(s, axis=0, keepdims=True)
        am = jnp.min(jnp.where(s == m, ids, 2.0 ** 30), axis=0, keepdims=True)
        hit = ids == am
        vals.append(m)
        picks.append(am if payload is None else jnp.max(jnp.where(hit, payload, -1.0), axis=0, keepdims=True))
        s = jnp.where(hit, NEG_INF, s)
    return jnp.concatenate(vals, axis=0), jnp.concatenate(picks, axis=0)


def _pair_candidates(v1, i1, v2, i2):
    n = PEER_TOPK
    sub = lax.broadcasted_iota(i32, (SUBLANES,) + v1.shape[1:], 0)
    as_f = lambda x: x.astype(f32)
    blocks = [(v1[0:1] + v2, as_f(lax.broadcasted_iota(i32, v2.shape, 0)), i1[0:1] * PEER_N_KEYS + i2)]
    for a in range(1, 5):
        blocks.append((v1[a:a + 1] + v2[:SUBLANES], as_f(a * n + sub), i1[a:a + 1] * PEER_N_KEYS + i2[:SUBLANES]))
    pick = lambda x: jnp.where(sub < 2, x[5:6], jnp.where(sub < 4, x[6:7], x[7:8]))
    alt = lambda x: jnp.where((sub & 1) == 0, x[0:1], x[1:2])
    a_of = jnp.where(sub < 2, 5, jnp.where(sub < 4, 6, 7))
    blocks.append((jnp.where(sub < 6, pick(v1) + alt(v2), NEG_INF), as_f(a_of * n + (sub & 1)),
                   pick(i1) * PEER_N_KEYS + alt(i2)))
    blocks.append((v1[SUBLANES:] + v2[0:1], as_f((sub + SUBLANES) * n), i1[SUBLANES:] * PEER_N_KEYS + i2[0:1]))
    return tuple(jnp.concatenate(parts, axis=0) for parts in zip(*blocks))


def _route_kernel(h_ref, wq_ref, k1_ref, k2_ref, row_ref, ne_ref, g_ref):
    q = _dot(h_ref[...].astype(bf16), wq_ref[...])
    dk = PEER_D_KEY // 2
    k1, k2 = k1_ref[...], k2_ref[...]
    for a in range(PEER_HEADS):
        qa = q[:, a * PEER_D_KEY:a * PEER_D_KEY + dk].astype(bf16)
        qb = q[:, a * PEER_D_KEY + dk:(a + 1) * PEER_D_KEY].astype(bf16)
        v1, i1 = _top_rows(_dot(k1, qa, _NT), PEER_TOPK)
        v2, i2 = _top_rows(_dot(k2, qb, _NT), PEER_TOPK)
        cand, ids, cidx = _pair_candidates(v1, i1, v2, i2)
        top, experts = _top_rows(cand, PEER_TOPK, ids=ids, payload=cidx)
        experts = experts.astype(i32)
        p = jnp.exp(top - top[0:1])
        gates = p / jnp.sum(p, axis=0, keepdims=True)
        for half in range(PEER_TOPK // SUBLANES):
            grp = slice(half * SUBLANES, (half + 1) * SUBLANES)
            e, g = experts[grp], gates[grp]
            odd = e & 1
            n_even = SUBLANES - jnp.sum(odd, axis=0, keepdims=True)
            sub = lax.broadcasted_iota(i32, e.shape, 0)
            evens_before = jnp.zeros_like(n_even)
            e_sorted, g_sorted = jnp.zeros_like(e), jnp.zeros_like(g)
            for r in range(SUBLANES):
                odd_r = odd[r:r + 1]
                dest = jnp.where(odd_r == 1, n_even + (r - evens_before), evens_before)
                hit = sub == dest
                e_sorted = jnp.where(hit, e[r:r + 1], e_sorted)
                g_sorted = jnp.where(hit, g[r:r + 1], g_sorted)
                evens_before = evens_before + (1 - odd_r)
            rows = slice(a * PEER_TOPK + half * SUBLANES, a * PEER_TOPK + (half + 1) * SUBLANES)
            row_ref[rows, :] = (e_sorted >> 1) * SUBLANES
            g_ref[rows, :] = g_sorted
            n = a * (PEER_TOPK // SUBLANES) + half
            ne_ref[n:n + 1, :] = n_even


def _route(h2, w_query, k1, k2, tm):
    T, D = h2.shape
    wq = w_query.astype(bf16)
    k1, k2 = k1.astype(bf16), k2.astype(bf16)
    hk = PEER_HEADS * PEER_TOPK
    full = lambda a: pl.BlockSpec(a.shape, lambda i: (0, 0))
    out = pl.BlockSpec((hk, tm), lambda i: (0, i))
    n_groups = hk // SUBLANES
    sd = jax.ShapeDtypeStruct
    return pl.pallas_call(
        _route_kernel, grid=(T // tm,),
        out_shape=[sd((hk, T), i32), sd((n_groups, T), i32), sd((hk, T), f32)],
        out_specs=[out, pl.BlockSpec((n_groups, tm), lambda i: (0, i)), out],
        in_specs=[pl.BlockSpec((tm, D), lambda i: (i, 0)), full(wq), full(k1), full(k2)],
        compiler_params=_params("parallel"), name="route",
    )(h2, wq, k1, k2)


def _pack_table(tab):
    n, d = tab.shape
    assert d == SUBLANES * LANES
    u = lax.bitcast_convert_type(tab.astype(bf16), jnp.uint16).astype(jnp.uint32).reshape(n // 2, 2, SUBLANES, LANES)
    return lax.bitcast_convert_type((u[:, 0] << 16) | u[:, 1], i32).reshape(n // 2 * SUBLANES, LANES)


def _shift_patterns():
    n_even = np.arange(SUBLANES + 1)[:, None, None]
    p = np.arange(SUBLANES)[None, :, None]
    pat = np.where(p < n_even, 0, 16) + np.zeros((1, 1, LANES), np.int64)
    return jnp.asarray(pat.reshape(-1, LANES), i32)


def _unpack(w, sh_ref, n_even, p):
    shift = jnp.broadcast_to(sh_ref[pl.ds(n_even * SUBLANES + p, 1), :], w.shape)
    return pltpu.bitcast(jnp.left_shift(w, shift) & jnp.int32(-65536), f32)


def _row(tab_ref, row0, sh_ref, n_even, p):
    return _unpack(tab_ref[pl.ds(pl.multiple_of(row0, SUBLANES), SUBLANES), :], sh_ref, n_even, p)


TOKEN_BATCH = 32
GROUP_UNROLL = 4


_TREE_ORDER = (0, 4, 2, 6, 1, 5, 3, 7)


def _packed(w):
    return pltpu.bitcast(w, bf16)


def _merge(p, q, mask, shift):
    moved = pltpu.roll(jnp.where(mask, q, p), shift, 0)
    return pltpu.bitcast(_packed(jnp.where(mask, p, q)) + _packed(moved), i32)


def _sublane_sums(ps):
    sub = lax.broadcasted_iota(i32, ps[0].shape, 0)
    quad = lambda a, b, o: _merge(a, b, ((sub - o) & 7) < 4, 4)
    duo = lambda a, b, o: _merge(a, b, ((sub - o) & 3) < 2, 6)
    r1 = duo(quad(ps[0], ps[1], 0), quad(ps[2], ps[3], 2), 0)
    r2 = duo(quad(ps[4], ps[5], 1), quad(ps[6], ps[7], 3), 1)
    return _merge(r1, r2, (sub & 1) == 0, 7)


def _peer_down_kernel(*refs):
    row_refs, (ne_ref, x_ref, gt_ref, sh_ref, tab_ref, ct_ref, part_ref, actt_ref) = refs[:SUBLANES], refs[SUBLANES:]
    hk, tb = ct_ref.shape
    n_groups = hk // SUBLANES
    lane = lax.broadcasted_iota(i32, (hk, tb), 1)

    def batch(b, carry):
        t0 = b * TOKEN_BATCH

        def token(u, c1):
            t = t0 + u
            xh = pltpu.bitcast(x_ref[t].astype(bf16).astype(f32), i32)
            xw = _packed(xh | lax.shift_right_logical(xh, 16))

            def group(g, c2):
                k0 = pl.multiple_of(g * SUBLANES, SUBLANES)
                tg = t * n_groups + g
                ps = []
                for k in _TREE_ORDER:
                    w = tab_ref[pl.ds(pl.multiple_of(row_refs[k][tg], SUBLANES), SUBLANES), :]
                    ps.append(pltpu.bitcast(_packed(w) * xw, i32))
                sums = _sublane_sums(ps)
                shift = sh_ref[pl.ds(pl.multiple_of(ne_ref[tg] * SUBLANES, SUBLANES), SUBLANES), :]
                part_ref[u, pl.ds(k0, SUBLANES), :] = pltpu.bitcast(jnp.left_shift(sums, shift) & jnp.int32(-65536), f32)
                return c2

            lax.fori_loop(0, n_groups, group, 0, unroll=4 * GROUP_UNROLL)
            return c1

        lax.fori_loop(0, TOKEN_BATCH, token, 0)
        a = actt_ref[...]
        for u in range(TOKEN_BATCH):
            a = jnp.where(lane == t0 + u, jnp.sum(part_ref[u], axis=1, keepdims=True), a)
        actt_ref[...] = a
        return carry

    lax.fori_loop(0, tb // TOKEN_BATCH, batch, 0)
    a = actt_ref[...]
    gelu = 0.5 * a * (1.0 + lax.erf(a * (2.0 ** -0.5)))
    ct_ref[...] = gt_ref[...] * gelu


def _peer_specs(tb, hk):
    per_group = pl.BlockSpec((tb * hk // SUBLANES,), lambda i: (i,), memory_space=pltpu.SMEM)
    whole = pl.BlockSpec(memory_space=pltpu.VMEM)
    return per_group, whole


def _peer_down(rows8, n_even, x3, gates_t, tab, tb):
    hk, T = gates_t.shape
    per_group, whole = _peer_specs(tb, hk)
    per_k = pl.BlockSpec((hk, tb), lambda i: (0, i))
    return pl.pallas_call(
        _peer_down_kernel, grid=(T // tb,), out_shape=jax.ShapeDtypeStruct((hk, T), f32), out_specs=per_k,
        in_specs=[per_group] * (SUBLANES + 1) + [pl.BlockSpec((tb,) + x3.shape[1:], lambda i: (i, 0, 0)), per_k,
                                                 whole, whole],
        scratch_shapes=[pltpu.VMEM((TOKEN_BATCH, hk, LANES), f32), pltpu.VMEM((hk, tb), f32)],
        compiler_params=_params("arbitrary"), name="peer_down",
    )(*rows8, n_even, x3, gates_t, _shift_patterns(), tab)


def _peer_up_kernel(*refs):
    row_refs, (ne_ref, ct_ref, sh_ref, tab_ref, o_ref, cx_ref) = refs[:SUBLANES], refs[SUBLANES:]
    hk, tb = ct_ref.shape
    n_groups = hk // SUBLANES
    n_acc = 4
    vreg = o_ref.shape[1:]
    lane = lax.broadcasted_iota(i32, (hk, tb), 1)

    def batch(b, carry):
        t0 = b * TOKEN_BATCH
        ct = ct_ref[...]
        for u in range(TOKEN_BATCH):
            col = jnp.sum(jnp.where(lane == t0 + u, ct, 0.0), axis=1, keepdims=True)
            cx_ref[u] = jnp.broadcast_to(col, (hk, LANES))

        def token(u, c1):
            t = t0 + u

            def fetch(g):
                tg = t * n_groups + g
                return tuple(tab_ref[pl.ds(pl.multiple_of(r[tg], SUBLANES), SUBLANES), :] for r in row_refs)

            def group(g, accs):
                accs, packed = list(accs), fetch(g)
                k0 = pl.multiple_of(g * SUBLANES, SUBLANES)
                n_even = ne_ref[t * n_groups + g]
                for p in range(SUBLANES):
                    coef = jnp.broadcast_to(cx_ref[u, pl.ds(k0 + p, 1), :], vreg)
                    accs[p % n_acc] = accs[p % n_acc] + _unpack(packed[p], sh_ref, n_even, p) * coef
                return tuple(accs)

            accs = lax.fori_loop(0, n_groups, group, (jnp.zeros(vreg, f32),) * n_acc, unroll=4 * GROUP_UNROLL)
            o_ref[t] = (accs[0] + accs[1]) + (accs[2] + accs[3])
            return c1

        lax.fori_loop(0, TOKEN_BATCH, token, 0)
        return carry

    lax.fori_loop(0, tb // TOKEN_BATCH, batch, 0)


def _peer_up(rows8, n_even, coef_t, tab, tb):
    hk, T = coef_t.shape
    per_group, whole = _peer_specs(tb, hk)
    return pl.pallas_call(
        _peer_up_kernel, grid=(T // tb,), out_shape=jax.ShapeDtypeStruct((T, SUBLANES, LANES), f32),
        out_specs=pl.BlockSpec((tb, SUBLANES, LANES), lambda i: (i, 0, 0)),
        in_specs=[per_group] * (SUBLANES + 1) + [pl.BlockSpec((hk, tb), lambda i: (0, i)), whole, whole],
        scratch_shapes=[pltpu.VMEM((TOKEN_BATCH, hk, LANES), f32)],
        compiler_params=_params("arbitrary"), name="peer_up",
    )(*rows8, n_even, coef_t, _shift_patterns(), tab)


def _ffn_out_kernel(h_ref, f_ref, g_ref, b_ref, o_ref, *, alpha):
    o_ref[...] = _layer_norm(alpha * h_ref[...] + f_ref[...], g_ref[...], b_ref[...])


def _ffn_out(h2, ff, g, b, alpha, tm):
    T, D = h2.shape
    g, b = g.reshape(1, D), b.reshape(1, D)
    tok = pl.BlockSpec((tm, D), lambda i: (i, 0))
    full = pl.BlockSpec((1, D), lambda i: (0, 0))
    return pl.pallas_call(
        functools.partial(_ffn_out_kernel, alpha=alpha), grid=(T // tm,),
        out_shape=jax.ShapeDtypeStruct((T, D), f32), out_specs=tok, in_specs=[tok, tok, full, full],
        compiler_params=_params("parallel"), name="ffn_out",
    )(h2, ff, g, b)


def _tile(n, want):
    t = min(n, want)
    assert n % t == 0, (n, t)
    return t


def kernel(x, positions, mem, w_in, gla_gate_up, gla_gate_bias, gla_norm_g, w_out, ln_mix_g, ln_mix_b, xattn_w_q, xattn_w_k, xattn_w_v, xattn_w_o, ln_mem_g, ln_mem_b, peer_w_query, peer_sub_keys_1, peer_sub_keys_2, peer_expert_down, peer_expert_up, ln_ffn_g, ln_ffn_b):
    B, S, D = x.shape
    T = B * S
    depth = w_in.shape[0]
    alpha = (2.0 * depth) ** 0.25
    tm = _tile(S, 512)
    h = x
    for l in range(depth):
        q, kt, v, qi, kit, misc, gq, gk, gv, la, gr = _proj(h, positions, w_in[l], gla_gate_up[l], gla_gate_bias[l], tm)
        y_dsa = _dsa(q, kt, v, qi, kit, misc, _tile(S, 128), _tile(S, 512))
        y_gla = _gla(gq, gk, gv, la, gr, gla_norm_g[l], tm)
        h1 = _mix_out(h.reshape(T, D), y_dsa.reshape(T, W_DSA), y_gla.reshape(T, W_GV), w_out[l],
                      ln_mix_g[l], ln_mix_b[l], alpha, tm)
        km, vm = _mem_kv(mem, xattn_w_k[l], xattn_w_v[l])
        h2 = _xattn(h1.reshape(B, S, D), km, vm, xattn_w_q[l], xattn_w_o[l], ln_mem_g[l], ln_mem_b[l], alpha, tm)
        h2 = h2.reshape(T, D)
        rows_t, ne_t, gates_t = _route(h2, peer_w_query[l], peer_sub_keys_1[l], peer_sub_keys_2[l], _tile(T, 256))
        flat = lambda a: a.T.reshape(-1)
        n_even = flat(ne_t)
        by_pos = rows_t.reshape(-1, SUBLANES, T).transpose(1, 2, 0).reshape(SUBLANES, -1)
        rows8 = [by_pos[p] for p in range(SUBLANES)]
        tb = _tile(T, LANES)
        coef_t = _peer_down(rows8, n_even, h2.reshape(T, SUBLANES, LANES), gates_t, _pack_table(peer_expert_down[l]), tb)
        ff = _peer_up(rows8, n_even, coef_t, _pack_table(peer_expert_up[l]), tb)
        h = _ffn_out(h2, ff.reshape(T, D), ln_ffn_g[l], ln_ffn_b[l], alpha, tm).reshape(B, S, D)
    return h
```

```python
import functools

import jax
import jax.numpy as jnp
import numpy as np
from jax import lax
from jax.experimental import pallas as pl
from jax.experimental.pallas import tpu as pltpu

f32 = jnp.float32
bf16 = jnp.bfloat16
i32 = jnp.int32

DSA_HEADS = 8
DSA_HEAD_DIM = 64
IDX_HEADS = 8
IDX_DIM = 32
TOPK_MAX = 256
GLA_HEADS = 4
GLA_DK = 64
GLA_DV = 128
GLA_GATE_RANK = 16
GLA_GATE_TEMP = 16.0
GLA_CHUNK = 64
ROPE_THETA = 500000.0
ROPE_FRACTION = 4
XATTN_HEADS = 4
PEER_N_KEYS = 128
PEER_HEADS = 8
PEER_D_KEY = 256
PEER_TOPK = 16
LN_EPS = 1e-5
RMS_EPS = 1e-6

LANES = 128
SUBLANES = 8
VMEM_LIMIT = 56 * 1024 * 1024

TOKEN_TILE = 512
DSA_QUERY_BLOCK = LANES
DSA_KEY_CHUNK = 512
ROUTE_TILE = 256
PEER_TOKEN_BLOCK = LANES

INT_MIN = -(2 ** 31)
NEG_INF = float("-inf")

W_DSA = DSA_HEADS * DSA_HEAD_DIM
W_IDX = IDX_HEADS * IDX_DIM
W_GQK = GLA_HEADS * GLA_DK
W_GV = GLA_HEADS * GLA_DV
MISC_WI = IDX_DIM
MISC_LR = IDX_DIM + IDX_HEADS


def _dot(a, b, dims=(((1,), (0,)), ((), ())), precision=None):
    return lax.dot_general(a, b, dims, precision=precision, preferred_element_type=f32)


_NT = (((1,), (1,)), ((), ()))
_TN = (((0,), (0,)), ((), ()))


def _params(*sem):
    return pltpu.CompilerParams(dimension_semantics=sem, vmem_limit_bytes=VMEM_LIMIT)


def _layer_norm(y, g, b):
    mu = jnp.mean(y, axis=-1, keepdims=True)
    yc = y - mu
    var = jnp.mean(yc * yc, axis=-1, keepdims=True)
    return yc * lax.rsqrt(var + LN_EPS) * g + b


def _rot(xb, c, sa, sb, half):
    return xb * c + pltpu.roll(xb, LANES - half, 1) * sa + pltpu.roll(xb, half, 1) * sb


def _proj_kernel(x_ref, w_ref, gup_ref, gb_ref, cq_ref, saq_ref, sbq_ref, ci_ref, sai_ref, sbi_ref,
                 q_ref, kt_ref, v_ref, qi_ref, kit_ref, misc_ref, gq_ref, gk_ref, gv_ref, la_ref, gr_ref):
    x = x_ref[...].astype(bf16)
    tm = x.shape[0]
    cq, saq, sbq = cq_ref[...], saq_ref[...], sbq_ref[...]
    ci, sai, sbi = ci_ref[...], sai_ref[...], sbi_ref[...]
    hq = DSA_HEAD_DIM // ROPE_FRACTION // 2
    hi = IDX_DIM // ROPE_FRACTION // 2
    o = 0
    scale = DSA_HEAD_DIM ** -0.5 * np.log2(np.e)
    for j in range(W_DSA // LANES):
        a = _dot(x, w_ref[:, o + j * LANES:o + (j + 1) * LANES])
        q_ref[:, j * LANES:(j + 1) * LANES] = (_rot(a, cq, saq, sbq, hq) * scale).astype(bf16)
    o += W_DSA
    for j in range(W_DSA // LANES):
        a = _dot(x, w_ref[:, o + j * LANES:o + (j + 1) * LANES])
        kt_ref[j * LANES:(j + 1) * LANES, :] = _rot(a, cq, saq, sbq, hq).T.astype(bf16)
    o += W_DSA
    v_ref[...] = _dot(x, w_ref[:, o:o + W_DSA]).astype(bf16)
    o += W_DSA
    for j in range(W_IDX // LANES):
        a = _dot(x, w_ref[:, o + j * LANES:o + (j + 1) * LANES])
        qi_ref[:, j * LANES:(j + 1) * LANES] = _rot(a, ci, sai, sbi, hi).astype(bf16)
    o += W_IDX
    m = _dot(x, w_ref[:, o:o + LANES])
    lane = lax.broadcasted_iota(i32, (tm, LANES), 1)
    is_ki = lane < IDX_DIM
    m = _rot(m, jnp.where(is_ki, ci, 1.0), jnp.where(is_ki, sai, 0.0), jnp.where(is_ki, sbi, 0.0), hi)
    misc_ref[...] = m
    kit_ref[...] = m.T[:IDX_DIM, :].astype(bf16)
    z = _dot(m.astype(bf16), gup_ref[...]) + gb_ref[...]
    la_ref[...] = (jnp.minimum(z, 0.0) - jnp.log1p(jnp.exp(-jnp.abs(z)))) / GLA_GATE_TEMP
    o += LANES
    gq_ref[...] = _dot(x, w_ref[:, o:o + W_GQK])
    o += W_GQK
    gk_ref[...] = _dot(x, w_ref[:, o:o + W_GQK])
    o += W_GQK
    gv_ref[...] = _dot(x, w_ref[:, o:o + W_GV]).astype(bf16)
    o += W_GV
    gr_ref[...] = _dot(x, w_ref[:, o:o + W_GV])


def _rot_tables(positions, head_dim):
    r = head_dim // ROPE_FRACTION
    half = r // 2
    inv_freq = ROPE_THETA ** (-jnp.arange(half, dtype=f32) / half)
    ang = positions.astype(f32)[..., None] * inv_freq
    cos, sin = jnp.cos(ang), jnp.sin(ang)
    lane = np.arange(LANES) % head_dim
    src = np.where(lane < half, lane, np.clip(lane - half, 0, half - 1))
    cos_l, sin_l = cos[..., src], sin[..., src]
    c = jnp.where(lane < r, cos_l, 1.0)
    sa = jnp.where(lane < half, -sin_l, 0.0)
    sb = jnp.where((lane >= half) & (lane < r), sin_l, 0.0)
    return c, sa, sb


def _proj(x, positions, w_in, gate_up, gate_bias, tm):
    B, S, D = x.shape
    splits = np.cumsum([W_DSA, W_DSA, W_DSA, W_IDX, IDX_DIM, IDX_HEADS, W_GQK, W_GQK, W_GV, GLA_GATE_RANK])
    (wq, wk, wv, wqi, wki, wwi, wgq, wgk, wgv, wlr, wgr) = jnp.split(w_in, splits.tolist(), axis=1)
    pad = jnp.zeros((D, LANES - IDX_DIM - IDX_HEADS - GLA_GATE_RANK), w_in.dtype)
    w_a = jnp.concatenate([wq, wk, wv, wqi, wki, wwi, wlr, pad, wgq, wgk, wgv, wgr], axis=1).astype(bf16)
    gup = jnp.zeros((LANES, W_GQK), f32).at[MISC_LR:MISC_LR + GLA_GATE_RANK].set(gate_up).astype(bf16)
    tabs = _rot_tables(positions, DSA_HEAD_DIM) + _rot_tables(positions, IDX_DIM)
    W = w_a.shape[1]
    tok = lambda n: pl.BlockSpec((None, tm, n), lambda b, j: (b, j, 0))
    full = lambda a: pl.BlockSpec(a.shape, lambda b, j: (0,) * a.ndim)
    tr = lambda n: pl.BlockSpec((None, n, tm), lambda b, j: (b, 0, j))
    sd = jax.ShapeDtypeStruct
    out_shape = [sd((B, S, W_DSA), bf16), sd((B, W_DSA, S), bf16), sd((B, S, W_DSA), bf16), sd((B, S, W_IDX), bf16),
                 sd((B, IDX_DIM, S), bf16), sd((B, S, LANES), f32), sd((B, S, W_GQK), f32), sd((B, S, W_GQK), f32),
                 sd((B, S, W_GV), bf16), sd((B, S, W_GQK), f32), sd((B, S, W_GV), f32)]
    out_specs = [tok(W_DSA), tr(W_DSA), tok(W_DSA), tok(W_IDX), tr(IDX_DIM), tok(LANES), tok(W_GQK), tok(W_GQK),
                 tok(W_GV), tok(W_GQK), tok(W_GV)]
    gb = gate_bias.reshape(1, W_GQK)
    return pl.pallas_call(
        _proj_kernel, grid=(B, S // tm), out_shape=out_shape, out_specs=out_specs,
        in_specs=[tok(D), full(w_a), full(gup), full(gb)] + [tok(LANES)] * 6,
        compiler_params=_params("parallel", "parallel"), name="proj",
    )(x, w_a, gup, gb, *tabs)


def _dsa_kernel(q_ref, kt_ref, v_ref, qi_ref, kit_ref, misc_ref, o_ref, key_ref, keyt_ref, bias_ref, lg_ref, mx_ref,
                ls_ref, acc_ref, *, n_sel, idx_bits, kc):
    qb, S = q_ref.shape[0], kt_ref.shape[1]
    i = pl.program_id(1)
    nk = lax.div((i + 1) * qb + (kc - 1), kc)
    assert qb == LANES
    qi = qi_ref[...]
    wi = misc_ref[:, MISC_WI:MISC_WI + IDX_HEADS] * (IDX_HEADS ** -0.5) * (IDX_DIM ** -0.5)
    qpos = lax.broadcasted_iota(i32, (qb, 1), 0) + i * qb
    lane = lax.broadcasted_iota(i32, (qb, kc), 1)

    def chunk(c):
        return pl.ds(pl.multiple_of(c * kc, kc), kc)

    qi_heads = [qi[:, h * IDX_DIM:(h + 1) * IDX_DIM] for h in range(IDX_HEADS)]

    def score_chunk(c, carry):
        kit = kit_ref[:, chunk(c)]
        score = jnp.zeros((qb, kc), f32)
        for h in range(IDX_HEADS):
            d = _dot(qi_heads[h], kit)
            score = score + wi[:, h:h + 1] * jnp.maximum(d, 0.0)
        score = jnp.where(score == 0.0, 0.0, score)
        bits = pltpu.bitcast(score, i32)
        key = bits ^ ((bits >> 31) & jnp.int32(0x7FFFFFFF))
        key = jnp.where(lane + c * kc <= qpos, key, INT_MIN)
        key_ref[:, chunk(c)] = key
        for j in range(kc // LANES):
            keyt_ref[pl.ds(pl.multiple_of(c * kc + j * LANES, LANES), LANES), :] = key[:, j * LANES:(j + 1) * LANES].T
        return carry

    lax.fori_loop(0, nk, score_chunk, 0)

    qpos_row = lax.broadcasted_iota(i32, (1, qb), 1) + i * qb
    target = jnp.minimum(n_sel, qpos_row + 1).astype(f32)
    key_row = lax.broadcasted_iota(i32, (kc, qb), 0)
    n_part = 4

    def count(pred):
        def body(c, accs):
            hit = jnp.where(pred(keyt_ref[chunk(c), :], key_row + c * kc), 1.0, 0.0)
            accs = list(accs)
            for j in range(kc // SUBLANES):
                accs[j % n_part] = accs[j % n_part] + hit[j * SUBLANES:(j + 1) * SUBLANES]
            return tuple(accs)

        accs = lax.fori_loop(0, nk, body, (jnp.zeros((SUBLANES, qb), f32),) * n_part)
        return jnp.sum((accs[0] + accs[1]) + (accs[2] + accs[3]), axis=0, keepdims=True)

    base = jnp.where(count(lambda k, r: k >= 0) >= target, jnp.int32(0), jnp.int32(INT_MIN))

    def tau_bit(t, base):
        cand = base | jnp.left_shift(jnp.int32(1), 30 - t)
        return jnp.where(count(lambda k, r: k >= cand) >= target, cand, base)

    tau_row = lax.fori_loop(0, 31, tau_bit, base)
    excess = jnp.max(count(lambda k, r: k >= tau_row) - target)

    def tie_limit():
        need = target - count(lambda k, r: k > tau_row)

        def idx_bit(t, m):
            cand = m | jnp.left_shift(jnp.int32(1), idx_bits - 1 - t)
            return jnp.where(count(lambda k, r: (k == tau_row) & (r < cand)) < need, cand, m)

        return lax.fori_loop(0, idx_bits, idx_bit, jnp.zeros((1, qb), i32))

    m_row = lax.cond(excess > 0.0, tie_limit, lambda: jnp.full((1, qb), S, i32))
    to_col = lambda row: jnp.broadcast_to(row, (qb, qb)).T[:, 0:1]
    tau, m_idx = to_col(tau_row), to_col(m_row)

    def bias_chunk(c, carry):
        k = key_ref[:, chunk(c)]
        sel = (k > tau) | ((k == tau) & (lane + c * kc <= m_idx))
        bias_ref[:, chunk(c)] = jnp.where(sel, 0.0, NEG_INF)
        return carry

    lax.fori_loop(0, nk, bias_chunk, 0)

    q = q_ref[...]
    pair = LANES // DSA_HEAD_DIM

    def fold(x, op, acc):
        for j in range(kc // LANES):
            acc = op(acc, x[:, j * LANES:(j + 1) * LANES])
        return acc

    heads = range(DSA_HEADS)
    head_rows = [slice(h * DSA_HEAD_DIM, (h + 1) * DSA_HEAD_DIM) for h in heads]
    mx_ref[...] = jnp.full(mx_ref.shape, NEG_INF, f32)
    ls_ref[...] = jnp.zeros_like(ls_ref)
    acc_ref[...] = jnp.zeros_like(acc_ref)

    q_heads = [q[:, head_rows[h]] for h in heads]

    def logits(c, carry):
        bias = bias_ref[:, chunk(c)]
        for h in heads:
            lg = _dot(q_heads[h], kt_ref[head_rows[h], chunk(c)]) + bias
            lg_ref[h, :, chunk(c)] = lg
            mx_ref[h] = fold(lg, jnp.maximum, mx_ref[h])
        return carry

    lax.fori_loop(0, nk, logits, 0)
    ms = [jnp.max(mx_ref[h], axis=1, keepdims=True) for h in heads]

    def weigh(c, carry):
        for h in heads:
            p = jnp.exp2(lg_ref[h, :, chunk(c)] - ms[h])
            ls_ref[h] = fold(p, jnp.add, ls_ref[h])
            slab = slice((h // pair) * LANES, (h // pair + 1) * LANES)
            acc_ref[h] = acc_ref[h] + _dot(p.astype(bf16), v_ref[chunk(c), slab])
        return carry

    lax.fori_loop(0, nk, weigh, 0)
    for h in heads:
        l = jnp.sum(ls_ref[h], axis=1, keepdims=True)
        off = (h % pair) * DSA_HEAD_DIM
        o_ref[:, head_rows[h]] = (acc_ref[h][:, off:off + DSA_HEAD_DIM] / l).astype(o_ref.dtype)


def _dsa(q, kt, v, qi, kit, misc, qb, kc):
    B, S, _ = q.shape
    n_sel = min(TOPK_MAX, S // 4)
    blk = lambda n: pl.BlockSpec((None, qb, n), lambda b, i: (b, i, 0))
    per_b = lambda r, c: pl.BlockSpec((None, r, c), lambda b, i: (b, 0, 0))
    kern = functools.partial(_dsa_kernel, n_sel=n_sel, idx_bits=max(1, (S - 1).bit_length()), kc=kc)
    return pl.pallas_call(
        kern, grid=(B, S // qb), out_shape=jax.ShapeDtypeStruct((B, S, W_DSA), bf16), out_specs=blk(W_DSA),
        in_specs=[blk(W_DSA), per_b(W_DSA, S), per_b(S, W_DSA), blk(W_IDX), per_b(IDX_DIM, S), blk(LANES)],
        scratch_shapes=[pltpu.VMEM((qb, S), i32), pltpu.VMEM((S, qb), i32), pltpu.VMEM((qb, S), f32),
                        pltpu.VMEM((DSA_HEADS, qb, S), f32)]
        + [pltpu.VMEM((DSA_HEADS, qb, LANES), f32)] * 3,
        compiler_params=_params("parallel", "arbitrary"), name="dsa",
    )(q, kt, v, qi, kit, misc)


def _gla_kernel(gq_ref, gk_ref, gv_ref, la_ref, gr_ref, ng_ref, o_ref, state_ref):
    ct = gq_ref.shape[0]
    nch = ct // GLA_CHUNK

    @pl.when(pl.program_id(1) == 0)
    def _():
        state_ref[...] = jnp.zeros_like(state_ref)

    la = la_ref[...]
    r = lax.broadcasted_iota(i32, (ct, ct), 0)
    c = lax.broadcasted_iota(i32, (ct, ct), 1)
    same = (r // GLA_CHUNK) == (c // GLA_CHUNK)
    causal = same & (c <= r)
    tri = jnp.where(causal, 1.0, 0.0).astype(bf16)
    la_hi = la.astype(bf16)
    rest = la - la_hi.astype(f32)
    la_mid = rest.astype(bf16)
    la_lo = (rest - la_mid.astype(f32)).astype(bf16)
    bcum = (_dot(tri, la_hi) + _dot(tri, la_mid)) + _dot(tri, la_lo)
    blast = jnp.concatenate([jnp.broadcast_to(bcum[(n + 1) * GLA_CHUNK - 1:(n + 1) * GLA_CHUNK], (GLA_CHUNK, la.shape[1]))
                             for n in range(nch)], axis=0)
    q_dec = (gq_ref[...] * (GLA_DK ** -0.5) * jnp.exp(bcum)).astype(bf16)
    k_inv = (gk_ref[...] * jnp.exp(-bcum)).astype(bf16)
    k_end = (gk_ref[...] * jnp.exp(blast - bcum)).astype(bf16)
    decay = jnp.exp(blast)
    ng = ng_ref[...]
    for h in range(GLA_HEADS):
        ks = slice(h * GLA_DK, (h + 1) * GLA_DK)
        vs = slice(h * GLA_DV, (h + 1) * GLA_DV)
        qd, ki, ke, vh = q_dec[:, ks], k_inv[:, ks], k_end[:, ks], gv_ref[:, vs]
        attn = jnp.where(causal, _dot(qd, ki, _NT), 0.0)
        o = _dot(attn.astype(bf16), vh)
        st = state_ref[h]
        inter = []
        for n in range(nch):
            rows = slice(n * GLA_CHUNK, (n + 1) * GLA_CHUNK)
            inter.append(_dot(qd[rows], st.astype(bf16), _NT))
            st = st * decay[n * GLA_CHUNK:n * GLA_CHUNK + 1, ks] + _dot(vh[rows], ke[rows], _TN)
        state_ref[h] = st
        o = o + jnp.concatenate(inter, axis=0)
        o = o * lax.rsqrt(jnp.mean(o * o, axis=-1, keepdims=True) + RMS_EPS) * ng
        g = gr_ref[:, vs]
        o_ref[:, vs] = (o * (g * jax.nn.sigmoid(g))).astype(o_ref.dtype)


def _gla(gq, gk, gv, la, gr, norm_g, ct):
    B, S, _ = gq.shape
    blk = lambda n: pl.BlockSpec((None, ct, n), lambda b, j: (b, j, 0))
    ng = norm_g.reshape(1, GLA_DV)
    return pl.pallas_call(
        _gla_kernel, grid=(B, S // ct), out_shape=jax.ShapeDtypeStruct((B, S, W_GV), bf16), out_specs=blk(W_GV),
        in_specs=[blk(W_GQK), blk(W_GQK), blk(W_GV), blk(W_GQK), blk(W_GV), pl.BlockSpec(ng.shape, lambda b, j: (0, 0))],
        scratch_shapes=[pltpu.VMEM((GLA_HEADS, GLA_DV, GLA_DK), f32)],
        compiler_params=_params("parallel", "arbitrary"), name="gla",
    )(gq, gk, gv, la, gr, ng)


def _mix_out_kernel(x_ref, ya_ref, yb_ref, wa_ref, wb_ref, g_ref, b_ref, o_ref, *, alpha):
    mix = _dot(ya_ref[...], wa_ref[...]) + _dot(yb_ref[...], wb_ref[...])
    o_ref[...] = _layer_norm(alpha * x_ref[...] + mix, g_ref[...], b_ref[...])


def _mix_out(x2, ya, yb, w_out, g, b, alpha, tm):
    T, D = x2.shape
    wa, wb = w_out[:W_DSA].astype(bf16), w_out[W_DSA:].astype(bf16)
    tok = lambda n: pl.BlockSpec((tm, n), lambda i: (i, 0))
    full = lambda a: pl.BlockSpec(a.shape, lambda i: (0, 0))
    g, b = g.reshape(1, D), b.reshape(1, D)
    return pl.pallas_call(
        functools.partial(_mix_out_kernel, alpha=alpha), grid=(T // tm,),
        out_shape=jax.ShapeDtypeStruct((T, D), f32), out_specs=tok(D),
        in_specs=[tok(D), tok(W_DSA), tok(W_GV), full(wa), full(wb), full(g), full(b)],
        compiler_params=_params("parallel"), name="mix_out",
    )(x2, ya, yb, wa, wb, g, b)


def _mem_kv_kernel(m_ref, wk_ref, wv_ref, k_ref, v_ref):
    m = m_ref[...].astype(bf16)
    k_ref[...] = _dot(m, wk_ref[...]).astype(bf16)
    v_ref[...] = _dot(m, wv_ref[...]).astype(bf16)


def _mem_kv(mem, w_k, w_v):
    B, M, D = mem.shape
    wk, wv = w_k.astype(bf16), w_v.astype(bf16)
    blk = pl.BlockSpec((None, M, D), lambda b: (b, 0, 0))
    full = pl.BlockSpec((D, D), lambda b: (0, 0))
    sd = jax.ShapeDtypeStruct((B, M, D), bf16)
    return pl.pallas_call(_mem_kv_kernel, grid=(B,), out_shape=[sd, sd], out_specs=[blk, blk],
                          in_specs=[blk, full, full], compiler_params=_params("parallel"), name="mem_kv")(mem, wk, wv)


def _xattn_kernel(h_ref, k_ref, v_ref, wq_ref, wo_ref, g_ref, b_ref, o_ref, *, alpha):
    h = h_ref[...]
    D = h.shape[1]
    hd = D // XATTN_HEADS
    q = (_dot(h.astype(bf16), wq_ref[...]) * (hd ** -0.5)).astype(bf16)
    outs = []
    for a in range(XATTN_HEADS):
        s = slice(a * hd, (a + 1) * hd)
        lg = _dot(q[:, s], k_ref[:, s], _NT)
        p = jnp.exp(lg - jnp.max(lg, axis=1, keepdims=True))
        l = jnp.sum(p, axis=1, keepdims=True)
        outs.append((_dot(p.astype(bf16), v_ref[:, s]) / l).astype(bf16))
    ca = _dot(jnp.concatenate(outs, axis=1), wo_ref[...])
    o_ref[...] = _layer_norm(alpha * h + ca, g_ref[...], b_ref[...])


def _xattn(h1, km, vm, w_q, w_o, g, b, alpha, tm):
    B, S, D = h1.shape
    M = km.shape[1]
    wq, wo = w_q.astype(bf16), w_o.astype(bf16)
    g, b = g.reshape(1, D), b.reshape(1, D)
    tok = pl.BlockSpec((None, tm, D), lambda bi, j: (bi, j, 0))
    per_b = pl.BlockSpec((None, M, D), lambda bi, j: (bi, 0, 0))
    full = lambda a: pl.BlockSpec(a.shape, lambda bi, j: (0, 0))
    return pl.pallas_call(
        functools.partial(_xattn_kernel, alpha=alpha), grid=(B, S // tm),
        out_shape=jax.ShapeDtypeStruct((B, S, D), f32), out_specs=tok,
        in_specs=[tok, per_b, per_b, full(wq), full(wo), full(g), full(b)],
        compiler_params=_params("parallel", "parallel"), name="xattn",
    )(h1, km, vm, wq, wo, g, b)


def _top_rows(s, n_top, ids=None, payload=None):
    if ids is None:
        ids = lax.broadcasted_iota(i32, s.shape, 0).astype(f32)
    vals, picks = [], []
    for _ in range(n_top):
        m = jnp.max(s, axis=0, keepdims=True)
        am = jnp.min(jnp.where(s == m, ids, 2.0 ** 30), axis=0, keepdims=True)
        hit = ids == am
        vals.append(m)
        picks.append(am if payload is None else jnp.max(jnp.where(hit, payload, -1.0), axis=0, keepdims=True))
        s = jnp.where(hit, NEG_INF, s)
    return jnp.concatenate(vals, axis=0), jnp.concatenate(picks, axis=0)


def _pair_candidates(v1, i1, v2, i2):
    n = PEER_TOPK
    sub = lax.broadcasted_iota(i32, (SUBLANES,) + v1.shape[1:], 0)
    as_f = lambda x: x.astype(f32)
    blocks = [(v1[0:1] + v2, as_f(lax.broadcasted_iota(i32, v2.shape, 0)), i1[0:1] * PEER_N_KEYS + i2)]
    for a in range(1, 5):
        blocks.append((v1[a:a + 1] + v2[:SUBLANES], as_f(a * n + sub), i1[a:a + 1] * PEER_N_KEYS + i2[:SUBLANES]))
    pick = lambda x: jnp.where(sub < 2, x[5:6], jnp.where(sub < 4, x[6:7], x[7:8]))
    alt = lambda x: jnp.where((sub & 1) == 0, x[0:1], x[1:2])
    a_of = jnp.where(sub < 2, 5, jnp.where(sub < 4, 6, 7))
    blocks.append((jnp.where(sub < 6, pick(v1) + alt(v2), NEG_INF), as_f(a_of * n + (sub & 1)),
                   pick(i1) * PEER_N_KEYS + alt(i2)))
    blocks.append((v1[SUBLANES:] + v2[0:1], as_f((sub + SUBLANES) * n), i1[SUBLANES:] * PEER_N_KEYS + i2[0:1]))
    return tuple(jnp.concatenate(parts, axis=0) for parts in zip(*blocks))


def _route_kernel(h_ref, wq_ref, k1_ref, k2_ref, row_ref, ne_ref, g_ref):
    q = _dot(h_ref[...].astype(bf16), wq_ref[...])
    dk = PEER_D_KEY // 2
    k1, k2 = k1_ref[...], k2_ref[...]
    for a in range(PEER_HEADS):
        qa = q[:, a * PEER_D_KEY:a * PEER_D_KEY + dk].astype(bf16)
        qb = q[:, a * PEER_D_KEY + dk:(a + 1) * PEER_D_KEY].astype(bf16)
        v1, i1 = _top_rows(_dot(k1, qa, _NT), PEER_TOPK)
        v2, i2 = _top_rows(_dot(k2, qb, _NT), PEER_TOPK)
        cand, ids, cidx = _pair_candidates(v1, i1, v2, i2)
        top, experts = _top_rows(cand, PEER_TOPK, ids=ids, payload=cidx)
        experts = experts.astype(i32)
        p = jnp.exp(top - top[0:1])
        gates = p / jnp.sum(p, axis=0, keepdims=True)
        for half in range(PEER_TOPK // SUBLANES):
            grp = slice(half * SUBLANES, (half + 1) * SUBLANES)
            e, g = experts[grp], gates[grp]
            odd = e & 1
            n_even = SUBLANES - jnp.sum(odd, axis=0, keepdims=True)
            sub = lax.broadcasted_iota(i32, e.shape, 0)
            evens_before = jnp.zeros_like(n_even)
            e_sorted, g_sorted = jnp.zeros_like(e), jnp.zeros_like(g)
            for r in range(SUBLANES):
                odd_r = odd[r:r + 1]
                dest = jnp.where(odd_r == 1, n_even + (r - evens_before), evens_before)
                hit = sub == dest
                e_sorted = jnp.where(hit, e[r:r + 1], e_sorted)
                g_sorted = jnp.where(hit, g[r:r + 1], g_sorted)
                evens_before = evens_before + (1 - odd_r)
            rows = slice(a * PEER_TOPK + half * SUBLANES, a * PEER_TOPK + (half + 1) * SUBLANES)
            row_ref[rows, :] = (e_sorted >> 1) * SUBLANES
            g_ref[rows, :] = g_sorted
            n = a * (PEER_TOPK // SUBLANES) + half
            ne_ref[n:n + 1, :] = n_even


def _route(h2, w_query, k1, k2, tm):
    T, D = h2.shape
    wq = w_query.astype(bf16)
    k1, k2 = k1.astype(bf16), k2.astype(bf16)
    hk = PEER_HEADS * PEER_TOPK
    full = lambda a: pl.BlockSpec(a.shape, lambda i: (0, 0))
    out = pl.BlockSpec((hk, tm), lambda i: (0, i))
    n_groups = hk // SUBLANES
    sd = jax.ShapeDtypeStruct
    return pl.pallas_call(
        _route_kernel, grid=(T // tm,),
        out_shape=[sd((hk, T), i32), sd((n_groups, T), i32), sd((hk, T), f32)],
        out_specs=[out, pl.BlockSpec((n_groups, tm), lambda i: (0, i)), out],
        in_specs=[pl.BlockSpec((tm, D), lambda i: (i, 0)), full(wq), full(k1), full(k2)],
        compiler_params=_params("parallel"), name="route",
    )(h2, wq, k1, k2)


def _pack_table(tab):
    n, d = tab.shape
    assert d == SUBLANES * LANES
    u = lax.bitcast_convert_type(tab.astype(bf16), jnp.uint16).astype(jnp.uint32).reshape(n // 2, 2, SUBLANES, LANES)
    return lax.bitcast_convert_type((u[:, 0] << 16) | u[:, 1], i32).reshape(n // 2 * SUBLANES, LANES)


def _shift_patterns():
    n_even = np.arange(SUBLANES + 1)[:, None, None]
    p = np.arange(SUBLANES)[None, :, None]
    pat = np.where(p < n_even, 0, 16) + np.zeros((1, 1, LANES), np.int64)
    return jnp.asarray(pat.reshape(-1, LANES), i32)


def _unpack(w, sh_ref, n_even, p):
    shift = jnp.broadcast_to(sh_ref[pl.ds(n_even * SUBLANES + p, 1), :], w.shape)
    return pltpu.bitcast(jnp.left_shift(w, shift) & jnp.int32(-65536), f32)


TOKEN_BATCH = 32
GROUP_UNROLL = 16


_TREE_ORDER = (0, 4, 2, 6, 1, 5, 3, 7)


def _packed(w):
    return pltpu.bitcast(w, bf16)


def _merge(p, q, mask, shift):
    moved = pltpu.roll(jnp.where(mask, q, p), shift, 0)
    return pltpu.bitcast(_packed(jnp.where(mask, p, q)) + _packed(moved), i32)


def _sublane_sums(ps):
    sub = lax.broadcasted_iota(i32, ps[0].shape, 0)
    quad = lambda a, b, o: _merge(a, b, ((sub - o) & 7) < 4, 4)
    duo = lambda a, b, o: _merge(a, b, ((sub - o) & 3) < 2, 6)
    r1 = duo(quad(ps[0], ps[1], 0), quad(ps[2], ps[3], 2), 0)
    r2 = duo(quad(ps[4], ps[5], 1), quad(ps[6], ps[7], 3), 1)
    return _merge(r1, r2, (sub & 1) == 0, 7)


def _peer_down_kernel(*refs):
    row_refs, (ne_ref, x_ref, gt_ref, sh_ref, tab_ref, ct_ref, part_ref, actt_ref) = refs[:SUBLANES], refs[SUBLANES:]
    hk, tb = ct_ref.shape
    n_groups = hk // SUBLANES
    lane = lax.broadcasted_iota(i32, (hk, tb), 1)
    actt_ref[...] = jnp.zeros_like(actt_ref)

    def batch(b, carry):
        t0 = b * TOKEN_BATCH

        def token(u, c1):
            t = t0 + u
            xh = pltpu.bitcast(x_ref[t].astype(bf16).astype(f32), i32)
            xw = _packed(xh | lax.shift_right_logical(xh, 16))

            def group(g, c2):
                k0 = pl.multiple_of(g * SUBLANES, SUBLANES)
                tg = t * n_groups + g
                ps = []
                for k in _TREE_ORDER:
                    w = tab_ref[pl.ds(pl.multiple_of(row_refs[k][tg], SUBLANES), SUBLANES), :]
                    ps.append(pltpu.bitcast(_packed(w) * xw, i32))
                sums = _sublane_sums(ps)
                shift = sh_ref[pl.ds(pl.multiple_of(ne_ref[tg] * SUBLANES, SUBLANES), SUBLANES), :]
                part_ref[u, pl.ds(k0, SUBLANES), :] = pltpu.bitcast(jnp.left_shift(sums, shift) & jnp.int32(-65536), f32)
                return c2

            lax.fori_loop(0, n_groups, group, 0, unroll=GROUP_UNROLL)
            return c1

        lax.fori_loop(0, TOKEN_BATCH, token, 0)
        a = actt_ref[...]
        for u in range(TOKEN_BATCH):
            a = jnp.where(lane == t0 + u, jnp.sum(part_ref[u], axis=1, keepdims=True), a)
        actt_ref[...] = a
        return carry

    lax.fori_loop(0, tb // TOKEN_BATCH, batch, 0)
    a = actt_ref[...]
    gelu = 0.5 * a * (1.0 + lax.erf(a * (2.0 ** -0.5)))
    ct_ref[...] = gt_ref[...] * gelu


def _peer_specs(tb, hk):
    per_group = pl.BlockSpec((tb * hk // SUBLANES,), lambda i: (i,), memory_space=pltpu.SMEM)
    whole = pl.BlockSpec(memory_space=pltpu.VMEM)
    return per_group, whole


def _peer_down(rows8, n_even, x3, gates_t, tab, tb):
    hk, T = gates_t.shape
    per_group, whole = _peer_specs(tb, hk)
    per_k = pl.BlockSpec((hk, tb), lambda i: (0, i))
    return pl.pallas_call(
        _peer_down_kernel, grid=(T // tb,), out_shape=jax.ShapeDtypeStruct((hk, T), f32), out_specs=per_k,
        in_specs=[per_group] * (SUBLANES + 1) + [pl.BlockSpec((tb,) + x3.shape[1:], lambda i: (i, 0, 0)), per_k,
                                                 whole, whole],
        scratch_shapes=[pltpu.VMEM((TOKEN_BATCH, hk, LANES), f32), pltpu.VMEM((hk, tb), f32)],
        compiler_params=_params("arbitrary"), name="peer_down",
    )(*rows8, n_even, x3, gates_t, _shift_patterns(), tab)


def _peer_up_kernel(*refs):
    row_refs, (ne_ref, ct_ref, sh_ref, tab_ref, o_ref, cx_ref) = refs[:SUBLANES], refs[SUBLANES:]
    hk, tb = ct_ref.shape
    n_groups = hk // SUBLANES
    n_acc = 4
    vreg = o_ref.shape[1:]
    lane = lax.broadcasted_iota(i32, (hk, tb), 1)

    def batch(b, carry):
        t0 = b * TOKEN_BATCH
        ct = ct_ref[...]
        for u in range(TOKEN_BATCH):
            col = jnp.sum(jnp.where(lane == t0 + u, ct, 0.0), axis=1, keepdims=True)
            cx_ref[u] = jnp.broadcast_to(col, (hk, LANES))

        def token(u, c1):
            t = t0 + u

            def fetch(g):
                tg = t * n_groups + g
                return tuple(tab_ref[pl.ds(pl.multiple_of(r[tg], SUBLANES), SUBLANES), :] for r in row_refs)

            def group(g, accs):
                accs, packed = list(accs), fetch(g)
                k0 = pl.multiple_of(g * SUBLANES, SUBLANES)
                n_even = ne_ref[t * n_groups + g]
                for p in range(SUBLANES):
                    coef = jnp.broadcast_to(cx_ref[u, pl.ds(k0 + p, 1), :], vreg)
                    accs[p % n_acc] = accs[p % n_acc] + _unpack(packed[p], sh_ref, n_even, p) * coef
                return tuple(accs)

            accs = lax.fori_loop(0, n_groups, group, (jnp.zeros(vreg, f32),) * n_acc, unroll=GROUP_UNROLL)
            o_ref[t] = (accs[0] + accs[1]) + (accs[2] + accs[3])
            return c1

        lax.fori_loop(0, TOKEN_BATCH, token, 0)
        return carry

    lax.fori_loop(0, tb // TOKEN_BATCH, batch, 0)


def _peer_up(rows8, n_even, coef_t, tab, tb):
    hk, T = coef_t.shape
    per_group, whole = _peer_specs(tb, hk)
    return pl.pallas_call(
        _peer_up_kernel, grid=(T // tb,), out_shape=jax.ShapeDtypeStruct((T, SUBLANES, LANES), f32),
        out_specs=pl.BlockSpec((tb, SUBLANES, LANES), lambda i: (i, 0, 0)),
        in_specs=[per_group] * (SUBLANES + 1) + [pl.BlockSpec((hk, tb), lambda i: (0, i)), whole, whole],
        scratch_shapes=[pltpu.VMEM((TOKEN_BATCH, hk, LANES), f32)],
        compiler_params=_params("arbitrary"), name="peer_up",
    )(*rows8, n_even, coef_t, _shift_patterns(), tab)


def _ffn_out_kernel(h_ref, f_ref, g_ref, b_ref, o_ref, *, alpha):
    o_ref[...] = _layer_norm(alpha * h_ref[...] + f_ref[...], g_ref[...], b_ref[...])


def _ffn_out(h2, ff, g, b, alpha, tm):
    T, D = h2.shape
    g, b = g.reshape(1, D), b.reshape(1, D)
    tok = pl.BlockSpec((tm, D), lambda i: (i, 0))
    full = pl.BlockSpec((1, D), lambda i: (0, 0))
    return pl.pallas_call(
        functools.partial(_ffn_out_kernel, alpha=alpha), grid=(T // tm,),
        out_shape=jax.ShapeDtypeStruct((T, D), f32), out_specs=tok, in_specs=[tok, tok, full, full],
        compiler_params=_params("parallel"), name="ffn_out",
    )(h2, ff, g, b)


def _tile(n, want):
    t = min(n, want)
    assert n % t == 0, (n, t)
    return t


def kernel(x, positions, mem, w_in, gla_gate_up, gla_gate_bias, gla_norm_g, w_out, ln_mix_g, ln_mix_b, xattn_w_q, xattn_w_k, xattn_w_v, xattn_w_o, ln_mem_g, ln_mem_b, peer_w_query, peer_sub_keys_1, peer_sub_keys_2, peer_expert_down, peer_expert_up, ln_ffn_g, ln_ffn_b):
    B, S, D = x.shape
    T = B * S
    depth = w_in.shape[0]
    alpha = (2.0 * depth) ** 0.25
    tm = _tile(S, TOKEN_TILE)
    h = x
    for l in range(depth):
        q, kt, v, qi, kit, misc, gq, gk, gv, la, gr = _proj(h, positions, w_in[l], gla_gate_up[l], gla_gate_bias[l], tm)
        y_dsa = _dsa(q, kt, v, qi, kit, misc, _tile(S, DSA_QUERY_BLOCK), _tile(S, DSA_KEY_CHUNK))
        y_gla = _gla(gq, gk, gv, la, gr, gla_norm_g[l], tm)
        h1 = _mix_out(h.reshape(T, D), y_dsa.reshape(T, W_DSA), y_gla.reshape(T, W_GV), w_out[l],
                      ln_mix_g[l], ln_mix_b[l], alpha, tm)
        km, vm = _mem_kv(mem, xattn_w_k[l], xattn_w_v[l])
        h2 = _xattn(h1.reshape(B, S, D), km, vm, xattn_w_q[l], xattn_w_o[l], ln_mem_g[l], ln_mem_b[l], alpha, tm)
        h2 = h2.reshape(T, D)
        rows_t, ne_t, gates_t = _route(h2, peer_w_query[l], peer_sub_keys_1[l], peer_sub_keys_2[l],
                                       _tile(T, ROUTE_TILE))
        flat = lambda a: a.T.reshape(-1)
        n_even = flat(ne_t)
        by_pos = rows_t.reshape(-1, SUBLANES, T).transpose(1, 2, 0).reshape(SUBLANES, -1)
        rows8 = [by_pos[p] for p in range(SUBLANES)]
        tb = _tile(T, PEER_TOKEN_BLOCK)
        coef_t = _peer_down(rows8, n_even, h2.reshape(T, SUBLANES, LANES), gates_t, _pack_table(peer_expert_down[l]), tb)
        ff = _peer_up(rows8, n_even, coef_t, _pack_table(peer_expert_up[l]), tb)
        h = _ffn_out(h2, ff.reshape(T, D), ln_ffn_g[l], ln_ffn_b[l], alpha, tm).reshape(B, S, D)
    return h
```

```python
import functools

import jax
import jax.numpy as jnp
import numpy as np
from jax import lax
from jax.experimental import pallas as pl
from jax.experimental.pallas import tpu as pltpu

f32 = jnp.float32
bf16 = jnp.bfloat16
i32 = jnp.int32

DSA_HEADS = 8
DSA_HEAD_DIM = 64
IDX_HEADS = 8
IDX_DIM = 32
TOPK_MAX = 256
GLA_HEADS = 4
GLA_DK = 64
GLA_DV = 128
GLA_GATE_RANK = 16
GLA_GATE_TEMP = 16.0
GLA_CHUNK = 64
ROPE_THETA = 500000.0
ROPE_FRACTION = 4
XATTN_HEADS = 4
PEER_N_KEYS = 128
PEER_HEADS = 8
PEER_D_KEY = 256
PEER_TOPK = 16
LN_EPS = 1e-5
RMS_EPS = 1e-6

LANES = 128
SUBLANES = 8
VMEM_LIMIT = 56 * 1024 * 1024

TOKEN_TILE = 512
DSA_QUERY_BLOCK = LANES
DSA_KEY_CHUNK = 512
ROUTE_TILE = 256
PEER_TOKEN_BLOCK = LANES

INT_MIN = -(2 ** 31)
NEG_INF = float("-inf")

W_DSA = DSA_HEADS * DSA_HEAD_DIM
W_IDX = IDX_HEADS * IDX_DIM
W_GQK = GLA_HEADS * GLA_DK
W_GV = GLA_HEADS * GLA_DV
MISC_WI = IDX_DIM
MISC_LR = IDX_DIM + IDX_HEADS


def _dot(a, b, dims=(((1,), (0,)), ((), ())), precision=None):
    return lax.dot_general(a, b, dims, precision=precision, preferred_element_type=f32)


_NT = (((1,), (1,)), ((), ()))
_TN = (((0,), (0,)), ((), ()))


def _params(*sem):
    return pltpu.CompilerParams(dimension_semantics=sem, vmem_limit_bytes=VMEM_LIMIT)


def _layer_norm(y, g, b):
    mu = jnp.mean(y, axis=-1, keepdims=True)
    yc = y - mu
    var = jnp.mean(yc * yc, axis=-1, keepdims=True)
    return yc * lax.rsqrt(var + LN_EPS) * g + b


def _rot(xb, c, sa, sb, half):
    return xb * c + pltpu.roll(xb, LANES - half, 1) * sa + pltpu.roll(xb, half, 1) * sb


def _proj_kernel(x_ref, w_ref, gup_ref, gb_ref, trig_ref, spread_ref,
                 q_ref, kt_ref, v_ref, qi_ref, kit_ref, misc_ref, gq_ref, gk_ref, gv_ref, la_ref, gr_ref):
    x = x_ref[...].astype(bf16)
    tm = x.shape[0]
    tabs = _dot(trig_ref[...], spread_ref[...], precision=lax.Precision.HIGHEST)
    cq, saq, sbq, ci, sai, sbi = (tabs[:, j * LANES:(j + 1) * LANES] for j in range(6))
    hq = DSA_HEAD_DIM // ROPE_FRACTION // 2
    hi = IDX_DIM // ROPE_FRACTION // 2
    o = 0
    scale = DSA_HEAD_DIM ** -0.5 * np.log2(np.e)
    for j in range(W_DSA // LANES):
        a = _dot(x, w_ref[:, o + j * LANES:o + (j + 1) * LANES])
        q_ref[:, j * LANES:(j + 1) * LANES] = (_rot(a, cq, saq, sbq, hq) * scale).astype(bf16)
    o += W_DSA
    for j in range(W_DSA // LANES):
        a = _dot(x, w_ref[:, o + j * LANES:o + (j + 1) * LANES])
        kt_ref[j * LANES:(j + 1) * LANES, :] = _rot(a, cq, saq, sbq, hq).T.astype(bf16)
    o += W_DSA
    v_ref[...] = _dot(x, w_ref[:, o:o + W_DSA]).astype(bf16)
    o += W_DSA
    for j in range(W_IDX // LANES):
        a = _dot(x, w_ref[:, o + j * LANES:o + (j + 1) * LANES])
        qi_ref[:, j * LANES:(j + 1) * LANES] = _rot(a, ci, sai, sbi, hi).astype(bf16)
    o += W_IDX
    m = _dot(x, w_ref[:, o:o + LANES])
    lane = lax.broadcasted_iota(i32, (tm, LANES), 1)
    is_ki = lane < IDX_DIM
    m = _rot(m, jnp.where(is_ki, ci, 1.0), jnp.where(is_ki, sai, 0.0), jnp.where(is_ki, sbi, 0.0), hi)
    misc_ref[...] = m
    kit_ref[...] = m.T[:IDX_DIM, :].astype(bf16)
    z = _dot(m.astype(bf16), gup_ref[...]) + gb_ref[...]
    la_ref[...] = (jnp.minimum(z, 0.0) - jnp.log1p(jnp.exp(-jnp.abs(z)))) / GLA_GATE_TEMP
    o += LANES
    gq_ref[...] = _dot(x, w_ref[:, o:o + W_GQK])
    o += W_GQK
    gk_ref[...] = _dot(x, w_ref[:, o:o + W_GQK])
    o += W_GQK
    gv_ref[...] = _dot(x, w_ref[:, o:o + W_GV]).astype(bf16)
    o += W_GV
    gr_ref[...] = _dot(x, w_ref[:, o:o + W_GV])


def _rot_trig(positions):
    cols = []
    for head_dim in (DSA_HEAD_DIM, IDX_DIM):
        half = head_dim // ROPE_FRACTION // 2
        inv_freq = ROPE_THETA ** (-jnp.arange(half, dtype=f32) / half)
        ang = positions.astype(f32)[..., None] * inv_freq
        cols += [jnp.cos(ang), jnp.sin(ang)]
    return jnp.concatenate(cols + [jnp.ones(positions.shape + (1,), f32)], axis=-1)


def _rot_spread():
    e = np.zeros((25, 6 * LANES), np.float32)
    row0 = 0
    for t, head_dim in enumerate((DSA_HEAD_DIM, IDX_DIM)):
        r = head_dim // ROPE_FRACTION
        half = r // 2
        for lane in range(LANES):
            j = lane % head_dim
            c, sa, sb = (3 * t) * LANES + lane, (3 * t + 1) * LANES + lane, (3 * t + 2) * LANES + lane
            if j < half:
                e[row0 + j, c], e[row0 + half + j, sa] = 1.0, -1.0
            elif j < r:
                e[row0 + j - half, c], e[row0 + j, sb] = 1.0, 1.0
            else:
                e[24, c] = 1.0
        row0 += r
    return jnp.asarray(e)


def _proj(x, positions, w_in, gate_up, gate_bias, tm):
    B, S, D = x.shape
    splits = np.cumsum([W_DSA, W_DSA, W_DSA, W_IDX, IDX_DIM, IDX_HEADS, W_GQK, W_GQK, W_GV, GLA_GATE_RANK])
    (wq, wk, wv, wqi, wki, wwi, wgq, wgk, wgv, wlr, wgr) = jnp.split(w_in, splits.tolist(), axis=1)
    pad = jnp.zeros((D, LANES - IDX_DIM - IDX_HEADS - GLA_GATE_RANK), w_in.dtype)
    w_a = jnp.concatenate([wq, wk, wv, wqi, wki, wwi, wlr, pad, wgq, wgk, wgv, wgr], axis=1).astype(bf16)
    gup = jnp.zeros((LANES, W_GQK), f32).at[MISC_LR:MISC_LR + GLA_GATE_RANK].set(gate_up).astype(bf16)
    trig, spread = _rot_trig(positions), _rot_spread()
    tok = lambda n: pl.BlockSpec((None, tm, n), lambda b, j: (b, j, 0))
    full = lambda a: pl.BlockSpec(a.shape, lambda b, j: (0,) * a.ndim)
    tr = lambda n: pl.BlockSpec((None, n, tm), lambda b, j: (b, 0, j))
    sd = jax.ShapeDtypeStruct
    out_shape = [sd((B, S, W_DSA), bf16), sd((B, W_DSA, S), bf16), sd((B, S, W_DSA), bf16), sd((B, S, W_IDX), bf16),
                 sd((B, IDX_DIM, S), bf16), sd((B, S, LANES), f32), sd((B, S, W_GQK), f32), sd((B, S, W_GQK), f32),
                 sd((B, S, W_GV), bf16), sd((B, S, W_GQK), f32), sd((B, S, W_GV), f32)]
    out_specs = [tok(W_DSA), tr(W_DSA), tok(W_DSA), tok(W_IDX), tr(IDX_DIM), tok(LANES), tok(W_GQK), tok(W_GQK),
                 tok(W_GV), tok(W_GQK), tok(W_GV)]
    gb = gate_bias.reshape(1, W_GQK)
    return pl.pallas_call(
        _proj_kernel, grid=(B, S // tm), out_shape=out_shape, out_specs=out_specs,
        in_specs=[tok(D), full(w_a), full(gup), full(gb), tok(trig.shape[-1]), full(spread)],
        compiler_params=_params("parallel", "parallel"), name="proj",
    )(x, w_a, gup, gb, trig, spread)


def _dsa_kernel(q_ref, kt_ref, v_ref, qi_ref, kit_ref, misc_ref, o_ref, key_ref, keyt_ref, bias_ref, lg_ref, mx_ref,
                ls_ref, acc_ref, *, n_sel, idx_bits, kc):
    qb, S = q_ref.shape[0], kt_ref.shape[1]
    i = pl.program_id(1)
    nk = lax.div((i + 1) * qb + (kc - 1), kc)
    assert qb == LANES
    qi = qi_ref[...]
    wi = misc_ref[:, MISC_WI:MISC_WI + IDX_HEADS] * (IDX_HEADS ** -0.5) * (IDX_DIM ** -0.5)
    qpos = lax.broadcasted_iota(i32, (qb, 1), 0) + i * qb
    lane = lax.broadcasted_iota(i32, (qb, kc), 1)

    def chunk(c):
        return pl.ds(pl.multiple_of(c * kc, kc), kc)

    qi_heads = [qi[:, h * IDX_DIM:(h + 1) * IDX_DIM] for h in range(IDX_HEADS)]

    def score_chunk(c, carry):
        kit = kit_ref[:, chunk(c)]
        score = jnp.zeros((qb, kc), f32)
        for h in range(IDX_HEADS):
            d = _dot(qi_heads[h], kit)
            score = score + wi[:, h:h + 1] * jnp.maximum(d, 0.0)
        score = jnp.where(score == 0.0, 0.0, score)
        bits = pltpu.bitcast(score, i32)
        key = bits ^ ((bits >> 31) & jnp.int32(0x7FFFFFFF))
        key = jnp.where(lane + c * kc <= qpos, key, INT_MIN)
        key_ref[:, chunk(c)] = key
        for j in range(kc // LANES):
            keyt_ref[pl.ds(pl.multiple_of(c * kc + j * LANES, LANES), LANES), :] = key[:, j * LANES:(j + 1) * LANES].T
        return carry

    lax.fori_loop(0, nk, score_chunk, 0)

    qpos_row = lax.broadcasted_iota(i32, (1, qb), 1) + i * qb
    target = jnp.minimum(n_sel, qpos_row + 1).astype(f32)
    key_row = lax.broadcasted_iota(i32, (kc, qb), 0)
    n_part = 4

    def count(pred):
        def body(c, accs):
            hit = jnp.where(pred(keyt_ref[chunk(c), :], key_row + c * kc), 1.0, 0.0)
            accs = list(accs)
            for j in range(kc // SUBLANES):
                accs[j % n_part] = accs[j % n_part] + hit[j * SUBLANES:(j + 1) * SUBLANES]
            return tuple(accs)

        accs = lax.fori_loop(0, nk, body, (jnp.zeros((SUBLANES, qb), f32),) * n_part)
        return jnp.sum((accs[0] + accs[1]) + (accs[2] + accs[3]), axis=0, keepdims=True)

    base = jnp.where(count(lambda k, r: k >= 0) >= target, jnp.int32(0), jnp.int32(INT_MIN))

    def tau_bit(t, base):
        cand = base | jnp.left_shift(jnp.int32(1), 30 - t)
        return jnp.where(count(lambda k, r: k >= cand) >= target, cand, base)

    tau_row = lax.fori_loop(0, 31, tau_bit, base)
    excess = jnp.max(count(lambda k, r: k >= tau_row) - target)

    def tie_limit():
        need = target - count(lambda k, r: k > tau_row)

        def idx_bit(t, m):
            cand = m | jnp.left_shift(jnp.int32(1), idx_bits - 1 - t)
            return jnp.where(count(lambda k, r: (k == tau_row) & (r < cand)) < need, cand, m)

        return lax.fori_loop(0, idx_bits, idx_bit, jnp.zeros((1, qb), i32))

    m_row = lax.cond(excess > 0.0, tie_limit, lambda: jnp.full((1, qb), S, i32))
    to_col = lambda row: jnp.broadcast_to(row, (qb, qb)).T[:, 0:1]
    tau, m_idx = to_col(tau_row), to_col(m_row)

    def bias_chunk(c, carry):
        k = key_ref[:, chunk(c)]
        sel = (k > tau) | ((k == tau) & (lane + c * kc <= m_idx))
        bias_ref[:, chunk(c)] = jnp.where(sel, 0.0, NEG_INF)
        return carry

    lax.fori_loop(0, nk, bias_chunk, 0)

    q = q_ref[...]
    pair = LANES // DSA_HEAD_DIM

    def fold(x, op, acc):
        for j in range(kc // LANES):
            acc = op(acc, x[:, j * LANES:(j + 1) * LANES])
        return acc

    heads = range(DSA_HEADS)
    head_rows = [slice(h * DSA_HEAD_DIM, (h + 1) * DSA_HEAD_DIM) for h in heads]
    mx_ref[...] = jnp.full(mx_ref.shape, NEG_INF, f32)
    ls_ref[...] = jnp.zeros_like(ls_ref)
    acc_ref[...] = jnp.zeros_like(acc_ref)

    q_heads = [q[:, head_rows[h]] for h in heads]

    def logits(c, carry):
        bias = bias_ref[:, chunk(c)]
        for h in heads:
            lg = _dot(q_heads[h], kt_ref[head_rows[h], chunk(c)]) + bias
            lg_ref[h, :, chunk(c)] = lg
            mx_ref[h] = fold(lg, jnp.maximum, mx_ref[h])
        return carry

    lax.fori_loop(0, nk, logits, 0)
    ms = [jnp.max(mx_ref[h], axis=1, keepdims=True) for h in heads]

    def weigh(c, carry):
        for h in heads:
            p = jnp.exp2(lg_ref[h, :, chunk(c)] - ms[h])
            ls_ref[h] = fold(p, jnp.add, ls_ref[h])
            slab = slice((h // pair) * LANES, (h // pair + 1) * LANES)
            acc_ref[h] = acc_ref[h] + _dot(p.astype(bf16), v_ref[chunk(c), slab])
        return carry

    lax.fori_loop(0, nk, weigh, 0)
    for h in heads:
        l = jnp.sum(ls_ref[h], axis=1, keepdims=True)
        off = (h % pair) * DSA_HEAD_DIM
        o_ref[:, head_rows[h]] = (acc_ref[h][:, off:off + DSA_HEAD_DIM] / l).astype(o_ref.dtype)


def _dsa(q, kt, v, qi, kit, misc, qb, kc):
    B, S, _ = q.shape
    n_sel = min(TOPK_MAX, S // 4)
    blk = lambda n: pl.BlockSpec((None, qb, n), lambda b, i: (b, i, 0))
    per_b = lambda r, c: pl.BlockSpec((None, r, c), lambda b, i: (b, 0, 0))
    kern = functools.partial(_dsa_kernel, n_sel=n_sel, idx_bits=max(1, (S - 1).bit_length()), kc=kc)
    return pl.pallas_call(
        kern, grid=(B, S // qb), out_shape=jax.ShapeDtypeStruct((B, S, W_DSA), bf16), out_specs=blk(W_DSA),
        in_specs=[blk(W_DSA), per_b(W_DSA, S), per_b(S, W_DSA), blk(W_IDX), per_b(IDX_DIM, S), blk(LANES)],
        scratch_shapes=[pltpu.VMEM((qb, S), i32), pltpu.VMEM((S, qb), i32), pltpu.VMEM((qb, S), f32),
                        pltpu.VMEM((DSA_HEADS, qb, S), f32)]
        + [pltpu.VMEM((DSA_HEADS, qb, LANES), f32)] * 3,
        compiler_params=_params("parallel", "arbitrary"), name="dsa",
    )(q, kt, v, qi, kit, misc)


def _gla_kernel(gq_ref, gk_ref, gv_ref, la_ref, gr_ref, ng_ref, o_ref, state_ref):
    ct = gq_ref.shape[0]
    nch = ct // GLA_CHUNK

    @pl.when(pl.program_id(1) == 0)
    def _():
        state_ref[...] = jnp.zeros_like(state_ref)

    la = la_ref[...]
    r = lax.broadcasted_iota(i32, (ct, ct), 0)
    c = lax.broadcasted_iota(i32, (ct, ct), 1)
    same = (r // GLA_CHUNK) == (c // GLA_CHUNK)
    causal = same & (c <= r)
    tri = jnp.where(causal, 1.0, 0.0).astype(bf16)
    la_hi = la.astype(bf16)
    rest = la - la_hi.astype(f32)
    la_mid = rest.astype(bf16)
    la_lo = (rest - la_mid.astype(f32)).astype(bf16)
    bcum = (_dot(tri, la_hi) + _dot(tri, la_mid)) + _dot(tri, la_lo)
    blast = jnp.concatenate([jnp.broadcast_to(bcum[(n + 1) * GLA_CHUNK - 1:(n + 1) * GLA_CHUNK], (GLA_CHUNK, la.shape[1]))
                             for n in range(nch)], axis=0)
    q_dec = (gq_ref[...] * (GLA_DK ** -0.5) * jnp.exp(bcum)).astype(bf16)
    k_inv = (gk_ref[...] * jnp.exp(-bcum)).astype(bf16)
    k_end = (gk_ref[...] * jnp.exp(blast - bcum)).astype(bf16)
    decay = jnp.exp(blast)
    ng = ng_ref[...]
    for h in range(GLA_HEADS):
        ks = slice(h * GLA_DK, (h + 1) * GLA_DK)
        vs = slice(h * GLA_DV, (h + 1) * GLA_DV)
        qd, ki, ke, vh = q_dec[:, ks], k_inv[:, ks], k_end[:, ks], gv_ref[:, vs]
        attn = jnp.where(causal, _dot(qd, ki, _NT), 0.0)
        o = _dot(attn.astype(bf16), vh)
        st = state_ref[h]
        inter = []
        for n in range(nch):
            rows = slice(n * GLA_CHUNK, (n + 1) * GLA_CHUNK)
            inter.append(_dot(qd[rows], st.astype(bf16), _NT))
            st = st * decay[n * GLA_CHUNK:n * GLA_CHUNK + 1, ks] + _dot(vh[rows], ke[rows], _TN)
        state_ref[h] = st
        o = o + jnp.concatenate(inter, axis=0)
        o = o * lax.rsqrt(jnp.mean(o * o, axis=-1, keepdims=True) + RMS_EPS) * ng
        g = gr_ref[:, vs]
        o_ref[:, vs] = (o * (g * jax.nn.sigmoid(g))).astype(o_ref.dtype)


def _gla(gq, gk, gv, la, gr, norm_g, ct):
    B, S, _ = gq.shape
    blk = lambda n: pl.BlockSpec((None, ct, n), lambda b, j: (b, j, 0))
    ng = norm_g.reshape(1, GLA_DV)
    return pl.pallas_call(
        _gla_kernel, grid=(B, S // ct), out_shape=jax.ShapeDtypeStruct((B, S, W_GV), bf16), out_specs=blk(W_GV),
        in_specs=[blk(W_GQK), blk(W_GQK), blk(W_GV), blk(W_GQK), blk(W_GV), pl.BlockSpec(ng.shape, lambda b, j: (0, 0))],
        scratch_shapes=[pltpu.VMEM((GLA_HEADS, GLA_DV, GLA_DK), f32)],
        compiler_params=_params("parallel", "arbitrary"), name="gla",
    )(gq, gk, gv, la, gr, ng)


def _mix_out_kernel(x_ref, ya_ref, yb_ref, wa_ref, wb_ref, g_ref, b_ref, o_ref, *, alpha):
    mix = _dot(ya_ref[...], wa_ref[...]) + _dot(yb_ref[...], wb_ref[...])
    o_ref[...] = _layer_norm(alpha * x_ref[...] + mix, g_ref[...], b_ref[...])


def _mix_out(x2, ya, yb, w_out, g, b, alpha, tm):
    T, D = x2.shape
    wa, wb = w_out[:W_DSA].astype(bf16), w_out[W_DSA:].astype(bf16)
    tok = lambda n: pl.BlockSpec((tm, n), lambda i: (i, 0))
    full = lambda a: pl.BlockSpec(a.shape, lambda i: (0, 0))
    g, b = g.reshape(1, D), b.reshape(1, D)
    return pl.pallas_call(
        functools.partial(_mix_out_kernel, alpha=alpha), grid=(T // tm,),
        out_shape=jax.ShapeDtypeStruct((T, D), f32), out_specs=tok(D),
        in_specs=[tok(D), tok(W_DSA), tok(W_GV), full(wa), full(wb), full(g), full(b)],
        compiler_params=_params("parallel"), name="mix_out",
    )(x2, ya, yb, wa, wb, g, b)


def _mem_kv_kernel(m_ref, wk_ref, wv_ref, k_ref, v_ref):
    m = m_ref[...].astype(bf16)
    k_ref[...] = _dot(m, wk_ref[...]).astype(bf16)
    v_ref[...] = _dot(m, wv_ref[...]).astype(bf16)


def _mem_kv(mem, w_k, w_v):
    B, M, D = mem.shape
    wk, wv = w_k.astype(bf16), w_v.astype(bf16)
    blk = pl.BlockSpec((None, M, D), lambda b: (b, 0, 0))
    full = pl.BlockSpec((D, D), lambda b: (0, 0))
    sd = jax.ShapeDtypeStruct((B, M, D), bf16)
    return pl.pallas_call(_mem_kv_kernel, grid=(B,), out_shape=[sd, sd], out_specs=[blk, blk],
                          in_specs=[blk, full, full], compiler_params=_params("parallel"), name="mem_kv")(mem, wk, wv)


def _xattn_kernel(h_ref, k_ref, v_ref, wq_ref, wo_ref, g_ref, b_ref, o_ref, *, alpha):
    h = h_ref[...]
    D = h.shape[1]
    hd = D // XATTN_HEADS
    q = (_dot(h.astype(bf16), wq_ref[...]) * (hd ** -0.5)).astype(bf16)
    outs = []
    for a in range(XATTN_HEADS):
        s = slice(a * hd, (a + 1) * hd)
        lg = _dot(q[:, s], k_ref[:, s], _NT)
        p = jnp.exp(lg - jnp.max(lg, axis=1, keepdims=True))
        l = jnp.sum(p, axis=1, keepdims=True)
        outs.append((_dot(p.astype(bf16), v_ref[:, s]) / l).astype(bf16))
    ca = _dot(jnp.concatenate(outs, axis=1), wo_ref[...])
    o_ref[...] = _layer_norm(alpha * h + ca, g_ref[...], b_ref[...])


def _xattn(h1, km, vm, w_q, w_o, g, b, alpha, tm):
    B, S, D = h1.shape
    M = km.shape[1]
    wq, wo = w_q.astype(bf16), w_o.astype(bf16)
    g, b = g.reshape(1, D), b.reshape(1, D)
    tok = pl.BlockSpec((None, tm, D), lambda bi, j: (bi, j, 0))
    per_b = pl.BlockSpec((None, M, D), lambda bi, j: (bi, 0, 0))
    full = lambda a: pl.BlockSpec(a.shape, lambda bi, j: (0, 0))
    return pl.pallas_call(
        functools.partial(_xattn_kernel, alpha=alpha), grid=(B, S // tm),
        out_shape=jax.ShapeDtypeStruct((B, S, D), f32), out_specs=tok,
        in_specs=[tok, per_b, per_b, full(wq), full(wo), full(g), full(b)],
        compiler_params=_params("parallel", "parallel"), name="xattn",
    )(h1, km, vm, wq, wo, g, b)


def _top_rows(s, n_top, ids=None, payload=None):
    if ids is None:
        ids = lax.broadcasted_iota(i32, s.shape, 0).astype(f32)
    vals, picks = [], []
    for _ in range(n_top):
        m = jnp.max(s, axis=0, keepdims=True)
        am = jnp.min(jnp.where(s == m, ids, 2.0 ** 30), axis=0, keepdims=True)
        hit = ids == am
        vals.append(m)
        picks.append(am if payload is None else jnp.max(jnp.where(hit, payload, -1.0), axis=0, keepdims=True))
        s = jnp.where(hit, NEG_INF, s)
    return jnp.concatenate(vals, axis=0), jnp.concatenate(picks, axis=0)


def _pair_candidates(v1, i1, v2, i2):
    n = PEER_TOPK
    sub = lax.broadcasted_iota(i32, (SUBLANES,) + v1.shape[1:], 0)
    as_f = lambda x: x.astype(f32)
    blocks = [(v1[0:1] + v2, as_f(lax.broadcasted_iota(i32, v2.shape, 0)), i1[0:1] * PEER_N_KEYS + i2)]
    for a in range(1, 5):
        blocks.append((v1[a:a + 1] + v2[:SUBLANES], as_f(a * n + sub), i1[a:a + 1] * PEER_N_KEYS + i2[:SUBLANES]))
    pick = lambda x: jnp.where(sub < 2, x[5:6], jnp.where(sub < 4, x[6:7], x[7:8]))
    alt = lambda x: jnp.where((sub & 1) == 0, x[0:1], x[1:2])
    a_of = jnp.where(sub < 2, 5, jnp.where(sub < 4, 6, 7))
    blocks.append((jnp.where(sub < 6, pick(v1) + alt(v2), NEG_INF), as_f(a_of * n + (sub & 1)),
                   pick(i1) * PEER_N_KEYS + alt(i2)))
    blocks.append((v1[SUBLANES:] + v2[0:1], as_f((sub + SUBLANES) * n), i1[SUBLANES:] * PEER_N_KEYS + i2[0:1]))
    return tuple(jnp.concatenate(parts, axis=0) for parts in zip(*blocks))


def _route_kernel(h_ref, wq_ref, k1_ref, k2_ref, row_ref, ne_ref, g_ref):
    q = _dot(h_ref[...].astype(bf16), wq_ref[...])
    dk = PEER_D_KEY // 2
    k1, k2 = k1_ref[...], k2_ref[...]
    for a in range(PEER_HEADS):
        qa = q[:, a * PEER_D_KEY:a * PEER_D_KEY + dk].astype(bf16)
        qb = q[:, a * PEER_D_KEY + dk:(a + 1) * PEER_D_KEY].astype(bf16)
        v1, i1 = _top_rows(_dot(k1, qa, _NT), PEER_TOPK)
        v2, i2 = _top_rows(_dot(k2, qb, _NT), PEER_TOPK)
        cand, ids, cidx = _pair_candidates(v1, i1, v2, i2)
        top, experts = _top_rows(cand, PEER_TOPK, ids=ids, payload=cidx)
        experts = experts.astype(i32)
        p = jnp.exp(top - top[0:1])
        gates = p / jnp.sum(p, axis=0, keepdims=True)
        for half in range(PEER_TOPK // SUBLANES):
            grp = slice(half * SUBLANES, (half + 1) * SUBLANES)
            e, g = experts[grp], gates[grp]
            odd = e & 1
            n_even = SUBLANES - jnp.sum(odd, axis=0, keepdims=True)
            sub = lax.broadcasted_iota(i32, e.shape, 0)
            evens_before = jnp.zeros_like(n_even)
            e_sorted, g_sorted = jnp.zeros_like(e), jnp.zeros_like(g)
            for r in range(SUBLANES):
                odd_r = odd[r:r + 1]
                dest = jnp.where(odd_r == 1, n_even + (r - evens_before), evens_before)
                hit = sub == dest
                e_sorted = jnp.where(hit, e[r:r + 1], e_sorted)
                g_sorted = jnp.where(hit, g[r:r + 1], g_sorted)
                evens_before = evens_before + (1 - odd_r)
            rows = slice(a * PEER_TOPK + half * SUBLANES, a * PEER_TOPK + (half + 1) * SUBLANES)
            row_ref[rows, :] = (e_sorted >> 1) * SUBLANES
            g_ref[rows, :] = g_sorted
            n = a * (PEER_TOPK // SUBLANES) + half
            ne_ref[n:n + 1, :] = n_even


def _route(h2, w_query, k1, k2, tm):
    T, D = h2.shape
    wq = w_query.astype(bf16)
    k1, k2 = k1.astype(bf16), k2.astype(bf16)
    hk = PEER_HEADS * PEER_TOPK
    full = lambda a: pl.BlockSpec(a.shape, lambda i: (0, 0))
    out = pl.BlockSpec((hk, tm), lambda i: (0, i))
    n_groups = hk // SUBLANES
    sd = jax.ShapeDtypeStruct
    return pl.pallas_call(
        _route_kernel, grid=(T // tm,),
        out_shape=[sd((hk, T), i32), sd((n_groups, T), i32), sd((hk, T), f32)],
        out_specs=[out, pl.BlockSpec((n_groups, tm), lambda i: (0, i)), out],
        in_specs=[pl.BlockSpec((tm, D), lambda i: (i, 0)), full(wq), full(k1), full(k2)],
        compiler_params=_params("parallel"), name="route",
    )(h2, wq, k1, k2)


def _pack_table(tab):
    n, d = tab.shape
    assert d == SUBLANES * LANES
    u = lax.bitcast_convert_type(tab.astype(bf16), jnp.uint16).astype(jnp.uint32).reshape(n // 2, 2, SUBLANES, LANES)
    return lax.bitcast_convert_type((u[:, 0] << 16) | u[:, 1], i32).reshape(n // 2 * SUBLANES, LANES)


def _shift_patterns():
    n_even = np.arange(SUBLANES + 1)[:, None, None]
    p = np.arange(SUBLANES)[None, :, None]
    pat = np.where(p < n_even, 0, 16) + np.zeros((1, 1, LANES), np.int64)
    return jnp.asarray(pat.reshape(-1, LANES), i32)


def _unpack(w, sh_ref, n_even, p):
    shift = jnp.broadcast_to(sh_ref[pl.ds(n_even * SUBLANES + p, 1), :], w.shape)
    return pltpu.bitcast(jnp.left_shift(w, shift) & jnp.int32(-65536), f32)


TOKEN_BATCH = 64
GROUP_UNROLL = 16


_TREE_ORDER = (0, 4, 2, 6, 1, 5, 3, 7)


def _packed(w):
    return pltpu.bitcast(w, bf16)


def _merge(p, q, mask, shift):
    moved = pltpu.roll(jnp.where(mask, q, p), shift, 0)
    return pltpu.bitcast(_packed(jnp.where(mask, p, q)) + _packed(moved), i32)


def _sublane_sums(ps):
    sub = lax.broadcasted_iota(i32, ps[0].shape, 0)
    quad = lambda a, b, o: _merge(a, b, ((sub - o) & 7) < 4, 4)
    duo = lambda a, b, o: _merge(a, b, ((sub - o) & 3) < 2, 6)
    r1 = duo(quad(ps[0], ps[1], 0), quad(ps[2], ps[3], 2), 0)
    r2 = duo(quad(ps[4], ps[5], 1), quad(ps[6], ps[7], 3), 1)
    return _merge(r1, r2, (sub & 1) == 0, 7)


def _peer_down_kernel(*refs):
    row_refs, (ne_ref, x_ref, gt_ref, sh_ref, tab_ref, ct_ref, part_ref, actt_ref) = refs[:SUBLANES], refs[SUBLANES:]
    hk, tb = ct_ref.shape
    n_groups = hk // SUBLANES
    lane = lax.broadcasted_iota(i32, (hk, tb), 1)
    actt_ref[...] = jnp.zeros_like(actt_ref)

    def batch(b, carry):
        t0 = b * TOKEN_BATCH

        def token(u, c1):
            t = t0 + u
            xh = pltpu.bitcast(x_ref[t].astype(bf16).astype(f32), i32)
            xw = _packed(xh | lax.shift_right_logical(xh, 16))

            def group(g, c2):
                k0 = pl.multiple_of(g * SUBLANES, SUBLANES)
                tg = t * n_groups + g
                ps = []
                for k in _TREE_ORDER:
                    w = tab_ref[pl.ds(pl.multiple_of(row_refs[k][tg], SUBLANES), SUBLANES), :]
                    ps.append(pltpu.bitcast(_packed(w) * xw, i32))
                sums = _sublane_sums(ps)
                shift = sh_ref[pl.ds(pl.multiple_of(ne_ref[tg] * SUBLANES, SUBLANES), SUBLANES), :]
                part_ref[u, pl.ds(k0, SUBLANES), :] = pltpu.bitcast(jnp.left_shift(sums, shift) & jnp.int32(-65536), f32)
                return c2

            lax.fori_loop(0, n_groups, group, 0, unroll=GROUP_UNROLL)
            return c1

        lax.fori_loop(0, TOKEN_BATCH, token, 0)
        a = actt_ref[...]
        for u in range(TOKEN_BATCH):
            a = jnp.where(lane == t0 + u, jnp.sum(part_ref[u], axis=1, keepdims=True), a)
        actt_ref[...] = a
        return carry

    lax.fori_loop(0, tb // TOKEN_BATCH, batch, 0)
    a = actt_ref[...]
    gelu = 0.5 * a * (1.0 + lax.erf(a * (2.0 ** -0.5)))
    ct_ref[...] = gt_ref[...] * gelu


def _peer_specs(tb, hk):
    per_group = pl.BlockSpec((tb * hk // SUBLANES,), lambda i: (i,), memory_space=pltpu.SMEM)
    whole = pl.BlockSpec(memory_space=pltpu.VMEM)
    return per_group, whole


def _peer_down(rows8, n_even, x3, gates_t, tab, tb):
    hk, T = gates_t.shape
    per_group, whole = _peer_specs(tb, hk)
    per_k = pl.BlockSpec((hk, tb), lambda i: (0, i))
    return pl.pallas_call(
        _peer_down_kernel, grid=(T // tb,), out_shape=jax.ShapeDtypeStruct((hk, T), f32), out_specs=per_k,
        in_specs=[per_group] * (SUBLANES + 1) + [pl.BlockSpec((tb,) + x3.shape[1:], lambda i: (i, 0, 0)), per_k,
                                                 whole, whole],
        scratch_shapes=[pltpu.VMEM((TOKEN_BATCH, hk, LANES), f32), pltpu.VMEM((hk, tb), f32)],
        compiler_params=_params("arbitrary"), name="peer_down",
    )(*rows8, n_even, x3, gates_t, _shift_patterns(), tab)


def _peer_up_kernel(*refs):
    row_refs, (ne_ref, ct_ref, sh_ref, tab_ref, o_ref, cx_ref) = refs[:SUBLANES], refs[SUBLANES:]
    hk, tb = ct_ref.shape
    n_groups = hk // SUBLANES
    n_acc = 4
    vreg = o_ref.shape[1:]
    lane = lax.broadcasted_iota(i32, (hk, tb), 1)

    def batch(b, carry):
        t0 = b * TOKEN_BATCH
        ct = ct_ref[...]
        for u in range(TOKEN_BATCH):
            col = jnp.sum(jnp.where(lane == t0 + u, ct, 0.0), axis=1, keepdims=True)
            cx_ref[u] = jnp.broadcast_to(col, (hk, LANES))

        def token(u, c1):
            t = t0 + u

            def fetch(g):
                tg = t * n_groups + g
                return tuple(tab_ref[pl.ds(pl.multiple_of(r[tg], SUBLANES), SUBLANES), :] for r in row_refs)

            def group(g, accs):
                accs, packed = list(accs), fetch(g)
                k0 = pl.multiple_of(g * SUBLANES, SUBLANES)
                n_even = ne_ref[t * n_groups + g]
                for p in range(SUBLANES):
                    coef = jnp.broadcast_to(cx_ref[u, pl.ds(k0 + p, 1), :], vreg)
                    accs[p % n_acc] = accs[p % n_acc] + _unpack(packed[p], sh_ref, n_even, p) * coef
                return tuple(accs)

            accs = lax.fori_loop(0, n_groups, group, (jnp.zeros(vreg, f32),) * n_acc, unroll=GROUP_UNROLL)
            o_ref[t] = (accs[0] + accs[1]) + (accs[2] + accs[3])
            return c1

        lax.fori_loop(0, TOKEN_BATCH, token, 0)
        return carry

    lax.fori_loop(0, tb // TOKEN_BATCH, batch, 0)


def _peer_up(rows8, n_even, coef_t, tab, tb):
    hk, T = coef_t.shape
    per_group, whole = _peer_specs(tb, hk)
    return pl.pallas_call(
        _peer_up_kernel, grid=(T // tb,), out_shape=jax.ShapeDtypeStruct((T, SUBLANES, LANES), f32),
        out_specs=pl.BlockSpec((tb, SUBLANES, LANES), lambda i: (i, 0, 0)),
        in_specs=[per_group] * (SUBLANES + 1) + [pl.BlockSpec((hk, tb), lambda i: (0, i)), whole, whole],
        scratch_shapes=[pltpu.VMEM((TOKEN_BATCH, hk, LANES), f32)],
        compiler_params=_params("arbitrary"), name="peer_up",
    )(*rows8, n_even, coef_t, _shift_patterns(), tab)


def _ffn_out_kernel(h_ref, f_ref, g_ref, b_ref, o_ref, *, alpha):
    o_ref[...] = _layer_norm(alpha * h_ref[...] + f_ref[...], g_ref[...], b_ref[...])


def _ffn_out(h2, ff, g, b, alpha, tm):
    T, D = h2.shape
    g, b = g.reshape(1, D), b.reshape(1, D)
    tok = pl.BlockSpec((tm, D), lambda i: (i, 0))
    full = pl.BlockSpec((1, D), lambda i: (0, 0))
    return pl.pallas_call(
        functools.partial(_ffn_out_kernel, alpha=alpha), grid=(T // tm,),
        out_shape=jax.ShapeDtypeStruct((T, D), f32), out_specs=tok, in_specs=[tok, tok, full, full],
        compiler_params=_params("parallel"), name="ffn_out",
    )(h2, ff, g, b)


def _tile(n, want):
    t = min(n, want)
    assert n % t == 0, (n, t)
    return t


def kernel(x, positions, mem, w_in, gla_gate_up, gla_gate_bias, gla_norm_g, w_out, ln_mix_g, ln_mix_b, xattn_w_q, xattn_w_k, xattn_w_v, xattn_w_o, ln_mem_g, ln_mem_b, peer_w_query, peer_sub_keys_1, peer_sub_keys_2, peer_expert_down, peer_expert_up, ln_ffn_g, ln_ffn_b):
    B, S, D = x.shape
    T = B * S
    depth = w_in.shape[0]
    alpha = (2.0 * depth) ** 0.25
    tm = _tile(S, TOKEN_TILE)
    h = x
    for l in range(depth):
        q, kt, v, qi, kit, misc, gq, gk, gv, la, gr = _proj(h, positions, w_in[l], gla_gate_up[l], gla_gate_bias[l], tm)
        y_dsa = _dsa(q, kt, v, qi, kit, misc, _tile(S, DSA_QUERY_BLOCK), _tile(S, DSA_KEY_CHUNK))
        y_gla = _gla(gq, gk, gv, la, gr, gla_norm_g[l], tm)
        h1 = _mix_out(h.reshape(T, D), y_dsa.reshape(T, W_DSA), y_gla.reshape(T, W_GV), w_out[l],
                      ln_mix_g[l], ln_mix_b[l], alpha, tm)
        km, vm = _mem_kv(mem, xattn_w_k[l], xattn_w_v[l])
        h2 = _xattn(h1.reshape(B, S, D), km, vm, xattn_w_q[l], xattn_w_o[l], ln_mem_g[l], ln_mem_b[l], alpha, tm)
        h2 = h2.reshape(T, D)
        rows_t, ne_t, gates_t = _route(h2, peer_w_query[l], peer_sub_keys_1[l], peer_sub_keys_2[l],
                                       _tile(T, ROUTE_TILE))
        flat = lambda a: a.T.reshape(-1)
        n_even = flat(ne_t)
        by_pos = rows_t.reshape(-1, SUBLANES, T).transpose(1, 2, 0).reshape(SUBLANES, -1)
        rows8 = [by_pos[p] for p in range(SUBLANES)]
        tb = _tile(T, PEER_TOKEN_BLOCK)
        coef_t = _peer_down(rows8, n_even, h2.reshape(T, SUBLANES, LANES), gates_t, _pack_table(peer_expert_down[l]), tb)
        ff = _peer_up(rows8, n_even, coef_t, _pack_table(peer_expert_up[l]), tb)
        h = _ffn_out(h2, ff.reshape(T, D), ln_ffn_g[l], ln_ffn_b[l], alpha, tm).reshape(B, S, D)
    return h
```
